```python
import math
import jax
import jax.numpy as jnp
from jax import lax
import numpy as np


D_MODEL = 1024
BATCH = 8
SEQ = 4096
DEPTH = 4

GRID_W = 64
CTX_LEN = 256
N_MIXERS = 3
EPS = 1e-6
CHUNK = 64
A_HEADS = 8
A_DK = 128
A_DV = 128
A_CONV = 5
A_QK = A_HEADS * A_DK
A_V = A_HEADS * A_DV
A_QKV = 2 * A_QK + A_V
A_IN = A_QKV + A_V + 4 * A_HEADS
POOL_WINDOWS = (2, 4, 8, 16)
POOL_GROUP = D_MODEL // len(POOL_WINDOWS)
C_HEADS = 4
C_DK = D_MODEL // C_HEADS
C_DV = 2 * C_DK
C_QK = C_HEADS * C_DK
C_V = C_HEADS * C_DV
C_IN = 2 * C_QK + 2 * C_V
ROPE_BASE = 10000.0
N_GROUPS = 4
EXPERTS_PER_GROUP = 8
N_EXPERTS = N_GROUPS * EXPERTS_PER_GROUP
TOP_K = 2
D_EXPERT = D_MODEL // 2
MOE_BLOCK = 256
N_A = (DEPTH + 2) // 3
N_B = (DEPTH + 1) // 3
N_C = DEPTH // 3

kernel_name = 'hybrid_dit_gdn_pool_retnet_hmoe'


def rmsnorm(x, g):
    xf = x.astype(jnp.float32)
    y = xf * lax.rsqrt(jnp.mean(xf * xf, axis=-1, keepdims=True) + EPS)
    return (y * g.astype(jnp.float32)).astype(x.dtype)


def modulate(x, shift, scale):
    return x * (1.0 + scale) + shift


def split_heads(t, n_heads):
    b, n, _ = t.shape
    return t.reshape(b, n, n_heads, -1).transpose(0, 2, 1, 3)


def merge_heads(t):
    b, h, n, d = t.shape
    return t.transpose(0, 2, 1, 3).reshape(b, n, h * d)


def flip_seq(t):
    return jnp.flip(t, axis=2)


def l2norm(t):
    return t * lax.rsqrt(jnp.sum(t * t, axis=-1, keepdims=True) + EPS)


def short_conv(x, w):
    ch = x.shape[-1]
    y = lax.conv_general_dilated(x, w[:, None, :].astype(x.dtype), window_strides=(1,),
                                 padding=[(A_CONV // 2, A_CONV // 2)],
                                 dimension_numbers=('NWC', 'WIO', 'NWC'), feature_group_count=ch)
    return jax.nn.silu(y)


def gated_delta_chunked(q, k, v, beta, g, state0):
    b, h, n_tok, dk = q.shape
    dv = v.shape[-1]
    n = n_tok // CHUNK

    def chunks(t):
        return t.reshape(t.shape[:2] + (n, CHUNK) + t.shape[3:])

    q, k, v, beta, g = chunks(q), chunks(k), chunks(v), chunks(beta), chunks(g)
    gcum = jnp.cumsum(g, axis=-1)
    idx = jnp.arange(CHUNK)
    incl = idx[:, None] >= idx[None, :]
    strict = idx[:, None] > idx[None, :]
    rel = gcum[..., :, None] - gcum[..., None, :]
    decay = jnp.where(incl, jnp.exp(jnp.where(incl, rel, 0.0)), 0.0)
    kk = jnp.einsum('bhnid,bhnjd->bhnij', k, k)
    a_mat = jnp.eye(CHUNK, dtype=q.dtype) + jnp.where(strict, beta[..., :, None] * kk * decay, 0.0)
    rhs = jnp.concatenate([v * beta[..., None], k * (beta * jnp.exp(gcum))[..., None]], axis=-1)
    sol = lax.linalg.triangular_solve(a_mat, rhs, left_side=True, lower=True, unit_diagonal=True)
    u, w = sol[..., :dv], sol[..., dv:]
    qk = jnp.einsum('bhnid,bhnjd->bhnij', q, k) * decay
    q_in = q * jnp.exp(gcum)[..., None]
    k_out = k * jnp.exp(gcum[..., -1:] - gcum)[..., None]
    chunk_decay = jnp.exp(gcum[..., -1])

    def step(s, xs):
        qk_c, q_c, k_c, u_c, w_c, d_c = xs
        v_new = u_c - jnp.einsum('bhid,bhde->bhie', w_c, s)
        o = jnp.einsum('bhid,bhde->bhie', q_c, s) + jnp.einsum('bhij,bhje->bhie', qk_c, v_new)
        s = s * d_c[..., None, None] + jnp.einsum('bhid,bhie->bhde', k_c, v_new)
        return s, o

    xs = tuple(jnp.moveaxis(t, 2, 0) for t in (qk, q_in, k_out, u, w, chunk_decay))
    state, o = lax.scan(step, state0, xs)
    return jnp.moveaxis(o, 0, 2).reshape(b, h, n_tok, dv), state


def retention_chunked(q, k, v, log_gamma, state0):
    b, h, n_tok, dk = q.shape
    dv = v.shape[-1]
    n = n_tok // CHUNK
    idx = jnp.arange(CHUNK, dtype=jnp.float32)
    lg = log_gamma[:, None]
    rel = idx[:, None] - idx[None, :]
    dmat = jnp.where(rel >= 0, jnp.exp(jnp.maximum(rel, 0.0) * lg[..., None]), 0.0)
    q_dec = jnp.exp((idx + 1.0) * lg)
    k_dec = jnp.exp((CHUNK - 1.0 - idx) * lg)
    c_dec = jnp.exp(CHUNK * log_gamma)

    def chunk_major(t):
        return jnp.moveaxis(t.reshape(b, h, n, CHUNK, t.shape[-1]), 2, 0)

    def step(s, xs):
        q_c, k_c, v_c = xs
        inner = jnp.einsum('bhid,bhjd->bhij', q_c, k_c) * dmat
        o = jnp.einsum('bhij,bhje->bhie', inner, v_c) + jnp.einsum('bhid,bhde->bhie', q_c * q_dec[..., None], s)
        s = s * c_dec[:, None, None] + jnp.einsum('bhjd,bhje->bhde', k_c * k_dec[..., None], v_c)
        return s, o

    state, o = lax.scan(step, state0, (chunk_major(q), chunk_major(k), chunk_major(v)))
    return jnp.moveaxis(o, 0, 2).reshape(b, h, n_tok, dv), state


def axial_rotary(t, rows, cols):
    d = t.shape[-1]
    half, quarter = d // 2, d // 4
    inv_freq = ROPE_BASE ** (-jnp.arange(quarter, dtype=jnp.float32) / quarter)

    def rot(u, pos):
        ang = pos[:, None] * inv_freq[None, :]
        cos, sin = jnp.cos(ang), jnp.sin(ang)
        u1, u2 = u[..., :quarter], u[..., quarter:]
        return jnp.concatenate([u1 * cos - u2 * sin, u1 * sin + u2 * cos], axis=-1)

    return jnp.concatenate([rot(t[..., :half], rows), rot(t[..., half:], cols)], axis=-1)


def gdn_project(h, w_in, conv_w):
    p = h @ w_in
    qkv = short_conv(p[..., :A_QKV], conv_w).astype(jnp.float32)
    q = l2norm(split_heads(qkv[..., :A_QK], A_HEADS)) * (A_DK ** -0.5)
    k = l2norm(split_heads(qkv[..., A_QK:2 * A_QK], A_HEADS))
    v = split_heads(qkv[..., 2 * A_QK:], A_HEADS)
    z = p[..., A_QKV:A_QKV + A_V]
    ab = p[..., A_QKV + A_V:].astype(jnp.float32).reshape(p.shape[:2] + (2, 2, A_HEADS))
    return q, k, v, z, ab


def gdn_gates(ab, a_log, dt_bias):
    a, bt = ab[:, :, 0], ab[:, :, 1]
    g = -jnp.exp(a_log.astype(jnp.float32)) * jax.nn.softplus(a + dt_bias.astype(jnp.float32))
    beta = jax.nn.sigmoid(bt)
    return jnp.transpose(g, (2, 0, 3, 1)), jnp.transpose(beta, (2, 0, 3, 1))


def gdn_out(o, z, norm_g, w_out):
    o = o * lax.rsqrt(jnp.mean(o * o, axis=-1, keepdims=True) + EPS) * norm_g.astype(jnp.float32)
    return (merge_heads(o).astype(z.dtype) * jax.nn.silu(z)) @ w_out


def mixer_gdn(h_ctx, h_lat, w_in, conv_w, a_log, dt_bias, norm_g, w_out, need_ctx):
    qc, kc, vc, zc, abc = gdn_project(h_ctx, w_in, conv_w)
    ql, kl, vl, zl, abl = gdn_project(h_lat, w_in, conv_w)
    g_c, beta_c = gdn_gates(abc, a_log, dt_bias)
    g_l, beta_l = gdn_gates(abl, a_log, dt_bias)
    s0 = jnp.zeros((h_lat.shape[0], A_HEADS, A_DK, A_DV), jnp.float32)
    o_cf, s_cf = gated_delta_chunked(qc, kc, vc, beta_c[0], g_c[0], s0)
    o_lf, _ = gated_delta_chunked(ql, kl, vl, beta_l[0], g_l[0], s_cf)
    o_cb, s_cb = gated_delta_chunked(flip_seq(qc), flip_seq(kc), flip_seq(vc),
                                     flip_seq(beta_c[1]), flip_seq(g_c[1]), s0)
    o_lb, _ = gated_delta_chunked(flip_seq(ql), flip_seq(kl), flip_seq(vl),
                                  flip_seq(beta_l[1]), flip_seq(g_l[1]), s_cb)
    y_lat = gdn_out(o_lf + flip_seq(o_lb), zl, norm_g, w_out)
    y_ctx = gdn_out(o_cf + flip_seq(o_cb), zc, norm_g, w_out) if need_ctx else None
    return y_ctx, y_lat


def multiscale_pool(h, w_group, b_group, scale):
    b, n_tok, d = h.shape
    hf = h.astype(jnp.float32)
    cs = jnp.concatenate([jnp.zeros((b, 1, d), jnp.float32), jnp.cumsum(hf, axis=1)], axis=1)
    pos = jnp.arange(n_tok)
    parts = []
    for gi, win in enumerate(POOL_WINDOWS):
        lo = jnp.clip(pos - win // 2, 0, n_tok)
        hi = jnp.clip(pos + win - win // 2, 0, n_tok)
        sl = slice(gi * POOL_GROUP, (gi + 1) * POOL_GROUP)
        cnt = (hi - lo).astype(jnp.float32)[:, None]
        mean = (cs[:, hi, sl] - cs[:, lo, sl]) / cnt
        parts.append(mean - hf[..., sl])
    pooled = jnp.concatenate(parts, axis=-1).astype(h.dtype).reshape(b, n_tok, len(POOL_WINDOWS), POOL_GROUP)
    y = jnp.einsum('bngc,gce->bnge', pooled, w_group) + b_group
    return y.reshape(b, n_tok, d) * scale


def mixer_pool(h_ctx, h_lat, w_group, b_group, scale, need_ctx):
    y_lat = multiscale_pool(h_lat, w_group, b_group, scale)
    y_ctx = multiscale_pool(h_ctx, w_group, b_group, scale) if need_ctx else None
    return y_ctx, y_lat


def retention_project(h, w_in, rows, cols, rotate):
    p = h @ w_in
    q = split_heads(p[..., :C_QK], C_HEADS).astype(jnp.float32)
    k = split_heads(p[..., C_QK:2 * C_QK], C_HEADS).astype(jnp.float32) * (C_DK ** -0.5)
    v = split_heads(p[..., 2 * C_QK:2 * C_QK + C_V], C_HEADS).astype(jnp.float32)
    gate = p[..., 2 * C_QK + C_V:]
    if rotate:
        q = axial_rotary(q, rows, cols)
        k = axial_rotary(k, rows, cols)
    return q, k, v, gate


def retention_out(o, gate, norm_g, w_out):
    mu = jnp.mean(o, axis=-1, keepdims=True)
    var = jnp.mean(jnp.square(o - mu), axis=-1, keepdims=True)
    o = (o - mu) * lax.rsqrt(var + EPS) * norm_g.astype(jnp.float32)
    return (jax.nn.silu(gate) * merge_heads(o).astype(gate.dtype)) @ w_out


def mixer_retention(h_ctx, h_lat, rows, cols, w_in, decay_logit, norm_g, w_out, need_ctx):
    qc, kc, vc, gc = retention_project(h_ctx, w_in, rows, cols, False)
    ql, kl, vl, gl = retention_project(h_lat, w_in, rows, cols, True)
    log_gamma = jax.nn.log_sigmoid(decay_logit.astype(jnp.float32))
    s0 = jnp.zeros((h_lat.shape[0], C_HEADS, C_DK, C_DV), jnp.float32)
    o_cf, s_cf = retention_chunked(qc, kc, vc, log_gamma[0], s0)
    o_lf, _ = retention_chunked(ql, kl, vl, log_gamma[0], s_cf)
    o_cb, s_cb = retention_chunked(flip_seq(qc), flip_seq(kc), flip_seq(vc), log_gamma[1], s0)
    o_lb, _ = retention_chunked(flip_seq(ql), flip_seq(kl), flip_seq(vl), log_gamma[1], s_cb)
    y_lat = retention_out(o_lf + flip_seq(o_lb), gl, norm_g, w_out)
    y_ctx = retention_out(o_cf + flip_seq(o_cb), gc, norm_g, w_out) if need_ctx else None
    return y_ctx, y_lat


def expert_dispatch(h, expert_idx, weights, w_gate, w_up, w_down):
    t_tok, d = h.shape
    n_assign = t_tok * TOP_K
    flat_e = expert_idx.reshape(n_assign)
    flat_tok = jnp.arange(n_assign, dtype=jnp.int32) // TOP_K
    flat_w = weights.reshape(n_assign)
    order = jnp.argsort(flat_e)
    e_sorted = flat_e[order]
    counts = jnp.zeros((N_EXPERTS,), jnp.int32).at[flat_e].add(1)
    padded = (counts + MOE_BLOCK - 1) // MOE_BLOCK * MOE_BLOCK
    start = jnp.cumsum(counts) - counts
    pad_end = jnp.cumsum(padded)
    pad_start = pad_end - padded
    dest = pad_start[e_sorted] + (jnp.arange(n_assign, dtype=jnp.int32) - start[e_sorted])
    n_blocks = -(-(n_assign + N_EXPERTS * (MOE_BLOCK - 1)) // MOE_BLOCK)
    n_rows = n_blocks * MOE_BLOCK
    row_tok = jnp.full((n_rows,), t_tok, jnp.int32).at[dest].set(flat_tok[order])
    row_w = jnp.zeros((n_rows,), h.dtype).at[dest].set(flat_w[order].astype(h.dtype))
    block_e = jnp.minimum(jnp.searchsorted(pad_end, jnp.arange(n_blocks, dtype=jnp.int32) * MOE_BLOCK,
                                           side='right'), N_EXPERTS - 1)
    h_pad = jnp.concatenate([h, jnp.zeros((1, d), h.dtype)], axis=0)
    xb = h_pad[row_tok].reshape(n_blocks, MOE_BLOCK, d)

    def expert_block(args):
        xblk, e = args
        return (jax.nn.silu(xblk @ w_gate[e]) * (xblk @ w_up[e])) @ w_down[e]

    yb = lax.map(expert_block, (xb, block_e))
    y = yb.reshape(n_rows, d) * row_w[:, None]
    return jnp.zeros((t_tok + 1, d), h.dtype).at[row_tok].add(y)[:t_tok]


def hier_moe(h, wg_r, bg_r, we_r, be_r, w_gate, w_up, w_down):
    t_tok = h.shape[0]
    hf = h.astype(jnp.float32)
    p_group = jax.nn.softmax(hf @ wg_r.astype(jnp.float32) + bg_r.astype(jnp.float32), axis=-1)
    g_sel = jnp.argmax(p_group, axis=-1)
    p_g = jnp.take_along_axis(p_group, g_sel[:, None], axis=-1)
    le = (hf @ we_r.astype(jnp.float32) + be_r.astype(jnp.float32)).reshape(t_tok, N_GROUPS, EXPERTS_PER_GROUP)
    le_sel = jnp.take_along_axis(le, g_sel[:, None, None], axis=1)[:, 0]
    p_e = jax.nn.softmax(le_sel, axis=-1)
    top_p, top_i = lax.top_k(p_e, TOP_K)
    weights = p_g * top_p / jnp.sum(top_p, axis=-1, keepdims=True)
    expert_idx = g_sel[:, None].astype(jnp.int32) * EXPERTS_PER_GROUP + top_i.astype(jnp.int32)
    return expert_dispatch(h, expert_idx, weights, w_gate, w_up, w_down)


def setup_inputs(seed: int = 0) -> dict:
    key = jax.random.key(seed)
    ks = iter(jax.random.split(key, 40))

    def nrm(shape, s):
        return jax.random.normal(next(ks), shape, jnp.float32) * s

    x = nrm((BATCH, SEQ, D_MODEL), 1.0)
    c = nrm((BATCH, D_MODEL), 1.0)
    ctx = nrm((BATCH, CTX_LEN, D_MODEL), 1.0)
    c_ctx = nrm((D_MODEL,), 1.0)
    mod_w = nrm((DEPTH, D_MODEL, 6 * D_MODEL), 0.5 * D_MODEL ** -0.5)
    mod_b = nrm((DEPTH, 6 * D_MODEL), 0.01)
    norm1_g = 1.0 + nrm((DEPTH, D_MODEL), 0.02)
    norm2_g = 1.0 + nrm((DEPTH, D_MODEL), 0.02)
    final_g = 1.0 + nrm((D_MODEL,), 0.02)
    gdn_w_in = nrm((N_A, D_MODEL, A_IN), D_MODEL ** -0.5)
    gdn_conv_w = nrm((N_A, A_CONV, A_QKV), A_CONV ** -0.5)
    gdn_a_log = jnp.log(jax.random.uniform(next(ks), (N_A, 2, A_HEADS), jnp.float32, 1.0, 16.0))
    dt = jnp.exp(jax.random.uniform(next(ks), (N_A, 2, A_HEADS), jnp.float32, math.log(1e-3), math.log(1e-1)))
    gdn_dt_bias = dt + jnp.log(-jnp.expm1(-dt))
    gdn_norm_g = 1.0 + nrm((N_A, A_DV), 0.02)
    gdn_w_out = nrm((N_A, A_V, D_MODEL), A_V ** -0.5)
    pool_w = nrm((N_B, len(POOL_WINDOWS), POOL_GROUP, POOL_GROUP), POOL_GROUP ** -0.5)
    pool_b = nrm((N_B, len(POOL_WINDOWS), POOL_GROUP), 0.01)
    pool_scale = 0.5 + nrm((N_B, D_MODEL), 0.05)
    ret_w_in = nrm((N_C, D_MODEL, C_IN), D_MODEL ** -0.5)
    ret_decay_logit = jnp.log(2.0 ** (5.0 + jnp.arange(C_HEADS, dtype=jnp.float32)) - 1.0) + nrm((N_C, 2, C_HEADS), 0.1)
    ret_norm_g = 1.0 + nrm((N_C, C_DV), 0.02)
    ret_w_out = nrm((N_C, C_V, D_MODEL), C_V ** -0.5)
    router_group_w = nrm((DEPTH, D_MODEL, N_GROUPS), D_MODEL ** -0.5)
    router_group_b = nrm((DEPTH, N_GROUPS), 0.01)
    router_expert_w = nrm((DEPTH, D_MODEL, N_EXPERTS), D_MODEL ** -0.5)
    router_expert_b = nrm((DEPTH, N_EXPERTS), 0.01)
    exp_w_gate = nrm((DEPTH, N_EXPERTS, D_MODEL, D_EXPERT), D_MODEL ** -0.5)
    exp_w_up = nrm((DEPTH, N_EXPERTS, D_MODEL, D_EXPERT), D_MODEL ** -0.5)
    exp_w_down = nrm((DEPTH, N_EXPERTS, D_EXPERT, D_MODEL), D_EXPERT ** -0.5)
    return {'x': x, 'c': c, 'ctx': ctx, 'c_ctx': c_ctx,
            'mod_w': mod_w, 'mod_b': mod_b, 'norm1_g': norm1_g, 'norm2_g': norm2_g, 'final_g': final_g,
            'gdn_w_in': gdn_w_in, 'gdn_conv_w': gdn_conv_w, 'gdn_a_log': gdn_a_log, 'gdn_dt_bias': gdn_dt_bias,
            'gdn_norm_g': gdn_norm_g, 'gdn_w_out': gdn_w_out,
            'pool_w': pool_w, 'pool_b': pool_b, 'pool_scale': pool_scale,
            'ret_w_in': ret_w_in, 'ret_decay_logit': ret_decay_logit, 'ret_norm_g': ret_norm_g, 'ret_w_out': ret_w_out,
            'router_group_w': router_group_w, 'router_group_b': router_group_b,
            'router_expert_w': router_expert_w, 'router_expert_b': router_expert_b,
            'exp_w_gate': exp_w_gate, 'exp_w_up': exp_w_up, 'exp_w_down': exp_w_down}


def reference(x, c, ctx, c_ctx, mod_w, mod_b, norm1_g, norm2_g, final_g,
              gdn_w_in, gdn_conv_w, gdn_a_log, gdn_dt_bias, gdn_norm_g, gdn_w_out,
              pool_w, pool_b, pool_scale,
              ret_w_in, ret_decay_logit, ret_norm_g, ret_w_out,
              router_group_w, router_group_b, router_expert_w, router_expert_b,
              exp_w_gate, exp_w_up, exp_w_down):
    b, n_tok, d = x.shape
    ctx_len = ctx.shape[1]
    n_rows = n_tok // GRID_W
    rows = jnp.repeat(jnp.arange(n_rows, dtype=jnp.float32), GRID_W)
    cols = jnp.tile(jnp.arange(GRID_W, dtype=jnp.float32), n_rows)
    silu_c = jax.nn.silu(c)
    silu_cc = jax.nn.silu(c_ctx)
    for i in range(DEPTH):
        last = i == DEPTH - 1
        j = i // N_MIXERS
        kind = i % N_MIXERS
        m_lat = (silu_c @ mod_w[i] + mod_b[i])[:, None, :]
        m_ctx = silu_cc @ mod_w[i] + mod_b[i]
        sh1, sc1, g1, sh2, sc2, g2 = jnp.split(m_lat, 6, axis=-1)
        csh1, csc1, cg1, csh2, csc2, cg2 = jnp.split(m_ctx, 6, axis=-1)
        h_lat = modulate(rmsnorm(x, norm1_g[i]), sh1, sc1)
        h_ctx = modulate(rmsnorm(ctx, norm1_g[i]), csh1, csc1)
        if kind == 0:
            y_ctx, y_lat = mixer_gdn(h_ctx, h_lat, gdn_w_in[j], gdn_conv_w[j], gdn_a_log[j], gdn_dt_bias[j],
                                     gdn_norm_g[j], gdn_w_out[j], not last)
        elif kind == 1:
            y_ctx, y_lat = mixer_pool(h_ctx, h_lat, pool_w[j], pool_b[j], pool_scale[j], not last)
        else:
            y_ctx, y_lat = mixer_retention(h_ctx, h_lat, rows, cols, ret_w_in[j], ret_decay_logit[j],
                                           ret_norm_g[j], ret_w_out[j], not last)
        x = x + g1 * y_lat
        moe_args = (router_group_w[i], router_group_b[i], router_expert_w[i], router_expert_b[i],
                    exp_w_gate[i], exp_w_up[i], exp_w_down[i])
        h2_lat = modulate(rmsnorm(x, norm2_g[i]), sh2, sc2)
        if not last:
            ctx = ctx + cg1 * y_ctx
            h2_ctx = modulate(rmsnorm(ctx, norm2_g[i]), csh2, csc2)
            h2 = jnp.concatenate([h2_ctx, h2_lat], axis=1).reshape(-1, d)
            y2 = hier_moe(h2, *moe_args).reshape(b, ctx_len + n_tok, d)
            ctx = ctx + cg2 * y2[:, :ctx_len]
            x = x + g2 * y2[:, ctx_len:]
        else:
            y2 = hier_moe(h2_lat.reshape(-1, d), *moe_args).reshape(b, n_tok, d)
            x = x + g2 * y2
    return rmsnorm(x, final_g)
```

```python
import functools
import math

import jax
import jax.numpy as jnp
from jax import lax
from jax.experimental import pallas as pl
from jax.experimental.pallas import tpu as pltpu

F32 = jnp.float32
BF16 = jnp.bfloat16
HIGHEST = lax.Precision.HIGHEST

EPS = 1e-6
TM = 256
HALO = 8
CHUNK = 64
GDN_HEADS = 8
GDN_DK = 128
GDN_CONV = 5
RET_HEADS = 4
RET_DK = 256
RET_DV = 512
ROPE_BASE = 10000.0
GRID_W = 64
POOL_WINDOWS = (2, 4, 8, 16)
POOL_GROUP = 256
N_GROUPS = 4
EXPERTS_PER_GROUP = 8
N_EXPERTS = 32
MOE_BLOCK = 256
DMA_WINDOW = 32
V7X_VMEM_LIMIT_BYTES = 56 * 1024 * 1024


def _params(*sem):
    return pltpu.CompilerParams(dimension_semantics=sem, vmem_limit_bytes=V7X_VMEM_LIMIT_BYTES)


def _dot(a, b):
    return jnp.dot(a.astype(BF16), b.astype(BF16), preferred_element_type=F32)


def _dot_hi(a, b):
    return jnp.dot(a, b, precision=HIGHEST, preferred_element_type=F32)


def _dot_nt(a, b):
    return lax.dot_general(a.astype(BF16), b.astype(BF16), (((1,), (1,)), ((), ())), preferred_element_type=F32)


def _dot_nt_hi(a, b):
    return lax.dot_general(a, b, (((1,), (1,)), ((), ())), precision=HIGHEST, preferred_element_type=F32)


def _dot_tn(a, b):
    return lax.dot_general(a.astype(BF16), b.astype(BF16), (((0,), (0,)), ((), ())), preferred_element_type=F32)


def _block_of(i, size):
    return jnp.right_shift(i, int(math.log2(size)))


def _silu(x):
    return x * jax.nn.sigmoid(x)


def _softplus(x):
    return jnp.maximum(x, 0.0) + jnp.log(1.0 + jnp.exp(-jnp.abs(x)))


def _rms_mod(x, g, shift, scale):
    y = x * lax.rsqrt(jnp.mean(x * x, axis=-1, keepdims=True) + EPS) * g
    return y * (1.0 + scale) + shift


def _mod_spec(grid_rank_prefix=0):
    def idx(*g):
        b, t = g[grid_rank_prefix], g[grid_rank_prefix + 1]
        return (b, jnp.minimum(t, 1), 0, 0)
    return idx


def _mod_kernel(c_ref, w_ref, b_ref, o_ref):
    o_ref[0] = _dot_hi(_silu(c_ref[...]), w_ref[0]) + b_ref[0]


def _modulation(cvec, mod_w, mod_b):
    n_layers, d, d6 = mod_w.shape
    return pl.pallas_call(
        _mod_kernel,
        grid=(n_layers, d6 // d),
        in_specs=[pl.BlockSpec((16, d), lambda l, j: (0, 0)),
                  pl.BlockSpec((1, d, d), lambda l, j: (l, 0, j)),
                  pl.BlockSpec((1, 1, d), lambda l, j: (l, 0, j))],
        out_specs=pl.BlockSpec((1, 16, d), lambda l, j: (l, 0, j)),
        out_shape=jax.ShapeDtypeStruct((n_layers, 16, d6), F32),
        compiler_params=_params("parallel", "parallel"),
        name="modulation",
    )(cvec, mod_w, mod_b.reshape(n_layers, 1, d6))


def _in_kernel(x_ref, mod_ref, g_ref, w_ref, o_ref):
    m = mod_ref[0, 0]
    h = _rms_mod(x_ref[0], g_ref[...], m[0:1], m[1:2])
    o_ref[0] = _dot(h, w_ref[...])


def _in_proj(x, modv, g, w_bf16, tn):
    b, s, d = x.shape
    n = w_bf16.shape[1]
    return pl.pallas_call(
        _in_kernel,
        grid=(n // tn, b, s // TM),
        in_specs=[pl.BlockSpec((1, TM, d), lambda j, bi, t: (bi, t, 0)),
                  pl.BlockSpec((1, 1, 6, d), _mod_spec(1)),
                  pl.BlockSpec((1, d), lambda j, bi, t: (0, 0)),
                  pl.BlockSpec((d, tn), lambda j, bi, t: (0, j))],
        out_specs=pl.BlockSpec((1, TM, tn), lambda j, bi, t: (bi, t, j)),
        out_shape=jax.ShapeDtypeStruct((b, s, n), F32),
        compiler_params=_params("parallel", "parallel", "parallel"),
        name="in_proj",
    )(x, modv, g.reshape(1, d), w_bf16)


def _gates_kernel(x_ref, mod_ref, g_ref, wab_ref, alog_ref, dtb_ref, gc_ref, bt_ref):
    m = mod_ref[0, 0]
    h = _rms_mod(x_ref[0], g_ref[...], m[0:1], m[1:2])
    ab = _dot_hi(h, wab_ref[...])
    nh = GDN_HEADS
    gate = -jnp.exp(alog_ref[...]) * _softplus(ab[:, :2 * nh] + dtb_ref[...])
    beta = jax.nn.sigmoid(ab[:, 2 * nh:])
    r = lax.broadcasted_iota(jnp.int32, (TM, TM), 0)
    c = lax.broadcasted_iota(jnp.int32, (TM, TM), 1)
    same = _block_of(r, CHUNK) == _block_of(c, CHUNK)
    cum_f = jnp.where(same & (c <= r), 1.0, 0.0)
    cum_b = jnp.where(same & (c >= r), 1.0, 0.0)
    gc_ref[0, 0] = _dot_hi(cum_f, gate[:, :nh])
    gc_ref[1, 0] = _dot_hi(cum_b, gate[:, nh:])
    bt_ref[0, 0] = beta[:, :nh]
    bt_ref[1, 0] = beta[:, nh:]


def _gdn_gates(x, modv, g, w_ab, a_log, dt_bias):
    b, s, d = x.shape
    nh = GDN_HEADS
    out = jax.ShapeDtypeStruct((2, b, s, nh), F32)
    return pl.pallas_call(
        _gates_kernel,
        grid=(b, s // TM),
        in_specs=[pl.BlockSpec((1, TM, d), lambda bi, t: (bi, t, 0)),
                  pl.BlockSpec((1, 1, 6, d), _mod_spec()),
                  pl.BlockSpec((1, d), lambda bi, t: (0, 0)),
                  pl.BlockSpec((d, 4 * nh), lambda bi, t: (0, 0)),
                  pl.BlockSpec((1, 2 * nh), lambda bi, t: (0, 0)),
                  pl.BlockSpec((1, 2 * nh), lambda bi, t: (0, 0))],
        out_specs=[pl.BlockSpec((2, 1, TM, nh), lambda bi, t: (0, bi, t, 0)),
                   pl.BlockSpec((2, 1, TM, nh), lambda bi, t: (0, bi, t, 0))],
        out_shape=[out, out],
        compiler_params=_params("parallel", "parallel"),
        name="gdn_gates",
    )(x, modv, g.reshape(1, d), w_ab, a_log.reshape(1, 2 * nh), dt_bias.reshape(1, 2 * nh))


def _halo_specs(width, col_of, n_tiles):
    per = TM // HALO

    def prev(bi, t, *rest):
        return (bi, jnp.maximum(t * per - 1, 0), col_of(*rest))

    def nxt(bi, t, *rest):
        return (bi, jnp.minimum((t + 1) * per, n_tiles * per - 1), col_of(*rest))

    return pl.BlockSpec((1, HALO, width), prev), pl.BlockSpec((1, HALO, width), nxt)


def _halo_valid():
    t = pl.program_id(1)
    nt = pl.num_programs(1)
    return t >= 2, (t >= 1) & (t < nt - 1)


def _conv_kernel(cur_ref, prev_ref, next_ref, w_ref, o_ref, ext_ref):
    j = pl.program_id(2)
    prev_ok, next_ok = _halo_valid()
    ext_ref[0:HALO] = jnp.where(prev_ok, prev_ref[0], 0.0)
    ext_ref[HALO:HALO + TM] = cur_ref[0]
    ext_ref[HALO + TM:] = jnp.where(next_ok, next_ref[0], 0.0)
    w = w_ref[...]
    base = HALO - GDN_CONV // 2
    acc = w[0:1] * ext_ref[base:base + TM]
    for k in range(1, GDN_CONV):
        acc = acc + w[k:k + 1] * ext_ref[base + k:base + k + TM]
    y = _silu(acc)
    width = y.shape[1]
    is_v = j >= 2 * (GDN_HEADS * GDN_DK // width)
    is_q = j < (GDN_HEADS * GDN_DK // width)
    qscale = jnp.where(is_q, GDN_DK ** -0.5, 1.0)
    for hh in range(width // GDN_DK):
        seg = y[:, hh * GDN_DK:(hh + 1) * GDN_DK]
        nrm = seg * lax.rsqrt(jnp.sum(seg * seg, axis=-1, keepdims=True) + EPS) * qscale
        o_ref[0, :, hh * GDN_DK:(hh + 1) * GDN_DK] = jnp.where(is_v, seg, nrm)


def _gdn_conv(p, conv_w):
    b, s, _ = p.shape
    n = conv_w.shape[1]
    width = 512
    prev_spec, next_spec = _halo_specs(width, lambda j: j, s // TM)
    return pl.pallas_call(
        _conv_kernel,
        grid=(b, s // TM, n // width),
        in_specs=[pl.BlockSpec((1, TM, width), lambda bi, t, j: (bi, t, j)),
                  prev_spec, next_spec,
                  pl.BlockSpec((GDN_CONV, width), lambda bi, t, j: (0, j))],
        out_specs=pl.BlockSpec((1, TM, width), lambda bi, t, j: (bi, t, j)),
        out_shape=jax.ShapeDtypeStruct((b, s, n), F32),
        scratch_shapes=[pltpu.VMEM((TM + 2 * HALO, width), F32)],
        compiler_params=_params("parallel", "parallel", "parallel"),
        name="gdn_conv",
    )(p, p, p, conv_w)


def _unit_tri_inverse(a, r, c):
    eye = jnp.where(r == c, 1.0, 0.0)
    d1 = jnp.where(_block_of(r, 8) == _block_of(c, 8), a, 0.0)
    d2 = _dot_hi(d1, d1)
    d4 = _dot_hi(d2, d2)
    t = _dot_hi(_dot_hi(eye - d1, eye + d2), eye + d4)
    for size in (16, 32, 64):
        join = (_block_of(r, size) == _block_of(c, size)) & (_block_of(r, size // 2) != _block_of(c, size // 2))
        t = t - _dot_hi(t, _dot_hi(jnp.where(join, a, 0.0), t))
    return t


def _gdn_scan_kernel(q_ref, k_ref, v_ref, gc_ref, gct_ref, bt_ref, o_ref, s_ref):
    d = pl.program_id(1)
    step = pl.program_id(2)

    @pl.when(step == 0)
    def _():
        s_ref[...] = jnp.zeros_like(s_ref)

    fwd = d == 0
    r = lax.broadcasted_iota(jnp.int32, (CHUNK, CHUNK), 0)
    c = lax.broadcasted_iota(jnp.int32, (CHUNK, CHUNK), 1)
    ahead = jnp.where(fwd, r - c, c - r)
    incl = ahead >= 0
    strict = ahead > 0
    gc_all = gc_ref[0, 0]
    gct_all = gct_ref[0, 0, 0]
    bt_all = bt_ref[0, 0]
    dk = GDN_DK
    for h in range(GDN_HEADS):
        q = q_ref[0, :, h * dk:(h + 1) * dk]
        k = k_ref[0, :, h * dk:(h + 1) * dk]
        v = v_ref[0, :, h * dk:(h + 1) * dk]
        gcol = gc_all[:, h:h + 1]
        grow = gct_all[h:h + 1, :]
        beta = bt_all[:, h:h + 1]
        glast = jnp.where(fwd, gcol[CHUNK - 1:CHUNK], gcol[0:1])
        decay = jnp.where(incl, jnp.exp(jnp.where(incl, gcol - grow, 0.0)), 0.0)
        kk = _dot_nt_hi(k, k)
        a = jnp.where(strict, beta * kk * decay, 0.0)
        t = _unit_tri_inverse(a, r, c)
        egc = jnp.exp(gcol)
        u = _dot_hi(t, v * beta)
        w = _dot_hi(t, k * (beta * egc))
        qk = _dot_nt(q, k) * decay
        s = s_ref[h]
        v_new = u - _dot(w, s)
        o_ref[0, 0, :, h * dk:(h + 1) * dk] = _dot(q * egc, s) + _dot(qk, v_new)
        s_ref[h] = s * jnp.exp(glast) + _dot_tn(k * jnp.exp(glast - gcol), v_new)


def _scan_tile(d, step, n_tiles, ctx_tiles):
    back = jnp.where(step < ctx_tiles, ctx_tiles - 1 - step, n_tiles + ctx_tiles - 1 - step)
    return jnp.where(d == 0, step, back)


def _gdn_scan(qkv, gc, gct, bt):
    b, s, _ = qkv.shape
    nh, dk = GDN_HEADS, GDN_DK
    nc = s // CHUNK
    ctx_chunks = TM // CHUNK
    width = nh * dk
    tile = functools.partial(_scan_tile, n_tiles=nc, ctx_tiles=ctx_chunks)
    return pl.pallas_call(
        _gdn_scan_kernel,
        grid=(b, 2, nc),
        in_specs=[pl.BlockSpec((1, CHUNK, width), lambda bi, d, i: (bi, tile(d, i), 0)),
                  pl.BlockSpec((1, CHUNK, width), lambda bi, d, i: (bi, tile(d, i), 1)),
                  pl.BlockSpec((1, CHUNK, width), lambda bi, d, i: (bi, tile(d, i), 2)),
                  pl.BlockSpec((1, 1, CHUNK, nh), lambda bi, d, i: (d, bi, tile(d, i), 0)),
                  pl.BlockSpec((1, 1, 1, nh, CHUNK), lambda bi, d, i: (d, bi, tile(d, i), 0, 0)),
                  pl.BlockSpec((1, 1, CHUNK, nh), lambda bi, d, i: (d, bi, tile(d, i), 0))],
        out_specs=pl.BlockSpec((1, 1, CHUNK, width), lambda bi, d, i: (d, bi, tile(d, i), 0)),
        out_shape=jax.ShapeDtypeStruct((2, b, s, width), F32),
        scratch_shapes=[pltpu.VMEM((nh, dk, dk), F32)],
        compiler_params=_params("parallel", "parallel", "arbitrary"),
        name="gdn_scan",
    )(qkv, qkv, qkv, gc, gct, bt)


def _gdn_out_kernel(o_ref, z_ref, x_ref, mod_ref, ng_ref, w_ref, out_ref):
    m = mod_ref[0, 0]
    o = o_ref[0, 0] + o_ref[1, 0]
    ng = ng_ref[...]
    parts = []
    for h in range(GDN_HEADS):
        seg = o[:, h * GDN_DK:(h + 1) * GDN_DK]
        parts.append(seg * lax.rsqrt(jnp.mean(seg * seg, axis=-1, keepdims=True) + EPS) * ng)
    y = _dot(jnp.concatenate(parts, axis=-1) * _silu(z_ref[0]), w_ref[...])
    out_ref[0] = x_ref[0] + m[2:3] * y


def _gdn_out(o, p, x, modv, norm_g, w_out_bf16):
    b, s, d = x.shape
    width = GDN_HEADS * GDN_DK
    return pl.pallas_call(
        _gdn_out_kernel,
        grid=(b, s // TM),
        in_specs=[pl.BlockSpec((2, 1, TM, width), lambda bi, t: (0, bi, t, 0)),
                  pl.BlockSpec((1, TM, width), lambda bi, t: (bi, t, 3)),
                  pl.BlockSpec((1, TM, d), lambda bi, t: (bi, t, 0)),
                  pl.BlockSpec((1, 1, 6, d), _mod_spec()),
                  pl.BlockSpec((1, GDN_DK), lambda bi, t: (0, 0)),
                  pl.BlockSpec((width, d), lambda bi, t: (0, 0))],
        out_specs=pl.BlockSpec((1, TM, d), lambda bi, t: (bi, t, 0)),
        out_shape=jax.ShapeDtypeStruct((b, s, d), F32),
        compiler_params=_params("parallel", "parallel"),
        name="gdn_out",
    )(o, p, x, modv, norm_g.reshape(1, GDN_DK), w_out_bf16)


def _mixer_gdn(x, modv, norm_g, w_in, conv_w, a_log, dt_bias, out_norm_g, w_out):
    nh, dk = GDN_HEADS, GDN_DK
    n_main = 4 * nh * dk
    p = _in_proj(x, modv, norm_g, w_in[:, :n_main].astype(BF16), 2048)
    gc, bt = _gdn_gates(x, modv, norm_g, w_in[:, n_main:], a_log, dt_bias)
    b, s, _ = x.shape
    gct = gc.reshape(2, b, s // CHUNK, CHUNK, nh).swapaxes(-1, -2)
    qkv = _gdn_conv(p, conv_w)
    o = _gdn_scan(qkv, gc, gct, bt)
    return _gdn_out(o, p, x, modv, out_norm_g, w_out.astype(BF16))


def _pool_kernel(x_ref, xp_ref, xn_ref, mod_ref, g_ref, w_ref, b_ref, sc_ref, o_ref, ext_ref):
    t = pl.program_id(1)
    nt = pl.num_programs(1)
    prev_ok, next_ok = _halo_valid()
    m = mod_ref[0, 0]
    g = g_ref[...]
    x = x_ref[0]
    h = _rms_mod(x, g, m[0:1], m[1:2])
    ext_ref[0:HALO] = jnp.where(prev_ok, _rms_mod(xp_ref[0], g, m[0:1], m[1:2]), 0.0)
    ext_ref[HALO:HALO + TM] = h
    ext_ref[HALO + TM:] = jnp.where(next_ok, _rms_mod(xn_ref[0], g, m[0:1], m[1:2]), 0.0)
    row = lax.broadcasted_iota(jnp.int32, (TM, 1), 0)
    pos = row + jnp.where(t == 0, 0, (t - 1) * TM)
    n_seq = jnp.where(t == 0, TM, (nt - 1) * TM)
    pg = POOL_GROUP
    for gi, win in enumerate(POOL_WINDOWS):
        lo_off = HALO - win // 2
        acc = ext_ref[lo_off:lo_off + TM, gi * pg:(gi + 1) * pg]
        for k in range(1, win):
            acc = acc + ext_ref[lo_off + k:lo_off + k + TM, gi * pg:(gi + 1) * pg]
        lo = jnp.clip(pos - win // 2, 0, n_seq)
        hi = jnp.clip(pos + win - win // 2, 0, n_seq)
        pooled = acc / (hi - lo).astype(F32) - h[:, gi * pg:(gi + 1) * pg]
        y = (_dot(pooled, w_ref[gi]) + b_ref[gi]) * sc_ref[:, gi * pg:(gi + 1) * pg]
        o_ref[0, :, gi * pg:(gi + 1) * pg] = x[:, gi * pg:(gi + 1) * pg] + m[2:3, gi * pg:(gi + 1) * pg] * y


def _mixer_pool(x, modv, norm_g, w_group, b_group, scale):
    b, s, d = x.shape
    ng, pg = len(POOL_WINDOWS), POOL_GROUP
    prev_spec, next_spec = _halo_specs(d, lambda: 0, s // TM)
    return pl.pallas_call(
        _pool_kernel,
        grid=(b, s // TM),
        in_specs=[pl.BlockSpec((1, TM, d), lambda bi, t: (bi, t, 0)),
                  prev_spec, next_spec,
                  pl.BlockSpec((1, 1, 6, d), _mod_spec()),
                  pl.BlockSpec((1, d), lambda bi, t: (0, 0)),
                  pl.BlockSpec((ng, pg, pg), lambda bi, t: (0, 0, 0)),
                  pl.BlockSpec((ng, 1, pg), lambda bi, t: (0, 0, 0)),
                  pl.BlockSpec((1, d), lambda bi, t: (0, 0))],
        out_specs=pl.BlockSpec((1, TM, d), lambda bi, t: (bi, t, 0)),
        out_shape=jax.ShapeDtypeStruct((b, s, d), F32),
        scratch_shapes=[pltpu.VMEM((TM + 2 * HALO, d), F32)],
        compiler_params=_params("parallel", "parallel"),
        name="pool_mixer",
    )(x, x, x, modv, norm_g.reshape(1, d), w_group.astype(BF16), b_group.reshape(ng, 1, pg), scale.reshape(1, d))


def _rotate(t, cos, sin_signed):
    half = RET_DK // 2
    swapped = jnp.concatenate([pltpu.roll(t[:, :half], half // 2, 1), pltpu.roll(t[:, half:], half // 2, 1)], axis=-1)
    return t * cos + swapped * sin_signed


def _ret_scan_kernel(lg_ref, q_ref, k_ref, v_ref, cos_ref, sin_ref, o_ref, s_ref):
    d = pl.program_id(1)
    step = pl.program_id(2)

    @pl.when(step == 0)
    def _():
        s_ref[...] = jnp.zeros_like(s_ref)

    fwd = d == 0
    r = lax.broadcasted_iota(jnp.int32, (TM, TM), 0)
    c = lax.broadcasted_iota(jnp.int32, (TM, TM), 1)
    rel = jnp.where(fwd, r - c, c - r).astype(F32)
    row = lax.broadcasted_iota(jnp.int32, (TM, 1), 0)
    q_pow = jnp.where(fwd, row + 1, TM - row).astype(F32)
    k_pow = jnp.where(fwd, TM - 1 - row, row).astype(F32)
    cos = cos_ref[...]
    sin = sin_ref[...]
    dk, dv = RET_DK, RET_DV
    for h in range(RET_HEADS):
        lg = jnp.full((1, 1), lg_ref[d, h], F32)
        q = _rotate(q_ref[0, :, h * dk:(h + 1) * dk], cos, sin)
        k = _rotate(k_ref[0, :, h * dk:(h + 1) * dk] * (dk ** -0.5), cos, sin)
        v = v_ref[0, :, h * dv:(h + 1) * dv]
        dmat = jnp.where(rel >= 0, jnp.exp(jnp.maximum(rel, 0.0) * lg), 0.0)
        inner = _dot_nt(q, k) * dmat
        s = s_ref[h]
        o_ref[0, 0, :, h * dv:(h + 1) * dv] = _dot(inner, v) + _dot(q * jnp.exp(q_pow * lg), s)
        s_ref[h] = s * jnp.exp(TM * lg) + _dot_tn(k * jnp.exp(k_pow * lg), v)


def _ret_scan(p, log_gamma, cos, sin):
    b, s, _ = p.shape
    nt = s // TM
    qw, vw = RET_HEADS * RET_DK, RET_HEADS * RET_DV
    tile = functools.partial(_scan_tile, n_tiles=nt, ctx_tiles=1)
    return pl.pallas_call(
        _ret_scan_kernel,
        grid=(b, 2, nt),
        in_specs=[pl.BlockSpec(memory_space=pltpu.SMEM),
                  pl.BlockSpec((1, TM, qw), lambda bi, d, i: (bi, tile(d, i), 0)),
                  pl.BlockSpec((1, TM, qw), lambda bi, d, i: (bi, tile(d, i), 1)),
                  pl.BlockSpec((1, TM, vw), lambda bi, d, i: (bi, tile(d, i), 1)),
                  pl.BlockSpec((TM, RET_DK), lambda bi, d, i: (tile(d, i), 0)),
                  pl.BlockSpec((TM, RET_DK), lambda bi, d, i: (tile(d, i), 0))],
        out_specs=pl.BlockSpec((1, 1, TM, vw), lambda bi, d, i: (d, bi, tile(d, i), 0)),
        out_shape=jax.ShapeDtypeStruct((2, b, s, vw), F32),
        scratch_shapes=[pltpu.VMEM((RET_HEADS, RET_DK, RET_DV), F32)],
        compiler_params=_params("parallel", "parallel", "arbitrary"),
        name="ret_scan",
    )(log_gamma, p, p, p, cos, sin)


def _ret_out_kernel(o_ref, gate_ref, x_ref, mod_ref, ng_ref, w_ref, out_ref):
    m = mod_ref[0, 0]
    o = o_ref[0, 0] + o_ref[1, 0]
    ng = ng_ref[...]
    parts = []
    for h in range(RET_HEADS):
        seg = o[:, h * RET_DV:(h + 1) * RET_DV]
        mu = jnp.mean(seg, axis=-1, keepdims=True)
        cen = seg - mu
        var = jnp.mean(cen * cen, axis=-1, keepdims=True)
        parts.append(cen * lax.rsqrt(var + EPS) * ng)
    y = _dot(_silu(gate_ref[0]) * jnp.concatenate(parts, axis=-1), w_ref[...])
    out_ref[0] = x_ref[0] + m[2:3] * y


def _ret_out(o, p, x, modv, norm_g, w_out_bf16):
    b, s, d = x.shape
    vw = RET_HEADS * RET_DV
    return pl.pallas_call(
        _ret_out_kernel,
        grid=(b, s // TM),
        in_specs=[pl.BlockSpec((2, 1, TM, vw), lambda bi, t: (0, bi, t, 0)),
                  pl.BlockSpec((1, TM, vw), lambda bi, t: (bi, t, 2)),
                  pl.BlockSpec((1, TM, d), lambda bi, t: (bi, t, 0)),
                  pl.BlockSpec((1, 1, 6, d), _mod_spec()),
                  pl.BlockSpec((1, RET_DV), lambda bi, t: (0, 0)),
                  pl.BlockSpec((vw, d), lambda bi, t: (0, 0))],
        out_specs=pl.BlockSpec((1, TM, d), lambda bi, t: (bi, t, 0)),
        out_shape=jax.ShapeDtypeStruct((b, s, d), F32),
        compiler_params=_params("parallel", "parallel"),
        name="ret_out",
    )(o, p, x, modv, norm_g.reshape(1, RET_DV), w_out_bf16)


def _rotary_tables(s):
    n_lat = s - TM
    pos = jnp.arange(n_lat, dtype=jnp.int32)
    rows = (pos // GRID_W).astype(F32)
    cols = (pos % GRID_W).astype(F32)
    quarter = RET_DK // 4
    inv_freq = ROPE_BASE ** (-jnp.arange(quarter, dtype=F32) / quarter)
    ang_r = rows[:, None] * inv_freq[None, :]
    ang_c = cols[:, None] * inv_freq[None, :]
    cos = jnp.concatenate([jnp.cos(ang_r), jnp.cos(ang_r), jnp.cos(ang_c), jnp.cos(ang_c)], axis=-1)
    sin = jnp.concatenate([-jnp.sin(ang_r), jnp.sin(ang_r), -jnp.sin(ang_c), jnp.sin(ang_c)], axis=-1)
    cos = jnp.concatenate([jnp.ones((TM, RET_DK), F32), cos], axis=0)
    sin = jnp.concatenate([jnp.zeros((TM, RET_DK), F32), sin], axis=0)
    return cos, sin


def _mixer_retention(x, modv, norm_g, w_in, decay_logit, out_norm_g, w_out):
    p = _in_proj(x, modv, norm_g, w_in.astype(BF16), 2048)
    cos, sin = _rotary_tables(x.shape[1])
    o = _ret_scan(p, jax.nn.log_sigmoid(decay_logit.astype(F32)), cos, sin)
    return _ret_out(o, p, x, modv, out_norm_g, w_out.astype(BF16))


ROUTE_LANES = 128


def _route_kernel(x_ref, mod_ref, g_ref, w_ref, b_ref, rt_ref, cnt_out_ref, cnt_ref):
    first = (pl.program_id(0) == 0) & (pl.program_id(1) == 0)

    @pl.when(first)
    def _():
        cnt_ref[...] = jnp.zeros_like(cnt_ref)

    m = mod_ref[0, 0]
    h = _rms_mod(x_ref[0], g_ref[...], m[3:4], m[4:5])
    logits = _dot_hi(h, w_ref[...]) + b_ref[...]
    lane = lax.broadcasted_iota(jnp.int32, (TM, ROUTE_LANES), 1)
    big = jnp.int32(ROUTE_LANES)
    neg = -jnp.inf
    glog = jnp.where((lane >= N_EXPERTS) & (lane < N_EXPERTS + N_GROUPS), logits, neg)
    gmax = jnp.max(glog, axis=-1, keepdims=True)
    gsel = jnp.min(jnp.where(glog == gmax, lane, big), axis=-1, keepdims=True) - N_EXPERTS
    p_group = 1.0 / jnp.sum(jnp.exp(glog - gmax), axis=-1, keepdims=True)
    elog = jnp.where((lane >= gsel * EXPERTS_PER_GROUP) & (lane < (gsel + 1) * EXPERTS_PER_GROUP), logits, neg)
    m1 = jnp.max(elog, axis=-1, keepdims=True)
    i1 = jnp.min(jnp.where(elog == m1, lane, big), axis=-1, keepdims=True)
    elog2 = jnp.where(lane == i1, neg, elog)
    m2 = jnp.max(elog2, axis=-1, keepdims=True)
    i2 = jnp.min(jnp.where(elog2 == m2, lane, big), axis=-1, keepdims=True)
    e2 = jnp.exp(m2 - m1)
    w1 = p_group / (1.0 + e2)
    w2 = p_group * e2 / (1.0 + e2)
    onehot = jnp.where((lane == i1) | (lane == i2), 1.0, 0.0)
    r = lax.broadcasted_iota(jnp.int32, (TM, TM), 0)
    c = lax.broadcasted_iota(jnp.int32, (TM, TM), 1)
    before = _dot(jnp.where(c < r, 1.0, 0.0), onehot) + cnt_ref[...]
    r1 = jnp.sum(jnp.where(lane == i1, before, 0.0), axis=-1, keepdims=True)
    r2 = jnp.sum(jnp.where(lane == i2, before, 0.0), axis=-1, keepdims=True)
    cnt_ref[...] = cnt_ref[...] + jnp.sum(onehot, axis=0, keepdims=True)
    cnt_out_ref[...] = cnt_ref[...]
    vals = (i1.astype(F32), i2.astype(F32), w1, w2, r1, r2)
    out = jnp.zeros((TM, ROUTE_LANES), F32)
    for pos_, val in enumerate(vals):
        out = jnp.where(lane == pos_, val, out)
    rt_ref[0] = out


def _route(x, modv, norm_g, w_route, b_route):
    b, s, d = x.shape
    return pl.pallas_call(
        _route_kernel,
        grid=(b, s // TM),
        in_specs=[pl.BlockSpec((1, TM, d), lambda bi, t: (bi, t, 0)),
                  pl.BlockSpec((1, 1, 6, d), _mod_spec()),
                  pl.BlockSpec((1, d), lambda bi, t: (0, 0)),
                  pl.BlockSpec((d, ROUTE_LANES), lambda bi, t: (0, 0)),
                  pl.BlockSpec((1, ROUTE_LANES), lambda bi, t: (0, 0))],
        out_specs=[pl.BlockSpec((1, TM, ROUTE_LANES), lambda bi, t: (bi, t, 0)),
                   pl.BlockSpec((1, ROUTE_LANES), lambda bi, t: (0, 0))],
        out_shape=[jax.ShapeDtypeStruct((b, s, ROUTE_LANES), F32), jax.ShapeDtypeStruct((1, ROUTE_LANES), F32)],
        scratch_shapes=[pltpu.VMEM((1, ROUTE_LANES), F32)],
        compiler_params=_params("arbitrary", "arbitrary"),
        name="moe_route",
    )(x, modv, norm_g.reshape(1, d), w_route, b_route)


def _row_copy_window(n_rows, make_copy):
    def body(i, carry):
        @pl.when(i >= DMA_WINDOW)
        def _():
            for k in range(2):
                make_copy(i - DMA_WINDOW, k).wait()

        @pl.when(i < n_rows)
        def _():
            for k in range(2):
                make_copy(i, k).start()
        return carry

    lax.fori_loop(0, n_rows + DMA_WINDOW, body, 0)


def _dispatch_kernel(dest_ref, x_ref, mod_ref, g_ref, zero_ref, xb_ref, h_ref, sem):
    del zero_ref
    m = mod_ref[0, 0]
    h_ref[...] = _rms_mod(x_ref[0], g_ref[...], m[3:4], m[4:5])

    def make_copy(i, k):
        return pltpu.make_async_copy(h_ref.at[pl.ds(i, 1)], xb_ref.at[pl.ds(dest_ref[0, 0, 2 * i + k], 1)],
                                     sem.at[jnp.bitwise_and(i, DMA_WINDOW - 1), k])

    _row_copy_window(TM, make_copy)


def _dispatch(x, modv, norm_g, dest, n_rows):
    b, s, d = x.shape
    nt = s // TM
    return pl.pallas_call(
        _dispatch_kernel,
        grid=(b, nt),
        in_specs=[pl.BlockSpec((1, 1, 2 * TM), lambda bi, t: (bi * nt + t, 0, 0), memory_space=pltpu.SMEM),
                  pl.BlockSpec((1, TM, d), lambda bi, t: (bi, t, 0)),
                  pl.BlockSpec((1, 1, 6, d), _mod_spec()),
                  pl.BlockSpec((1, d), lambda bi, t: (0, 0)),
                  pl.BlockSpec(memory_space=pl.ANY)],
        out_specs=pl.BlockSpec(memory_space=pl.ANY),
        out_shape=jax.ShapeDtypeStruct((n_rows, d), F32),
        input_output_aliases={4: 0},
        scratch_shapes=[pltpu.VMEM((TM, d), F32), pltpu.SemaphoreType.DMA((DMA_WINDOW, 2))],
        compiler_params=_params("arbitrary", "arbitrary"),
        name="moe_dispatch",
    )(dest.reshape(b * nt, 1, 2 * TM), x, modv, norm_g.reshape(1, d), jnp.zeros((n_rows, d), F32))


def _expert_kernel(be_ref, nu_ref, x_ref, wg_ref, wu_ref, wd_ref, o_ref, wg_bf, wu_bf, wd_bf):
    i = pl.program_id(0)

    @pl.when(i < nu_ref[0])
    def _():
        @pl.when((i == 0) | (be_ref[i] != be_ref[jnp.maximum(i - 1, 0)]))
        def _():
            wg_bf[...] = wg_ref[0, 0].astype(BF16)
            wu_bf[...] = wu_ref[0, 0].astype(BF16)
            wd_bf[...] = wd_ref[0, 0].astype(BF16)

        xb = x_ref[...].astype(BF16)
        act = _silu(jnp.dot(xb, wg_bf[...], preferred_element_type=F32)) * jnp.dot(xb, wu_bf[...], preferred_element_type=F32)
        o_ref[...] = jnp.dot(act.astype(BF16), wd_bf[...], preferred_element_type=F32)

    @pl.when(i >= nu_ref[0])
    def _():
        o_ref[...] = jnp.zeros_like(o_ref)


def _experts(xb, block_e, n_used, layer, w_gate, w_up, w_down):
    n_rows, d = xb.shape
    de = w_gate.shape[-1]
    nb = n_rows // MOE_BLOCK

    def last_used(i, nu):
        return jnp.minimum(i, nu[0] - 1)

    grid_spec = pltpu.PrefetchScalarGridSpec(
        num_scalar_prefetch=2,
        grid=(nb,),
        in_specs=[pl.BlockSpec((MOE_BLOCK, d), lambda i, be, nu: (last_used(i, nu), 0)),
                  pl.BlockSpec((1, 1, d, de), lambda i, be, nu: (layer, be[last_used(i, nu)], 0, 0)),
                  pl.BlockSpec((1, 1, d, de), lambda i, be, nu: (layer, be[last_used(i, nu)], 0, 0)),
                  pl.BlockSpec((1, 1, de, d), lambda i, be, nu: (layer, be[last_used(i, nu)], 0, 0))],
        out_specs=pl.BlockSpec((MOE_BLOCK, d), lambda i, be, nu: (i, 0)),
        scratch_shapes=[pltpu.VMEM((d, de), BF16), pltpu.VMEM((d, de), BF16), pltpu.VMEM((de, d), BF16)],
    )
    return pl.pallas_call(
        _expert_kernel,
        grid_spec=grid_spec,
        out_shape=jax.ShapeDtypeStruct((n_rows, d), F32),
        compiler_params=_params("arbitrary"),
        name="moe_experts",
    )(block_e, n_used, xb, w_gate, w_up, w_down)


def _combine_kernel(dest_ref, x_ref, mod_ref, rt_ref, fg_ref, yb_ref, o_ref, y_ref, sem, *, final_norm):
    def make_copy(i, k):
        return pltpu.make_async_copy(yb_ref.at[pl.ds(dest_ref[0, 0, 2 * i + k], 1)], y_ref.at[k, pl.ds(i, 1)],
                                     sem.at[jnp.bitwise_and(i, DMA_WINDOW - 1), k])

    _row_copy_window(TM, make_copy)
    m = mod_ref[0, 0]
    rt = rt_ref[0]
    out = x_ref[0] + m[5:6] * (rt[:, 2:3] * y_ref[0] + rt[:, 3:4] * y_ref[1])
    if final_norm:
        out = out * lax.rsqrt(jnp.mean(out * out, axis=-1, keepdims=True) + EPS) * fg_ref[...]
    o_ref[0] = out


def _combine(x, modv, rt, dest, yb, final_g, final_norm):
    b, s, d = x.shape
    nt = s // TM
    skip = 1 if final_norm else 0
    return pl.pallas_call(
        functools.partial(_combine_kernel, final_norm=final_norm),
        grid=(b, nt - skip),
        in_specs=[pl.BlockSpec((1, 1, 2 * TM), lambda bi, t: (bi * nt + t + skip, 0, 0), memory_space=pltpu.SMEM),
                  pl.BlockSpec((1, TM, d), lambda bi, t: (bi, t + skip, 0)),
                  pl.BlockSpec((1, 1, 6, d), lambda bi, t: (bi, jnp.minimum(t + skip, 1), 0, 0)),
                  pl.BlockSpec((1, TM, ROUTE_LANES), lambda bi, t: (bi, t + skip, 0)),
                  pl.BlockSpec((1, d), lambda bi, t: (0, 0)),
                  pl.BlockSpec(memory_space=pl.ANY)],
        out_specs=pl.BlockSpec((1, TM, d), lambda bi, t: (bi, t, 0)),
        out_shape=jax.ShapeDtypeStruct((b, s - skip * TM, d), F32),
        scratch_shapes=[pltpu.VMEM((2, TM, d), F32), pltpu.SemaphoreType.DMA((DMA_WINDOW, 2))],
        compiler_params=_params("arbitrary", "arbitrary"),
        name="moe_combine",
    )(dest.reshape(b * nt, 1, 2 * TM), x, modv, rt, final_g.reshape(1, d), yb)


def _hier_moe(x, modv, norm_g, layer, wg_r, bg_r, we_r, be_r, w_gate, w_up, w_down, final_g, final_norm):
    b, s, d = x.shape
    n_tok = b * s
    pad = ROUTE_LANES - N_EXPERTS - N_GROUPS
    w_route = jnp.concatenate([we_r, wg_r, jnp.zeros((d, pad), F32)], axis=1)
    b_route = jnp.concatenate([be_r, bg_r, jnp.zeros((pad,), F32)]).reshape(1, ROUTE_LANES)
    rt, cnt = _route(x, modv, norm_g, w_route, b_route)
    counts = cnt[0, :N_EXPERTS].astype(jnp.int32)
    padded = (counts + MOE_BLOCK - 1) // MOE_BLOCK * MOE_BLOCK
    pad_end = jnp.cumsum(padded)
    pad_start = pad_end - padded
    n_blocks = -(-(2 * n_tok + N_EXPERTS * (MOE_BLOCK - 1)) // MOE_BLOCK)
    rt2 = rt.reshape(n_tok, ROUTE_LANES)
    expert = rt2[:, 0:2].astype(jnp.int32)
    dest = (pad_start[expert] + rt2[:, 4:6].astype(jnp.int32)).reshape(-1)
    block_e = jnp.minimum(jnp.searchsorted(pad_end, jnp.arange(n_blocks, dtype=jnp.int32) * MOE_BLOCK, side='right'),
                          N_EXPERTS - 1).astype(jnp.int32)
    n_used = (pad_end[-1:] // MOE_BLOCK).astype(jnp.int32)
    xb = _dispatch(x, modv, norm_g, dest, n_blocks * MOE_BLOCK)
    yb = _experts(xb, block_e, n_used, layer, w_gate, w_up, w_down)
    return _combine(x, modv, rt, dest, yb, final_g, final_norm)


def kernel(x, c, ctx, c_ctx, mod_w, mod_b, norm1_g, norm2_g, final_g, gdn_w_in, gdn_conv_w, gdn_a_log, gdn_dt_bias, gdn_norm_g, gdn_w_out, pool_w, pool_b, pool_scale, ret_w_in, ret_decay_logit, ret_norm_g, ret_w_out, router_group_w, router_group_b, router_expert_w, router_expert_b, exp_w_gate, exp_w_up, exp_w_down):
    b, n_lat, d = x.shape
    depth = mod_w.shape[0]
    assert ctx.shape[1] == TM and n_lat % TM == 0 and b < 16
    xs = jnp.concatenate([ctx, x], axis=1)
    cvec = jnp.concatenate([c, c_ctx[None], jnp.zeros((15 - b, d), F32)], axis=0)
    mods = _modulation(cvec, mod_w, mod_b)
    for i in range(depth):
        j, kind = i // 3, i % 3
        lat = mods[i, :b].reshape(b, 1, 6, d)
        con = jnp.broadcast_to(mods[i, b].reshape(1, 1, 6, d), (b, 1, 6, d))
        modv = jnp.concatenate([con, lat], axis=1)
        if kind == 0:
            xs = _mixer_gdn(xs, modv, norm1_g[i], gdn_w_in[j], gdn_conv_w[j], gdn_a_log[j], gdn_dt_bias[j],
                            gdn_norm_g[j], gdn_w_out[j])
        elif kind == 1:
            xs = _mixer_pool(xs, modv, norm1_g[i], pool_w[j], pool_b[j], pool_scale[j])
        else:
            xs = _mixer_retention(xs, modv, norm1_g[i], ret_w_in[j], ret_decay_logit[j], ret_norm_g[j], ret_w_out[j])
        xs = _hier_moe(xs, modv, norm2_g[i], i, router_group_w[i], router_group_b[i], router_expert_w[i],
                       router_expert_b[i], exp_w_gate, exp_w_up, exp_w_down, final_g, i == depth - 1)
    return xs
```

```python
import functools
import math

import jax
import jax.numpy as jnp
from jax import lax
from jax.experimental import pallas as pl
from jax.experimental.pallas import tpu as pltpu

F32 = jnp.float32
BF16 = jnp.bfloat16
HIGHEST = lax.Precision.HIGHEST

EPS = 1e-6
TM = 256
HALO = 8
CHUNK = 64
GDN_HEADS = 8
GDN_DK = 128
GDN_CONV = 5
RET_HEADS = 4
RET_DK = 256
RET_DV = 512
ROPE_BASE = 10000.0
GRID_W = 64
POOL_WINDOWS = (2, 4, 8, 16)
POOL_GROUP = 256
N_GROUPS = 4
EXPERTS_PER_GROUP = 8
N_EXPERTS = 32
MOE_BLOCK = 256
DMA_WINDOW = 32
V7X_VMEM_LIMIT_BYTES = 56 * 1024 * 1024


def _params(*sem):
    return pltpu.CompilerParams(dimension_semantics=sem, vmem_limit_bytes=V7X_VMEM_LIMIT_BYTES)


def _dot(a, b):
    return jnp.dot(a.astype(BF16), b.astype(BF16), preferred_element_type=F32)


def _dot_hi(a, b):
    return jnp.dot(a, b, precision=HIGHEST, preferred_element_type=F32)


def _dot_nt(a, b):
    return lax.dot_general(a.astype(BF16), b.astype(BF16), (((1,), (1,)), ((), ())), preferred_element_type=F32)


def _dot_nt_hi(a, b):
    return lax.dot_general(a, b, (((1,), (1,)), ((), ())), precision=HIGHEST, preferred_element_type=F32)


def _dot_tn(a, b):
    return lax.dot_general(a.astype(BF16), b.astype(BF16), (((0,), (0,)), ((), ())), preferred_element_type=F32)


def _block_of(i, size):
    return jnp.right_shift(i, int(math.log2(size)))


def _silu(x):
    return x * jax.nn.sigmoid(x)


def _softplus(x):
    return jnp.maximum(x, 0.0) + jnp.log(1.0 + jnp.exp(-jnp.abs(x)))


def _rms_mod(x, g, shift, scale):
    y = x * lax.rsqrt(jnp.mean(x * x, axis=-1, keepdims=True) + EPS) * g
    return y * (1.0 + scale) + shift


def _mod_spec(grid_rank_prefix=0):
    def idx(*g):
        b, t = g[grid_rank_prefix], g[grid_rank_prefix + 1]
        return (b, jnp.minimum(t, 1), 0, 0)
    return idx


def _mod_kernel(c_ref, w_ref, b_ref, o_ref):
    o_ref[0] = _dot_hi(_silu(c_ref[...]), w_ref[0]) + b_ref[0]


def _modulation(cvec, mod_w, mod_b):
    n_layers, d, d6 = mod_w.shape
    return pl.pallas_call(
        _mod_kernel,
        grid=(n_layers, d6 // d),
        in_specs=[pl.BlockSpec((16, d), lambda l, j: (0, 0)),
                  pl.BlockSpec((1, d, d), lambda l, j: (l, 0, j)),
                  pl.BlockSpec((1, 1, d), lambda l, j: (l, 0, j))],
        out_specs=pl.BlockSpec((1, 16, d), lambda l, j: (l, 0, j)),
        out_shape=jax.ShapeDtypeStruct((n_layers, 16, d6), F32),
        compiler_params=_params("parallel", "parallel"),
        name="modulation",
    )(cvec, mod_w, mod_b.reshape(n_layers, 1, d6))


def _in_kernel(x_ref, mod_ref, g_ref, w_ref, o_ref):
    m = mod_ref[0, 0]
    h = _rms_mod(x_ref[0], g_ref[...], m[0:1], m[1:2])
    o_ref[0] = _dot(h, w_ref[...])


def _in_proj(x, modv, g, w_bf16, tn):
    b, s, d = x.shape
    n = w_bf16.shape[1]
    return pl.pallas_call(
        _in_kernel,
        grid=(n // tn, b, s // TM),
        in_specs=[pl.BlockSpec((1, TM, d), lambda j, bi, t: (bi, t, 0)),
                  pl.BlockSpec((1, 1, 6, d), _mod_spec(1)),
                  pl.BlockSpec((1, d), lambda j, bi, t: (0, 0)),
                  pl.BlockSpec((d, tn), lambda j, bi, t: (0, j))],
        out_specs=pl.BlockSpec((1, TM, tn), lambda j, bi, t: (bi, t, j)),
        out_shape=jax.ShapeDtypeStruct((b, s, n), F32),
        compiler_params=_params("parallel", "parallel", "parallel"),
        name="in_proj",
    )(x, modv, g.reshape(1, d), w_bf16)


def _gates_kernel(x_ref, mod_ref, g_ref, wab_ref, alog_ref, dtb_ref, gc_ref, bt_ref):
    m = mod_ref[0, 0]
    h = _rms_mod(x_ref[0], g_ref[...], m[0:1], m[1:2])
    ab = _dot_hi(h, wab_ref[...])
    nh = GDN_HEADS
    gate = -jnp.exp(alog_ref[...]) * _softplus(ab[:, :2 * nh] + dtb_ref[...])
    beta = jax.nn.sigmoid(ab[:, 2 * nh:])
    r = lax.broadcasted_iota(jnp.int32, (TM, TM), 0)
    c = lax.broadcasted_iota(jnp.int32, (TM, TM), 1)
    same = _block_of(r, CHUNK) == _block_of(c, CHUNK)
    cum_f = jnp.where(same & (c <= r), 1.0, 0.0)
    cum_b = jnp.where(same & (c >= r), 1.0, 0.0)
    gc_ref[0, 0] = _dot_hi(cum_f, gate[:, :nh])
    gc_ref[1, 0] = _dot_hi(cum_b, gate[:, nh:])
    bt_ref[0, 0] = beta[:, :nh]
    bt_ref[1, 0] = beta[:, nh:]


def _gdn_gates(x, modv, g, w_ab, a_log, dt_bias):
    b, s, d = x.shape
    nh = GDN_HEADS
    out = jax.ShapeDtypeStruct((2, b, s, nh), F32)
    return pl.pallas_call(
        _gates_kernel,
        grid=(b, s // TM),
        in_specs=[pl.BlockSpec((1, TM, d), lambda bi, t: (bi, t, 0)),
                  pl.BlockSpec((1, 1, 6, d), _mod_spec()),
                  pl.BlockSpec((1, d), lambda bi, t: (0, 0)),
                  pl.BlockSpec((d, 4 * nh), lambda bi, t: (0, 0)),
                  pl.BlockSpec((1, 2 * nh), lambda bi, t: (0, 0)),
                  pl.BlockSpec((1, 2 * nh), lambda bi, t: (0, 0))],
        out_specs=[pl.BlockSpec((2, 1, TM, nh), lambda bi, t: (0, bi, t, 0)),
                   pl.BlockSpec((2, 1, TM, nh), lambda bi, t: (0, bi, t, 0))],
        out_shape=[out, out],
        compiler_params=_params("parallel", "parallel"),
        name="gdn_gates",
    )(x, modv, g.reshape(1, d), w_ab, a_log.reshape(1, 2 * nh), dt_bias.reshape(1, 2 * nh))


def _halo_specs(width, col_of, n_tiles):
    per = TM // HALO

    def prev(bi, t, *rest):
        return (bi, jnp.maximum(t * per - 1, 0), col_of(*rest))

    def nxt(bi, t, *rest):
        return (bi, jnp.minimum((t + 1) * per, n_tiles * per - 1), col_of(*rest))

    return pl.BlockSpec((1, HALO, width), prev), pl.BlockSpec((1, HALO, width), nxt)


def _halo_valid():
    t = pl.program_id(1)
    nt = pl.num_programs(1)
    return t >= 2, (t >= 1) & (t < nt - 1)


def _conv_kernel(cur_ref, prev_ref, next_ref, w_ref, o_ref, ext_ref):
    j = pl.program_id(2)
    prev_ok, next_ok = _halo_valid()
    ext_ref[0:HALO] = jnp.where(prev_ok, prev_ref[0], 0.0)
    ext_ref[HALO:HALO + TM] = cur_ref[0]
    ext_ref[HALO + TM:] = jnp.where(next_ok, next_ref[0], 0.0)
    w = w_ref[...]
    base = HALO - GDN_CONV // 2
    acc = w[0:1] * ext_ref[base:base + TM]
    for k in range(1, GDN_CONV):
        acc = acc + w[k:k + 1] * ext_ref[base + k:base + k + TM]
    y = _silu(acc)
    width = y.shape[1]
    is_v = j >= 2 * (GDN_HEADS * GDN_DK // width)
    is_q = j < (GDN_HEADS * GDN_DK // width)
    qscale = jnp.where(is_q, GDN_DK ** -0.5, 1.0)
    for hh in range(width // GDN_DK):
        seg = y[:, hh * GDN_DK:(hh + 1) * GDN_DK]
        nrm = seg * lax.rsqrt(jnp.sum(seg * seg, axis=-1, keepdims=True) + EPS) * qscale
        o_ref[0, :, hh * GDN_DK:(hh + 1) * GDN_DK] = jnp.where(is_v, seg, nrm)


def _gdn_conv(p, conv_w):
    b, s, _ = p.shape
    n = conv_w.shape[1]
    width = 512
    prev_spec, next_spec = _halo_specs(width, lambda j: j, s // TM)
    return pl.pallas_call(
        _conv_kernel,
        grid=(b, s // TM, n // width),
        in_specs=[pl.BlockSpec((1, TM, width), lambda bi, t, j: (bi, t, j)),
                  prev_spec, next_spec,
                  pl.BlockSpec((GDN_CONV, width), lambda bi, t, j: (0, j))],
        out_specs=pl.BlockSpec((1, TM, width), lambda bi, t, j: (bi, t, j)),
        out_shape=jax.ShapeDtypeStruct((b, s, n), F32),
        scratch_shapes=[pltpu.VMEM((TM + 2 * HALO, width), F32)],
        compiler_params=_params("parallel", "parallel", "parallel"),
        name="gdn_conv",
    )(p, p, p, conv_w)


GDN_PAIR = 2 * GDN_DK
GDN_PAIRS = GDN_HEADS // 2
INV_LANES = 128


def _pair_cols(cols, hp, width):
    lane = lax.broadcasted_iota(jnp.int32, (cols.shape[0], width), 1)
    return jnp.where(lane < width // 2, cols[:, 2 * hp:2 * hp + 1], cols[:, 2 * hp + 1:2 * hp + 2])


def _pair_blockdiag_rows(x, lane_block):
    n, w = x.shape
    r = lax.broadcasted_iota(jnp.int32, (2 * n, w), 0)
    c = lax.broadcasted_iota(jnp.int32, (2 * n, w), 1)
    same = _block_of(r, n) == jnp.bitwise_and(_block_of(c, lane_block), 1)
    return jnp.where(same, jnp.concatenate([x, x], axis=0), 0.0)


def _pair_decay(gc_cols, gct_row, hp, fwd, inclusive):
    r = lax.broadcasted_iota(jnp.int32, (CHUNK, 2 * CHUNK), 0)
    c = jnp.bitwise_and(lax.broadcasted_iota(jnp.int32, (CHUNK, 2 * CHUNK), 1), CHUNK - 1)
    ahead = (r - c) if fwd else (c - r)
    keep = (ahead >= 0) if inclusive else (ahead > 0)
    rel = _pair_cols(gc_cols, hp, 2 * CHUNK) - gct_row
    return jnp.where(keep, jnp.exp(jnp.where(keep, rel, 0.0)), 0.0)


def _gdn_a_kernel(k_ref, gc_ref, gctp_ref, bt_ref, a_ref):
    for ch in range(TM // CHUNK):
        rows = pl.ds(ch * CHUNK, CHUNK)
        for hp in range(GDN_PAIRS):
            kp = k_ref[0, rows, hp * GDN_PAIR:(hp + 1) * GDN_PAIR]
            kk = _dot_nt_hi(kp, _pair_blockdiag_rows(kp, GDN_DK))
            for d in range(2):
                decay = _pair_decay(gc_ref[d, 0, rows, :], gctp_ref[d, 0, ch, hp:hp + 1, :], hp, d == 0, False)
                a_ref[d, 0, ch, hp] = _pair_cols(bt_ref[d, 0, rows, :], hp, 2 * CHUNK) * kk * decay


def _gdn_a(qkv, gc, gctp, bt):
    b, s, _ = qkv.shape
    nh = GDN_HEADS
    cpt = TM // CHUNK
    gate_spec = pl.BlockSpec((2, 1, TM, nh), lambda bi, t: (0, bi, t, 0))
    return pl.pallas_call(
        _gdn_a_kernel,
        grid=(b, s // TM),
        in_specs=[pl.BlockSpec((1, TM, nh * GDN_DK), lambda bi, t: (bi, t, 1)),
                  gate_spec,
                  pl.BlockSpec((2, 1, cpt, GDN_PAIRS, 2 * CHUNK), lambda bi, t: (0, bi, t, 0, 0)),
                  gate_spec],
        out_specs=pl.BlockSpec((2, 1, cpt, GDN_PAIRS, CHUNK, 2 * CHUNK), lambda bi, t: (0, bi, t, 0, 0, 0)),
        out_shape=jax.ShapeDtypeStruct((2, b, s // CHUNK, GDN_PAIRS, CHUNK, 2 * CHUNK), F32),
        compiler_params=_params("parallel", "parallel"),
        name="gdn_a",
    )(qkv, gc, gctp, bt)


def _substitute_rows(at_ref, x_ref, hh):
    n = CHUNK
    zero = jnp.zeros((8, INV_LANES), F32)
    for i in range(n):
        nb = (i + 7) // 8
        acc = [-at_ref[hh, pl.ds(i * n + jb * 8, 8), :] for jb in range(nb)]
        for m in range(1, i):
            a_im = jnp.broadcast_to(at_ref[hh, pl.ds(i * n + m, 1), :], (8, INV_LANES))
            for jb in range((m + 7) // 8):
                acc[jb] = acc[jb] - a_im * x_ref[hh, pl.ds(m * n + jb * 8, 8), :]
        for jb in range(n // 8):
            x_ref[hh, pl.ds(i * n + jb * 8, 8), :] = acc[jb] if jb < nb else zero


def _gdn_inv_kernel(a_ref, t_ref, at_ref, x_ref):
    fwd = pl.program_id(0) == 0
    n = CHUNK

    def load(r, transposed):
        slab = a_ref[0, pl.ds(r, INV_LANES, stride=n), :].T
        for hh in range(2):
            dst = pl.ds(r, n, stride=n) if transposed else pl.ds(pl.multiple_of(r * n, n), n)
            at_ref[hh, dst, :] = slab[hh * n:(hh + 1) * n]

    def store(r, transposed):
        src = pl.ds(r, n, stride=n) if transposed else pl.ds(pl.multiple_of(r * n, n), n)
        eye = jnp.where(lax.broadcasted_iota(jnp.int32, (n, INV_LANES), 0) == r, 1.0, 0.0)
        slab = jnp.concatenate([x_ref[0, src, :] + eye, x_ref[1, src, :] + eye], axis=0)
        t_ref[0, pl.ds(r, INV_LANES, stride=n), :] = slab.T

    def rows(fn, transposed):
        def body(r, carry):
            fn(r, transposed)
            return carry
        lax.fori_loop(0, n, body, 0)

    @pl.when(fwd)
    def _():
        rows(load, False)

    @pl.when(jnp.logical_not(fwd))
    def _():
        rows(load, True)

    def halves(hh, carry):
        _substitute_rows(at_ref, x_ref, hh)
        return carry

    lax.fori_loop(0, 2, halves, 0)

    @pl.when(fwd)
    def _():
        rows(store, False)

    @pl.when(jnp.logical_not(fwd))
    def _():
        rows(store, True)


def _gdn_inv(a):
    shape = a.shape
    n_sys = shape[1] * shape[2] * shape[3]
    assert n_sys % INV_LANES == 0
    rows_per_step = INV_LANES * CHUNK
    t = pl.pallas_call(
        _gdn_inv_kernel,
        grid=(2, n_sys // INV_LANES),
        in_specs=[pl.BlockSpec((1, rows_per_step, 2 * CHUNK), lambda d, g: (d, g, 0))],
        out_specs=pl.BlockSpec((1, rows_per_step, 2 * CHUNK), lambda d, g: (d, g, 0)),
        out_shape=jax.ShapeDtypeStruct((2, n_sys * CHUNK, 2 * CHUNK), F32),
        scratch_shapes=[pltpu.VMEM((2, CHUNK * CHUNK, INV_LANES), F32), pltpu.VMEM((2, CHUNK * CHUNK, INV_LANES), F32)],
        compiler_params=_params("parallel", "parallel"),
        name="gdn_inv",
    )(a.reshape(2, n_sys * CHUNK, 2 * CHUNK))
    return t.reshape(shape)


def _gdn_uw_kernel(t_ref, q_ref, k_ref, v_ref, gc_ref, gctp_ref, bt_ref, u_ref, w_ref, qk_ref):
    for ch in range(TM // CHUNK):
        rows = pl.ds(ch * CHUNK, CHUNK)
        for hp in range(GDN_PAIRS):
            cols = slice(hp * GDN_PAIR, (hp + 1) * GDN_PAIR)
            kp = k_ref[0, rows, cols]
            vp = v_ref[0, rows, cols]
            qk = _dot_nt(q_ref[0, rows, cols], _pair_blockdiag_rows(kp, GDN_DK))
            for d in range(2):
                gc = gc_ref[d, 0, rows, :]
                beta = _pair_cols(bt_ref[d, 0, rows, :], hp, GDN_PAIR)
                rhs = jnp.concatenate([vp * beta, kp * (beta * jnp.exp(_pair_cols(gc, hp, GDN_PAIR)))], axis=1)
                uw = _dot(t_ref[d, 0, ch, hp], _pair_blockdiag_rows(rhs, GDN_DK))
                u_ref[d, 0, rows, cols] = uw[:, :GDN_PAIR]
                w_ref[d, 0, rows, cols] = uw[:, GDN_PAIR:].astype(BF16)
                decay = _pair_decay(gc, gctp_ref[d, 0, ch, hp:hp + 1, :], hp, d == 0, True)
                qk_ref[d, 0, ch, hp] = (qk * decay).astype(BF16)


def _gdn_uw(t, qkv, gc, gctp, bt):
    b, s, _ = qkv.shape
    nh = GDN_HEADS
    width = nh * GDN_DK
    cpt = TM // CHUNK
    gate_spec = pl.BlockSpec((2, 1, TM, nh), lambda bi, t_: (0, bi, t_, 0))
    sys_spec = pl.BlockSpec((2, 1, cpt, GDN_PAIRS, CHUNK, 2 * CHUNK), lambda bi, t_: (0, bi, t_, 0, 0, 0))
    tok_spec = pl.BlockSpec((2, 1, TM, width), lambda bi, t_: (0, bi, t_, 0))
    return pl.pallas_call(
        _gdn_uw_kernel,
        grid=(b, s // TM),
        in_specs=[sys_spec,
                  pl.BlockSpec((1, TM, width), lambda bi, t_: (bi, t_, 0)),
                  pl.BlockSpec((1, TM, width), lambda bi, t_: (bi, t_, 1)),
                  pl.BlockSpec((1, TM, width), lambda bi, t_: (bi, t_, 2)),
                  gate_spec,
                  pl.BlockSpec((2, 1, cpt, GDN_PAIRS, 2 * CHUNK), lambda bi, t_: (0, bi, t_, 0, 0)),
                  gate_spec],
        out_specs=[tok_spec, tok_spec, sys_spec],
        out_shape=[jax.ShapeDtypeStruct((2, b, s, width), F32), jax.ShapeDtypeStruct((2, b, s, width), BF16),
                   jax.ShapeDtypeStruct((2, b, s // CHUNK, GDN_PAIRS, CHUNK, 2 * CHUNK), BF16)],
        compiler_params=_params("parallel", "parallel"),
        name="gdn_uw",
    )(t, qkv, qkv, qkv, gc, gctp, bt)


def _gdn_scan_kernel(qf_ref, kf_ref, uf_ref, wf_ref, qkf_ref, gcf_ref, qb_ref, kb_ref, ub_ref, wb_ref, qkb_ref, gcb_ref,
                     of_ref, ob_ref, s_ref):
    @pl.when(pl.program_id(1) == 0)
    def _():
        s_ref[...] = jnp.zeros_like(s_ref)

    dk = GDN_DK
    zeros = jnp.zeros((dk, dk), BF16)
    directions = ((qf_ref, kf_ref, uf_ref, wf_ref, qkf_ref, gcf_ref, of_ref),
                  (qb_ref, kb_ref, ub_ref, wb_ref, qkb_ref, gcb_ref, ob_ref))
    for d, (q_ref, k_ref, u_ref, w_ref, qk_ref, gc_ref, o_ref) in enumerate(directions):
        gc_all = gc_ref[0, 0]
        glast = gc_all[CHUNK - 1:CHUNK] if d == 0 else gc_all[0:1]
        chunk_decay = jnp.exp(glast)
        for hp in range(GDN_PAIRS):
            cols = slice(hp * GDN_PAIR, (hp + 1) * GDN_PAIR)
            gcp = _pair_cols(gc_all, hp, GDN_PAIR)
            q_in = q_ref[0, :, cols] * jnp.exp(gcp)
            k_out = k_ref[0, :, cols] * jnp.exp(_pair_cols(glast, hp, GDN_PAIR) - gcp)
            sa = s_ref[d, 2 * hp]
            sb = s_ref[d, 2 * hp + 1]
            s_bd = jnp.concatenate([jnp.concatenate([sa.astype(BF16), zeros], axis=1),
                                    jnp.concatenate([zeros, sb.astype(BF16)], axis=1)], axis=0)
            both = jnp.dot(jnp.concatenate([w_ref[0, 0, :, cols], q_in.astype(BF16)], axis=0), s_bd,
                           preferred_element_type=F32)
            v_new = u_ref[0, 0, :, cols] - both[:CHUNK]
            intra = jnp.dot(qk_ref[0, 0, 0, hp], _pair_blockdiag_rows(v_new, dk).astype(BF16), preferred_element_type=F32)
            o_ref[0, :, cols] = both[CHUNK:] + intra
            upd = _dot_tn(k_out, v_new)
            s_ref[d, 2 * hp] = sa * chunk_decay[:, 2 * hp:2 * hp + 1] + upd[:dk, :dk]
            s_ref[d, 2 * hp + 1] = sb * chunk_decay[:, 2 * hp + 1:2 * hp + 2] + upd[dk:, dk:]


def _scan_tile(d, step, n_tiles, ctx_tiles):
    back = jnp.where(step < ctx_tiles, ctx_tiles - 1 - step, n_tiles + ctx_tiles - 1 - step)
    return jnp.where(d == 0, step, back)


def _gdn_scan(qkv, u, w, qk, gc):
    b, s, _ = qkv.shape
    nh, dk = GDN_HEADS, GDN_DK
    nc = s // CHUNK
    width = nh * dk
    tile = functools.partial(_scan_tile, n_tiles=nc, ctx_tiles=TM // CHUNK)

    def specs(d):
        return [pl.BlockSpec((1, CHUNK, width), lambda bi, i: (bi, tile(d, i), 0)),
                pl.BlockSpec((1, CHUNK, width), lambda bi, i: (bi, tile(d, i), 1)),
                pl.BlockSpec((1, 1, CHUNK, width), lambda bi, i: (d, bi, tile(d, i), 0)),
                pl.BlockSpec((1, 1, CHUNK, width), lambda bi, i: (d, bi, tile(d, i), 0)),
                pl.BlockSpec((1, 1, 1, GDN_PAIRS, CHUNK, 2 * CHUNK), lambda bi, i: (d, bi, tile(d, i), 0, 0, 0)),
                pl.BlockSpec((1, 1, CHUNK, nh), lambda bi, i: (d, bi, tile(d, i), 0))]

    out = jax.ShapeDtypeStruct((b, s, width), F32)
    return pl.pallas_call(
        _gdn_scan_kernel,
        grid=(b, nc),
        in_specs=specs(0) + specs(1),
        out_specs=[pl.BlockSpec((1, CHUNK, width), lambda bi, i: (bi, tile(0, i), 0)),
                   pl.BlockSpec((1, CHUNK, width), lambda bi, i: (bi, tile(1, i), 0))],
        out_shape=[out, out],
        scratch_shapes=[pltpu.VMEM((2, nh, dk, dk), F32)],
        compiler_params=_params("parallel", "arbitrary"),
        name="gdn_scan",
    )(qkv, qkv, u, w, qk, gc, qkv, qkv, u, w, qk, gc)


def _gdn_out_kernel(of_ref, ob_ref, z_ref, x_ref, mod_ref, ng_ref, w_ref, out_ref):
    m = mod_ref[0, 0]
    o = of_ref[0] + ob_ref[0]
    ng = ng_ref[...]
    parts = []
    for h in range(GDN_HEADS):
        seg = o[:, h * GDN_DK:(h + 1) * GDN_DK]
        parts.append(seg * lax.rsqrt(jnp.mean(seg * seg, axis=-1, keepdims=True) + EPS) * ng)
    y = _dot(jnp.concatenate(parts, axis=-1) * _silu(z_ref[0]), w_ref[...])
    out_ref[0] = x_ref[0] + m[2:3] * y


def _gdn_out(o_f, o_b, p, x, modv, norm_g, w_out_bf16):
    b, s, d = x.shape
    width = GDN_HEADS * GDN_DK
    return pl.pallas_call(
        _gdn_out_kernel,
        grid=(b, s // TM),
        in_specs=[pl.BlockSpec((1, TM, width), lambda bi, t: (bi, t, 0)),
                  pl.BlockSpec((1, TM, width), lambda bi, t: (bi, t, 0)),
                  pl.BlockSpec((1, TM, width), lambda bi, t: (bi, t, 3)),
                  pl.BlockSpec((1, TM, d), lambda bi, t: (bi, t, 0)),
                  pl.BlockSpec((1, 1, 6, d), _mod_spec()),
                  pl.BlockSpec((1, GDN_DK), lambda bi, t: (0, 0)),
                  pl.BlockSpec((width, d), lambda bi, t: (0, 0))],
        out_specs=pl.BlockSpec((1, TM, d), lambda bi, t: (bi, t, 0)),
        out_shape=jax.ShapeDtypeStruct((b, s, d), F32),
        compiler_params=_params("parallel", "parallel"),
        name="gdn_out",
    )(o_f, o_b, p, x, modv, norm_g.reshape(1, GDN_DK), w_out_bf16)


def _mixer_gdn(x, modv, norm_g, w_in, conv_w, a_log, dt_bias, out_norm_g, w_out):
    nh, dk = GDN_HEADS, GDN_DK
    n_main = 4 * nh * dk
    p = _in_proj(x, modv, norm_g, w_in[:, :n_main].astype(BF16), 2048)
    gc, bt = _gdn_gates(x, modv, norm_g, w_in[:, n_main:], a_log, dt_bias)
    b, s, _ = x.shape
    gctp = gc.reshape(2, b, s // CHUNK, CHUNK, GDN_PAIRS, 2).transpose(0, 1, 2, 4, 5, 3)
    gctp = gctp.reshape(2, b, s // CHUNK, GDN_PAIRS, 2 * CHUNK)
    qkv = _gdn_conv(p, conv_w)
    t = _gdn_inv(_gdn_a(qkv, gc, gctp, bt))
    u, w, qk = _gdn_uw(t, qkv, gc, gctp, bt)
    o_f, o_b = _gdn_scan(qkv, u, w, qk, gc)
    return _gdn_out(o_f, o_b, p, x, modv, out_norm_g, w_out.astype(BF16))


def _pool_kernel(x_ref, xp_ref, xn_ref, mod_ref, g_ref, w_ref, b_ref, sc_ref, o_ref, ext_ref):
    t = pl.program_id(1)
    nt = pl.num_programs(1)
    prev_ok, next_ok = _halo_valid()
    m = mod_ref[0, 0]
    g = g_ref[...]
    x = x_ref[0]
    h = _rms_mod(x, g, m[0:1], m[1:2])
    ext_ref[0:HALO] = jnp.where(prev_ok, _rms_mod(xp_ref[0], g, m[0:1], m[1:2]), 0.0)
    ext_ref[HALO:HALO + TM] = h
    ext_ref[HALO + TM:] = jnp.where(next_ok, _rms_mod(xn_ref[0], g, m[0:1], m[1:2]), 0.0)
    row = lax.broadcasted_iota(jnp.int32, (TM, 1), 0)
    pos = row + jnp.where(t == 0, 0, (t - 1) * TM)
    n_seq = jnp.where(t == 0, TM, (nt - 1) * TM)
    pg = POOL_GROUP
    for gi, win in enumerate(POOL_WINDOWS):
        lo_off = HALO - win // 2
        acc = ext_ref[lo_off:lo_off + TM, gi * pg:(gi + 1) * pg]
        for k in range(1, win):
            acc = acc + ext_ref[lo_off + k:lo_off + k + TM, gi * pg:(gi + 1) * pg]
        lo = jnp.clip(pos - win // 2, 0, n_seq)
        hi = jnp.clip(pos + win - win // 2, 0, n_seq)
        pooled = acc / (hi - lo).astype(F32) - h[:, gi * pg:(gi + 1) * pg]
        y = (_dot(pooled, w_ref[gi]) + b_ref[gi]) * sc_ref[:, gi * pg:(gi + 1) * pg]
        o_ref[0, :, gi * pg:(gi + 1) * pg] = x[:, gi * pg:(gi + 1) * pg] + m[2:3, gi * pg:(gi + 1) * pg] * y


def _mixer_pool(x, modv, norm_g, w_group, b_group, scale):
    b, s, d = x.shape
    ng, pg = len(POOL_WINDOWS), POOL_GROUP
    prev_spec, next_spec = _halo_specs(d, lambda: 0, s // TM)
    return pl.pallas_call(
        _pool_kernel,
        grid=(b, s // TM),
        in_specs=[pl.BlockSpec((1, TM, d), lambda bi, t: (bi, t, 0)),
                  prev_spec, next_spec,
                  pl.BlockSpec((1, 1, 6, d), _mod_spec()),
                  pl.BlockSpec((1, d), lambda bi, t: (0, 0)),
                  pl.BlockSpec((ng, pg, pg), lambda bi, t: (0, 0, 0)),
                  pl.BlockSpec((ng, 1, pg), lambda bi, t: (0, 0, 0)),
                  pl.BlockSpec((1, d), lambda bi, t: (0, 0))],
        out_specs=pl.BlockSpec((1, TM, d), lambda bi, t: (bi, t, 0)),
        out_shape=jax.ShapeDtypeStruct((b, s, d), F32),
        scratch_shapes=[pltpu.VMEM((TM + 2 * HALO, d), F32)],
        compiler_params=_params("parallel", "parallel"),
        name="pool_mixer",
    )(x, x, x, modv, norm_g.reshape(1, d), w_group.astype(BF16), b_group.reshape(ng, 1, pg), scale.reshape(1, d))


def _rotate(t, cos, sin_signed):
    half = RET_DK // 2
    swapped = jnp.concatenate([pltpu.roll(t[:, :half], half // 2, 1), pltpu.roll(t[:, half:], half // 2, 1)], axis=-1)
    return t * cos + swapped * sin_signed


def _ret_scan_kernel(lg_ref, q_ref, k_ref, v_ref, cos_ref, sin_ref, o_ref, s_ref):
    d = pl.program_id(1)
    step = pl.program_id(2)

    @pl.when(step == 0)
    def _():
        s_ref[...] = jnp.zeros_like(s_ref)

    fwd = d == 0
    r = lax.broadcasted_iota(jnp.int32, (TM, TM), 0)
    c = lax.broadcasted_iota(jnp.int32, (TM, TM), 1)
    rel = jnp.where(fwd, r - c, c - r).astype(F32)
    row = lax.broadcasted_iota(jnp.int32, (TM, 1), 0)
    q_pow = jnp.where(fwd, row + 1, TM - row).astype(F32)
    k_pow = jnp.where(fwd, TM - 1 - row, row).astype(F32)
    cos = cos_ref[...]
    sin = sin_ref[...]
    dk, dv = RET_DK, RET_DV
    for h in range(RET_HEADS):
        lg = jnp.full((1, 1), lg_ref[d, h], F32)
        q = _rotate(q_ref[0, :, h * dk:(h + 1) * dk], cos, sin)
        k = _rotate(k_ref[0, :, h * dk:(h + 1) * dk] * (dk ** -0.5), cos, sin)
        v = v_ref[0, :, h * dv:(h + 1) * dv]
        dmat = jnp.where(rel >= 0, jnp.exp(jnp.maximum(rel, 0.0) * lg), 0.0)
        inner = _dot_nt(q, k) * dmat
        s = s_ref[h]
        o_ref[0, 0, :, h * dv:(h + 1) * dv] = _dot(inner, v) + _dot(q * jnp.exp(q_pow * lg), s)
        s_ref[h] = s * jnp.exp(TM * lg) + _dot_tn(k * jnp.exp(k_pow * lg), v)


def _ret_scan(p, log_gamma, cos, sin):
    b, s, _ = p.shape
    nt = s // TM
    qw, vw = RET_HEADS * RET_DK, RET_HEADS * RET_DV
    tile = functools.partial(_scan_tile, n_tiles=nt, ctx_tiles=1)
    return pl.pallas_call(
        _ret_scan_kernel,
        grid=(b, 2, nt),
        in_specs=[pl.BlockSpec(memory_space=pltpu.SMEM),
                  pl.BlockSpec((1, TM, qw), lambda bi, d, i: (bi, tile(d, i), 0)),
                  pl.BlockSpec((1, TM, qw), lambda bi, d, i: (bi, tile(d, i), 1)),
                  pl.BlockSpec((1, TM, vw), lambda bi, d, i: (bi, tile(d, i), 1)),
                  pl.BlockSpec((TM, RET_DK), lambda bi, d, i: (tile(d, i), 0)),
                  pl.BlockSpec((TM, RET_DK), lambda bi, d, i: (tile(d, i), 0))],
        out_specs=pl.BlockSpec((1, 1, TM, vw), lambda bi, d, i: (d, bi, tile(d, i), 0)),
        out_shape=jax.ShapeDtypeStruct((2, b, s, vw), F32),
        scratch_shapes=[pltpu.VMEM((RET_HEADS, RET_DK, RET_DV), F32)],
        compiler_params=_params("parallel", "parallel", "arbitrary"),
        name="ret_scan",
    )(log_gamma, p, p, p, cos, sin)


def _ret_out_kernel(o_ref, gate_ref, x_ref, mod_ref, ng_ref, w_ref, out_ref):
    m = mod_ref[0, 0]
    o = o_ref[0, 0] + o_ref[1, 0]
    ng = ng_ref[...]
    parts = []
    for h in range(RET_HEADS):
        seg = o[:, h * RET_DV:(h + 1) * RET_DV]
        mu = jnp.mean(seg, axis=-1, keepdims=True)
        cen = seg - mu
        var = jnp.mean(cen * cen, axis=-1, keepdims=True)
        parts.append(cen * lax.rsqrt(var + EPS) * ng)
    y = _dot(_silu(gate_ref[0]) * jnp.concatenate(parts, axis=-1), w_ref[...])
    out_ref[0] = x_ref[0] + m[2:3] * y


def _ret_out(o, p, x, modv, norm_g, w_out_bf16):
    b, s, d = x.shape
    vw = RET_HEADS * RET_DV
    return pl.pallas_call(
        _ret_out_kernel,
        grid=(b, s // TM),
        in_specs=[pl.BlockSpec((2, 1, TM, vw), lambda bi, t: (0, bi, t, 0)),
                  pl.BlockSpec((1, TM, vw), lambda bi, t: (bi, t, 2)),
                  pl.BlockSpec((1, TM, d), lambda bi, t: (bi, t, 0)),
                  pl.BlockSpec((1, 1, 6, d), _mod_spec()),
                  pl.BlockSpec((1, RET_DV), lambda bi, t: (0, 0)),
                  pl.BlockSpec((vw, d), lambda bi, t: (0, 0))],
        out_specs=pl.BlockSpec((1, TM, d), lambda bi, t: (bi, t, 0)),
        out_shape=jax.ShapeDtypeStruct((b, s, d), F32),
        compiler_params=_params("parallel", "parallel"),
        name="ret_out",
    )(o, p, x, modv, norm_g.reshape(1, RET_DV), w_out_bf16)


def _rotary_tables(s):
    n_lat = s - TM
    pos = jnp.arange(n_lat, dtype=jnp.int32)
    rows = (pos // GRID_W).astype(F32)
    cols = (pos % GRID_W).astype(F32)
    quarter = RET_DK // 4
    inv_freq = ROPE_BASE ** (-jnp.arange(quarter, dtype=F32) / quarter)
    ang_r = rows[:, None] * inv_freq[None, :]
    ang_c = cols[:, None] * inv_freq[None, :]
    cos = jnp.concatenate([jnp.cos(ang_r), jnp.cos(ang_r), jnp.cos(ang_c), jnp.cos(ang_c)], axis=-1)
    sin = jnp.concatenate([-jnp.sin(ang_r), jnp.sin(ang_r), -jnp.sin(ang_c), jnp.sin(ang_c)], axis=-1)
    cos = jnp.concatenate([jnp.ones((TM, RET_DK), F32), cos], axis=0)
    sin = jnp.concatenate([jnp.zeros((TM, RET_DK), F32), sin], axis=0)
    return cos, sin


def _mixer_retention(x, modv, norm_g, w_in, decay_logit, out_norm_g, w_out):
    p = _in_proj(x, modv, norm_g, w_in.astype(BF16), 2048)
    cos, sin = _rotary_tables(x.shape[1])
    o = _ret_scan(p, jax.nn.log_sigmoid(decay_logit.astype(F32)), cos, sin)
    return _ret_out(o, p, x, modv, out_norm_g, w_out.astype(BF16))


ROUTE_LANES = 128


def _route_kernel(x_ref, mod_ref, g_ref, w_ref, b_ref, rt_ref, cnt_out_ref, cnt_ref):
    first = (pl.program_id(0) == 0) & (pl.program_id(1) == 0)

    @pl.when(first)
    def _():
        cnt_ref[...] = jnp.zeros_like(cnt_ref)

    m = mod_ref[0, 0]
    h = _rms_mod(x_ref[0], g_ref[...], m[3:4], m[4:5])
    logits = _dot_hi(h, w_ref[...]) + b_ref[...]
    lane = lax.broadcasted_iota(jnp.int32, (TM, ROUTE_LANES), 1)
    big = jnp.int32(ROUTE_LANES)
    neg = -jnp.inf
    glog = jnp.where((lane >= N_EXPERTS) & (lane < N_EXPERTS + N_GROUPS), logits, neg)
    gmax = jnp.max(glog, axis=-1, keepdims=True)
    gsel = jnp.min(jnp.where(glog == gmax, lane, big), axis=-1, keepdims=True) - N_EXPERTS
    p_group = 1.0 / jnp.sum(jnp.exp(glog - gmax), axis=-1, keepdims=True)
    elog = jnp.where((lane >= gsel * EXPERTS_PER_GROUP) & (lane < (gsel + 1) * EXPERTS_PER_GROUP), logits, neg)
    m1 = jnp.max(elog, axis=-1, keepdims=True)
    i1 = jnp.min(jnp.where(elog == m1, lane, big), axis=-1, keepdims=True)
    elog2 = jnp.where(lane == i1, neg, elog)
    m2 = jnp.max(elog2, axis=-1, keepdims=True)
    i2 = jnp.min(jnp.where(elog2 == m2, lane, big), axis=-1, keepdims=True)
    e2 = jnp.exp(m2 - m1)
    w1 = p_group / (1.0 + e2)
    w2 = p_group * e2 / (1.0 + e2)
    onehot = jnp.where((lane == i1) | (lane == i2), 1.0, 0.0)
    r = lax.broadcasted_iota(jnp.int32, (TM, TM), 0)
    c = lax.broadcasted_iota(jnp.int32, (TM, TM), 1)
    before = _dot(jnp.where(c < r, 1.0, 0.0), onehot) + cnt_ref[...]
    r1 = jnp.sum(jnp.where(lane == i1, before, 0.0), axis=-1, keepdims=True)
    r2 = jnp.sum(jnp.where(lane == i2, before, 0.0), axis=-1, keepdims=True)
    cnt_ref[...] = cnt_ref[...] + jnp.sum(onehot, axis=0, keepdims=True)
    cnt_out_ref[...] = cnt_ref[...]
    vals = (i1.astype(F32), i2.astype(F32), w1, w2, r1, r2)
    out = jnp.zeros((TM, ROUTE_LANES), F32)
    for pos_, val in enumerate(vals):
        out = jnp.where(lane == pos_, val, out)
    rt_ref[0] = out


def _route(x, modv, norm_g, w_route, b_route):
    b, s, d = x.shape
    return pl.pallas_call(
        _route_kernel,
        grid=(b, s // TM),
        in_specs=[pl.BlockSpec((1, TM, d), lambda bi, t: (bi, t, 0)),
                  pl.BlockSpec((1, 1, 6, d), _mod_spec()),
                  pl.BlockSpec((1, d), lambda bi, t: (0, 0)),
                  pl.BlockSpec((d, ROUTE_LANES), lambda bi, t: (0, 0)),
                  pl.BlockSpec((1, ROUTE_LANES), lambda bi, t: (0, 0))],
        out_specs=[pl.BlockSpec((1, TM, ROUTE_LANES), lambda bi, t: (bi, t, 0)),
                   pl.BlockSpec((1, ROUTE_LANES), lambda bi, t: (0, 0))],
        out_shape=[jax.ShapeDtypeStruct((b, s, ROUTE_LANES), F32), jax.ShapeDtypeStruct((1, ROUTE_LANES), F32)],
        scratch_shapes=[pltpu.VMEM((1, ROUTE_LANES), F32)],
        compiler_params=_params("arbitrary", "arbitrary"),
        name="moe_route",
    )(x, modv, norm_g.reshape(1, d), w_route, b_route)


def _row_copy_window(n_rows, make_copy):
    def body(i, carry):
        @pl.when(i >= DMA_WINDOW)
        def _():
            for k in range(2):
                make_copy(i - DMA_WINDOW, k).wait()

        @pl.when(i < n_rows)
        def _():
            for k in range(2):
                make_copy(i, k).start()
        return carry

    lax.fori_loop(0, n_rows + DMA_WINDOW, body, 0)


def _dispatch_kernel(dest_ref, x_ref, mod_ref, g_ref, zero_ref, xb_ref, h_ref, sem):
    del zero_ref
    m = mod_ref[0, 0]
    h_ref[...] = _rms_mod(x_ref[0], g_ref[...], m[3:4], m[4:5])

    def make_copy(i, k):
        return pltpu.make_async_copy(h_ref.at[pl.ds(i, 1)], xb_ref.at[pl.ds(dest_ref[0, 0, 2 * i + k], 1)],
                                     sem.at[jnp.bitwise_and(i, DMA_WINDOW - 1), k])

    _row_copy_window(TM, make_copy)


def _dispatch(x, modv, norm_g, dest, n_rows):
    b, s, d = x.shape
    nt = s // TM
    return pl.pallas_call(
        _dispatch_kernel,
        grid=(b, nt),
        in_specs=[pl.BlockSpec((1, 1, 2 * TM), lambda bi, t: (bi * nt + t, 0, 0), memory_space=pltpu.SMEM),
                  pl.BlockSpec((1, TM, d), lambda bi, t: (bi, t, 0)),
                  pl.BlockSpec((1, 1, 6, d), _mod_spec()),
                  pl.BlockSpec((1, d), lambda bi, t: (0, 0)),
                  pl.BlockSpec(memory_space=pl.ANY)],
        out_specs=pl.BlockSpec(memory_space=pl.ANY),
        out_shape=jax.ShapeDtypeStruct((n_rows, d), F32),
        input_output_aliases={4: 0},
        scratch_shapes=[pltpu.VMEM((TM, d), F32), pltpu.SemaphoreType.DMA((DMA_WINDOW, 2))],
        compiler_params=_params("arbitrary", "arbitrary"),
        name="moe_dispatch",
    )(dest.reshape(b * nt, 1, 2 * TM), x, modv, norm_g.reshape(1, d), jnp.zeros((n_rows, d), F32))


def _expert_kernel(be_ref, nu_ref, x_ref, wg_ref, wu_ref, wd_ref, o_ref, wg_bf, wu_bf, wd_bf):
    i = pl.program_id(0)

    @pl.when(i < nu_ref[0])
    def _():
        @pl.when((i == 0) | (be_ref[i] != be_ref[jnp.maximum(i - 1, 0)]))
        def _():
            wg_bf[...] = wg_ref[0, 0].astype(BF16)
            wu_bf[...] = wu_ref[0, 0].astype(BF16)
            wd_bf[...] = wd_ref[0, 0].astype(BF16)

        xb = x_ref[...].astype(BF16)
        act = _silu(jnp.dot(xb, wg_bf[...], preferred_element_type=F32)) * jnp.dot(xb, wu_bf[...], preferred_element_type=F32)
        o_ref[...] = jnp.dot(act.astype(BF16), wd_bf[...], preferred_element_type=F32)

    @pl.when(i >= nu_ref[0])
    def _():
        o_ref[...] = jnp.zeros_like(o_ref)


def _experts(xb, block_e, n_used, layer, w_gate, w_up, w_down):
    n_rows, d = xb.shape
    de = w_gate.shape[-1]
    nb = n_rows // MOE_BLOCK

    def last_used(i, nu):
        return jnp.minimum(i, nu[0] - 1)

    grid_spec = pltpu.PrefetchScalarGridSpec(
        num_scalar_prefetch=2,
        grid=(nb,),
        in_specs=[pl.BlockSpec((MOE_BLOCK, d), lambda i, be, nu: (last_used(i, nu), 0)),
                  pl.BlockSpec((1, 1, d, de), lambda i, be, nu: (layer, be[last_used(i, nu)], 0, 0)),
                  pl.BlockSpec((1, 1, d, de), lambda i, be, nu: (layer, be[last_used(i, nu)], 0, 0)),
                  pl.BlockSpec((1, 1, de, d), lambda i, be, nu: (layer, be[last_used(i, nu)], 0, 0))],
        out_specs=pl.BlockSpec((MOE_BLOCK, d), lambda i, be, nu: (i, 0)),
        scratch_shapes=[pltpu.VMEM((d, de), BF16), pltpu.VMEM((d, de), BF16), pltpu.VMEM((de, d), BF16)],
    )
    return pl.pallas_call(
        _expert_kernel,
        grid_spec=grid_spec,
        out_shape=jax.ShapeDtypeStruct((n_rows, d), F32),
        compiler_params=_params("arbitrary"),
        name="moe_experts",
    )(block_e, n_used, xb, w_gate, w_up, w_down)


def _combine_kernel(dest_ref, x_ref, mod_ref, rt_ref, fg_ref, yb_ref, o_ref, y_ref, sem, *, final_norm):
    def make_copy(i, k):
        return pltpu.make_async_copy(yb_ref.at[pl.ds(dest_ref[0, 0, 2 * i + k], 1)], y_ref.at[k, pl.ds(i, 1)],
                                     sem.at[jnp.bitwise_and(i, DMA_WINDOW - 1), k])

    _row_copy_window(TM, make_copy)
    m = mod_ref[0, 0]
    rt = rt_ref[0]
    out = x_ref[0] + m[5:6] * (rt[:, 2:3] * y_ref[0] + rt[:, 3:4] * y_ref[1])
    if final_norm:
        out = out * lax.rsqrt(jnp.mean(out * out, axis=-1, keepdims=True) + EPS) * fg_ref[...]
    o_ref[0] = out


def _combine(x, modv, rt, dest, yb, final_g, final_norm):
    b, s, d = x.shape
    nt = s // TM
    skip = 1 if final_norm else 0
    return pl.pallas_call(
        functools.partial(_combine_kernel, final_norm=final_norm),
        grid=(b, nt - skip),
        in_specs=[pl.BlockSpec((1, 1, 2 * TM), lambda bi, t: (bi * nt + t + skip, 0, 0), memory_space=pltpu.SMEM),
                  pl.BlockSpec((1, TM, d), lambda bi, t: (bi, t + skip, 0)),
                  pl.BlockSpec((1, 1, 6, d), lambda bi, t: (bi, jnp.minimum(t + skip, 1), 0, 0)),
                  pl.BlockSpec((1, TM, ROUTE_LANES), lambda bi, t: (bi, t + skip, 0)),
                  pl.BlockSpec((1, d), lambda bi, t: (0, 0)),
                  pl.BlockSpec(memory_space=pl.ANY)],
        out_specs=pl.BlockSpec((1, TM, d), lambda bi, t: (bi, t, 0)),
        out_shape=jax.ShapeDtypeStruct((b, s - skip * TM, d), F32),
        scratch_shapes=[pltpu.VMEM((2, TM, d), F32), pltpu.SemaphoreType.DMA((DMA_WINDOW, 2))],
        compiler_params=_params("arbitrary", "arbitrary"),
        name="moe_combine",
    )(dest.reshape(b * nt, 1, 2 * TM), x, modv, rt, final_g.reshape(1, d), yb)


def _hier_moe(x, modv, norm_g, layer, wg_r, bg_r, we_r, be_r, w_gate, w_up, w_down, final_g, final_norm):
    b, s, d = x.shape
    n_tok = b * s
    pad = ROUTE_LANES - N_EXPERTS - N_GROUPS
    w_route = jnp.concatenate([we_r, wg_r, jnp.zeros((d, pad), F32)], axis=1)
    b_route = jnp.concatenate([be_r, bg_r, jnp.zeros((pad,), F32)]).reshape(1, ROUTE_LANES)
    rt, cnt = _route(x, modv, norm_g, w_route, b_route)
    counts = cnt[0, :N_EXPERTS].astype(jnp.int32)
    padded = (counts + MOE_BLOCK - 1) // MOE_BLOCK * MOE_BLOCK
    pad_end = jnp.cumsum(padded)
    pad_start = pad_end - padded
    n_blocks = -(-(2 * n_tok + N_EXPERTS * (MOE_BLOCK - 1)) // MOE_BLOCK)
    rt2 = rt.reshape(n_tok, ROUTE_LANES)
    expert = rt2[:, 0:2].astype(jnp.int32)
    dest = (pad_start[expert] + rt2[:, 4:6].astype(jnp.int32)).reshape(-1)
    block_e = jnp.minimum(jnp.searchsorted(pad_end, jnp.arange(n_blocks, dtype=jnp.int32) * MOE_BLOCK, side='right'),
                          N_EXPERTS - 1).astype(jnp.int32)
    n_used = (pad_end[-1:] // MOE_BLOCK).astype(jnp.int32)
    xb = _dispatch(x, modv, norm_g, dest, n_blocks * MOE_BLOCK)
    yb = _experts(xb, block_e, n_used, layer, w_gate, w_up, w_down)
    return _combine(x, modv, rt, dest, yb, final_g, final_norm)


def kernel(x, c, ctx, c_ctx, mod_w, mod_b, norm1_g, norm2_g, final_g, gdn_w_in, gdn_conv_w, gdn_a_log, gdn_dt_bias, gdn_norm_g, gdn_w_out, pool_w, pool_b, pool_scale, ret_w_in, ret_decay_logit, ret_norm_g, ret_w_out, router_group_w, router_group_b, router_expert_w, router_expert_b, exp_w_gate, exp_w_up, exp_w_down):
    b, n_lat, d = x.shape
    depth = mod_w.shape[0]
    assert ctx.shape[1] == TM and n_lat % TM == 0 and b < 16
    xs = jnp.concatenate([ctx, x], axis=1)
    cvec = jnp.concatenate([c, c_ctx[None], jnp.zeros((15 - b, d), F32)], axis=0)
    mods = _modulation(cvec, mod_w, mod_b)
    for i in range(depth):
        j, kind = i // 3, i % 3
        lat = mods[i, :b].reshape(b, 1, 6, d)
        con = jnp.broadcast_to(mods[i, b].reshape(1, 1, 6, d), (b, 1, 6, d))
        modv = jnp.concatenate([con, lat], axis=1)
        if kind == 0:
            xs = _mixer_gdn(xs, modv, norm1_g[i], gdn_w_in[j], gdn_conv_w[j], gdn_a_log[j], gdn_dt_bias[j],
                            gdn_norm_g[j], gdn_w_out[j])
        elif kind == 1:
            xs = _mixer_pool(xs, modv, norm1_g[i], pool_w[j], pool_b[j], pool_scale[j])
        else:
            xs = _mixer_retention(xs, modv, norm1_g[i], ret_w_in[j], ret_decay_logit[j], ret_norm_g[j], ret_w_out[j])
        xs = _hier_moe(xs, modv, norm2_g[i], i, router_group_w[i], router_group_b[i], router_expert_w[i],
                       router_expert_b[i], exp_w_gate, exp_w_up, exp_w_down, final_g, i == depth - 1)
    return xs
```

```python
import functools
import math

import jax
import jax.numpy as jnp
from jax import lax
from jax.experimental import pallas as pl
from jax.experimental.pallas import tpu as pltpu

F32 = jnp.float32
BF16 = jnp.bfloat16
HIGHEST = lax.Precision.HIGHEST

EPS = 1e-6
TM = 256
HALO = 8
CHUNK = 64
GDN_HEADS = 8
GDN_DK = 128
GDN_CONV = 5
RET_HEADS = 4
RET_DK = 256
RET_DV = 512
ROPE_BASE = 10000.0
GRID_W = 64
POOL_WINDOWS = (2, 4, 8, 16)
POOL_GROUP = 256
N_GROUPS = 4
EXPERTS_PER_GROUP = 8
N_EXPERTS = 32
MOE_BLOCK = 256
V7X_VMEM_LIMIT_BYTES = 56 * 1024 * 1024


def _params(*sem):
    return pltpu.CompilerParams(dimension_semantics=sem, vmem_limit_bytes=V7X_VMEM_LIMIT_BYTES)


def _dot(a, b):
    return jnp.dot(a.astype(BF16), b.astype(BF16), preferred_element_type=F32)


def _dot_hi(a, b):
    return jnp.dot(a, b, precision=HIGHEST, preferred_element_type=F32)


def _dot_nt(a, b):
    return lax.dot_general(a.astype(BF16), b.astype(BF16), (((1,), (1,)), ((), ())), preferred_element_type=F32)


def _dot_nt_hi(a, b):
    return lax.dot_general(a, b, (((1,), (1,)), ((), ())), precision=HIGHEST, preferred_element_type=F32)


def _dot_tn(a, b):
    return lax.dot_general(a.astype(BF16), b.astype(BF16), (((0,), (0,)), ((), ())), preferred_element_type=F32)


def _block_of(i, size):
    return jnp.right_shift(i, int(math.log2(size)))


def _silu(x):
    return x * jax.nn.sigmoid(x)


def _softplus(x):
    return jnp.maximum(x, 0.0) + jnp.log(1.0 + jnp.exp(-jnp.abs(x)))


def _rms_mod(x, g, shift, scale):
    y = x * lax.rsqrt(jnp.mean(x * x, axis=-1, keepdims=True) + EPS) * g
    return y * (1.0 + scale) + shift


def _mod_spec(grid_rank_prefix=0):
    def idx(*g):
        b, t = g[grid_rank_prefix], g[grid_rank_prefix + 1]
        return (b, jnp.minimum(t, 1), 0, 0)
    return idx


def _mod_kernel(c_ref, w_ref, b_ref, o_ref):
    o_ref[0] = _dot_hi(_silu(c_ref[...]), w_ref[0]) + b_ref[0]


def _modulation(cvec, mod_w, mod_b):
    n_layers, d, d6 = mod_w.shape
    return pl.pallas_call(
        _mod_kernel,
        grid=(n_layers, d6 // d),
        in_specs=[pl.BlockSpec((16, d), lambda l, j: (0, 0)),
                  pl.BlockSpec((1, d, d), lambda l, j: (l, 0, j)),
                  pl.BlockSpec((1, 1, d), lambda l, j: (l, 0, j))],
        out_specs=pl.BlockSpec((1, 16, d), lambda l, j: (l, 0, j)),
        out_shape=jax.ShapeDtypeStruct((n_layers, 16, d6), F32),
        compiler_params=_params("parallel", "parallel"),
        name="modulation",
    )(cvec, mod_w, mod_b.reshape(n_layers, 1, d6))


def _in_kernel(x_ref, mod_ref, g_ref, w_ref, o_ref):
    m = mod_ref[0, 0]
    h = _rms_mod(x_ref[0], g_ref[...], m[0:1], m[1:2])
    o_ref[0] = _dot(h, w_ref[...])


def _in_proj(x, modv, g, w_bf16, tn):
    b, s, d = x.shape
    n = w_bf16.shape[1]
    return pl.pallas_call(
        _in_kernel,
        grid=(n // tn, b, s // TM),
        in_specs=[pl.BlockSpec((1, TM, d), lambda j, bi, t: (bi, t, 0)),
                  pl.BlockSpec((1, 1, 6, d), _mod_spec(1)),
                  pl.BlockSpec((1, d), lambda j, bi, t: (0, 0)),
                  pl.BlockSpec((d, tn), lambda j, bi, t: (0, j))],
        out_specs=pl.BlockSpec((1, TM, tn), lambda j, bi, t: (bi, t, j)),
        out_shape=jax.ShapeDtypeStruct((b, s, n), F32),
        compiler_params=_params("parallel", "parallel", "parallel"),
        name="in_proj",
    )(x, modv, g.reshape(1, d), w_bf16)


def _gates_kernel(x_ref, mod_ref, g_ref, wab_ref, alog_ref, dtb_ref, gc_ref, bt_ref):
    m = mod_ref[0, 0]
    h = _rms_mod(x_ref[0], g_ref[...], m[0:1], m[1:2])
    ab = _dot_hi(h, wab_ref[...])
    nh = GDN_HEADS
    gate = -jnp.exp(alog_ref[...]) * _softplus(ab[:, :2 * nh] + dtb_ref[...])
    beta = jax.nn.sigmoid(ab[:, 2 * nh:])
    r = lax.broadcasted_iota(jnp.int32, (TM, TM), 0)
    c = lax.broadcasted_iota(jnp.int32, (TM, TM), 1)
    same = _block_of(r, CHUNK) == _block_of(c, CHUNK)
    cum_f = jnp.where(same & (c <= r), 1.0, 0.0)
    cum_b = jnp.where(same & (c >= r), 1.0, 0.0)
    gc_ref[0, 0] = _dot_hi(cum_f, gate[:, :nh])
    gc_ref[1, 0] = _dot_hi(cum_b, gate[:, nh:])
    bt_ref[0, 0] = beta[:, :nh]
    bt_ref[1, 0] = beta[:, nh:]


def _gdn_gates(x, modv, g, w_ab, a_log, dt_bias):
    b, s, d = x.shape
    nh = GDN_HEADS
    out = jax.ShapeDtypeStruct((2, b, s, nh), F32)
    return pl.pallas_call(
        _gates_kernel,
        grid=(b, s // TM),
        in_specs=[pl.BlockSpec((1, TM, d), lambda bi, t: (bi, t, 0)),
                  pl.BlockSpec((1, 1, 6, d), _mod_spec()),
                  pl.BlockSpec((1, d), lambda bi, t: (0, 0)),
                  pl.BlockSpec((d, 4 * nh), lambda bi, t: (0, 0)),
                  pl.BlockSpec((1, 2 * nh), lambda bi, t: (0, 0)),
                  pl.BlockSpec((1, 2 * nh), lambda bi, t: (0, 0))],
        out_specs=[pl.BlockSpec((2, 1, TM, nh), lambda bi, t: (0, bi, t, 0)),
                   pl.BlockSpec((2, 1, TM, nh), lambda bi, t: (0, bi, t, 0))],
        out_shape=[out, out],
        compiler_params=_params("parallel", "parallel"),
        name="gdn_gates",
    )(x, modv, g.reshape(1, d), w_ab, a_log.reshape(1, 2 * nh), dt_bias.reshape(1, 2 * nh))


def _halo_specs(width, col_of, n_tiles):
    per = TM // HALO

    def prev(bi, t, *rest):
        return (bi, jnp.maximum(t * per - 1, 0), col_of(*rest))

    def nxt(bi, t, *rest):
        return (bi, jnp.minimum((t + 1) * per, n_tiles * per - 1), col_of(*rest))

    return pl.BlockSpec((1, HALO, width), prev), pl.BlockSpec((1, HALO, width), nxt)


def _halo_valid():
    t = pl.program_id(1)
    nt = pl.num_programs(1)
    return t >= 2, (t >= 1) & (t < nt - 1)


def _conv_kernel(cur_ref, prev_ref, next_ref, w_ref, o_ref, ext_ref):
    j = pl.program_id(2)
    prev_ok, next_ok = _halo_valid()
    ext_ref[0:HALO] = jnp.where(prev_ok, prev_ref[0], 0.0)
    ext_ref[HALO:HALO + TM] = cur_ref[0]
    ext_ref[HALO + TM:] = jnp.where(next_ok, next_ref[0], 0.0)
    w = w_ref[...]
    base = HALO - GDN_CONV // 2
    acc = w[0:1] * ext_ref[base:base + TM]
    for k in range(1, GDN_CONV):
        acc = acc + w[k:k + 1] * ext_ref[base + k:base + k + TM]
    y = _silu(acc)
    width = y.shape[1]
    is_v = j >= 2 * (GDN_HEADS * GDN_DK // width)
    is_q = j < (GDN_HEADS * GDN_DK // width)
    qscale = jnp.where(is_q, GDN_DK ** -0.5, 1.0)
    for hh in range(width // GDN_DK):
        seg = y[:, hh * GDN_DK:(hh + 1) * GDN_DK]
        nrm = seg * lax.rsqrt(jnp.sum(seg * seg, axis=-1, keepdims=True) + EPS) * qscale
        o_ref[0, :, hh * GDN_DK:(hh + 1) * GDN_DK] = jnp.where(is_v, seg, nrm)


def _gdn_conv(p, conv_w):
    b, s, _ = p.shape
    n = conv_w.shape[1]
    width = GDN_HEADS * GDN_DK
    prev_spec, next_spec = _halo_specs(width, lambda j: j, s // TM)
    return pl.pallas_call(
        _conv_kernel,
        grid=(b, s // TM, n // width),
        in_specs=[pl.BlockSpec((1, TM, width), lambda bi, t, j: (bi, t, j)),
                  prev_spec, next_spec,
                  pl.BlockSpec((GDN_CONV, width), lambda bi, t, j: (0, j))],
        out_specs=pl.BlockSpec((1, TM, width), lambda bi, t, j: (bi, t, j)),
        out_shape=jax.ShapeDtypeStruct((b, s, n), F32),
        scratch_shapes=[pltpu.VMEM((TM + 2 * HALO, width), F32)],
        compiler_params=_params("parallel", "parallel", "parallel"),
        name="gdn_conv",
    )(p, p, p, conv_w)


GDN_PAIR = 2 * GDN_DK
GDN_PAIRS = GDN_HEADS // 2
INV_LANES = 128


def _pair_cols(cols, hp, width):
    lane = lax.broadcasted_iota(jnp.int32, (cols.shape[0], width), 1)
    return jnp.where(lane < width // 2, cols[:, 2 * hp:2 * hp + 1], cols[:, 2 * hp + 1:2 * hp + 2])


def _pair_blockdiag_rows(x, lane_block):
    n, w = x.shape
    r = lax.broadcasted_iota(jnp.int32, (2 * n, w), 0)
    c = lax.broadcasted_iota(jnp.int32, (2 * n, w), 1)
    same = _block_of(r, n) == jnp.bitwise_and(_block_of(c, lane_block), 1)
    return jnp.where(same, jnp.concatenate([x, x], axis=0), 0.0)


def _pair_decay(gc_cols, gct_row, hp, fwd, inclusive):
    r = lax.broadcasted_iota(jnp.int32, (CHUNK, 2 * CHUNK), 0)
    c = jnp.bitwise_and(lax.broadcasted_iota(jnp.int32, (CHUNK, 2 * CHUNK), 1), CHUNK - 1)
    ahead = (r - c) if fwd else (c - r)
    keep = (ahead >= 0) if inclusive else (ahead > 0)
    rel = _pair_cols(gc_cols, hp, 2 * CHUNK) - gct_row
    return jnp.where(keep, jnp.exp(jnp.where(keep, rel, 0.0)), 0.0)


def _gdn_a_kernel(k_ref, gc_ref, gctp_ref, bt_ref, a_ref):
    for ch in range(TM // CHUNK):
        rows = pl.ds(ch * CHUNK, CHUNK)
        for hp in range(GDN_PAIRS):
            kp = k_ref[0, rows, hp * GDN_PAIR:(hp + 1) * GDN_PAIR]
            kk = _dot_nt_hi(kp, _pair_blockdiag_rows(kp, GDN_DK))
            for d in range(2):
                decay = _pair_decay(gc_ref[d, 0, rows, :], gctp_ref[d, 0, ch, hp:hp + 1, :], hp, d == 0, False)
                a_ref[d, 0, ch, hp] = _pair_cols(bt_ref[d, 0, rows, :], hp, 2 * CHUNK) * kk * decay


def _gdn_a(qkv, gc, gctp, bt):
    b, s, _ = qkv.shape
    nh = GDN_HEADS
    cpt = TM // CHUNK
    gate_spec = pl.BlockSpec((2, 1, TM, nh), lambda bi, t: (0, bi, t, 0))
    return pl.pallas_call(
        _gdn_a_kernel,
        grid=(b, s // TM),
        in_specs=[pl.BlockSpec((1, TM, nh * GDN_DK), lambda bi, t: (bi, t, 1)),
                  gate_spec,
                  pl.BlockSpec((2, 1, cpt, GDN_PAIRS, 2 * CHUNK), lambda bi, t: (0, bi, t, 0, 0)),
                  gate_spec],
        out_specs=pl.BlockSpec((2, 1, cpt, GDN_PAIRS, CHUNK, 2 * CHUNK), lambda bi, t: (0, bi, t, 0, 0, 0)),
        out_shape=jax.ShapeDtypeStruct((2, b, s // CHUNK, GDN_PAIRS, CHUNK, 2 * CHUNK), F32),
        compiler_params=_params("parallel", "parallel"),
        name="gdn_a",
    )(qkv, gc, gctp, bt)


def _substitute_rows(at_ref, x_ref, hh):
    n = CHUNK
    zero = jnp.zeros((8, INV_LANES), F32)
    for i in range(n):
        nb = (i + 7) // 8
        acc = [-at_ref[hh, pl.ds(i * n + jb * 8, 8), :] for jb in range(nb)]
        for m in range(1, i):
            a_im = jnp.broadcast_to(at_ref[hh, pl.ds(i * n + m, 1), :], (8, INV_LANES))
            for jb in range((m + 7) // 8):
                acc[jb] = acc[jb] - a_im * x_ref[hh, pl.ds(m * n + jb * 8, 8), :]
        for jb in range(n // 8):
            x_ref[hh, pl.ds(i * n + jb * 8, 8), :] = acc[jb] if jb < nb else zero


def _gdn_inv_kernel(a_ref, t_ref, at_ref, x_ref):
    fwd = pl.program_id(0) == 0
    n = CHUNK

    def load(r, transposed):
        slab = a_ref[0, pl.ds(r, INV_LANES, stride=n), :].T
        for hh in range(2):
            dst = pl.ds(r, n, stride=n) if transposed else pl.ds(r * n, n)
            at_ref[hh, dst, :] = slab[hh * n:(hh + 1) * n]

    def store(r, transposed):
        src = pl.ds(r, n, stride=n) if transposed else pl.ds(r * n, n)
        eye = jnp.where(lax.broadcasted_iota(jnp.int32, (n, INV_LANES), 0) == r, 1.0, 0.0)
        slab = jnp.concatenate([x_ref[0, src, :] + eye, x_ref[1, src, :] + eye], axis=0)
        t_ref[0, pl.ds(r, INV_LANES, stride=n), :] = slab.T

    def rows(fn, transposed):
        for r in range(n):
            fn(r, transposed)

    @pl.when(fwd)
    def _():
        rows(load, False)

    @pl.when(jnp.logical_not(fwd))
    def _():
        rows(load, True)

    def halves(hh, carry):
        _substitute_rows(at_ref, x_ref, hh)
        return carry

    lax.fori_loop(0, 2, halves, 0)

    @pl.when(fwd)
    def _():
        rows(store, False)

    @pl.when(jnp.logical_not(fwd))
    def _():
        rows(store, True)


def _gdn_inv(a):
    shape = a.shape
    n_sys = shape[1] * shape[2] * shape[3]
    assert n_sys % INV_LANES == 0
    rows_per_step = INV_LANES * CHUNK
    t = pl.pallas_call(
        _gdn_inv_kernel,
        grid=(2, n_sys // INV_LANES),
        in_specs=[pl.BlockSpec((1, rows_per_step, 2 * CHUNK), lambda d, g: (d, g, 0))],
        out_specs=pl.BlockSpec((1, rows_per_step, 2 * CHUNK), lambda d, g: (d, g, 0)),
        out_shape=jax.ShapeDtypeStruct((2, n_sys * CHUNK, 2 * CHUNK), F32),
        scratch_shapes=[pltpu.VMEM((2, CHUNK * CHUNK, INV_LANES), F32), pltpu.VMEM((2, CHUNK * CHUNK, INV_LANES), F32)],
        compiler_params=_params("parallel", "parallel"),
        name="gdn_inv",
    )(a.reshape(2, n_sys * CHUNK, 2 * CHUNK))
    return t.reshape(shape)


def _gdn_uw_kernel(t_ref, q_ref, k_ref, v_ref, gc_ref, gctp_ref, bt_ref, u_ref, w_ref, qk_ref):
    for ch in range(TM // CHUNK):
        rows = pl.ds(ch * CHUNK, CHUNK)
        for hp in range(GDN_PAIRS):
            cols = slice(hp * GDN_PAIR, (hp + 1) * GDN_PAIR)
            kp = k_ref[0, rows, cols]
            vp = v_ref[0, rows, cols]
            qk = _dot_nt(q_ref[0, rows, cols], _pair_blockdiag_rows(kp, GDN_DK))
            for d in range(2):
                gc = gc_ref[d, 0, rows, :]
                beta = _pair_cols(bt_ref[d, 0, rows, :], hp, GDN_PAIR)
                rhs = jnp.concatenate([vp * beta, kp * (beta * jnp.exp(_pair_cols(gc, hp, GDN_PAIR)))], axis=1)
                uw = _dot(t_ref[d, 0, ch, hp], _pair_blockdiag_rows(rhs, GDN_DK))
                u_ref[d, 0, rows, cols] = uw[:, :GDN_PAIR]
                w_ref[d, 0, rows, cols] = uw[:, GDN_PAIR:].astype(BF16)
                decay = _pair_decay(gc, gctp_ref[d, 0, ch, hp:hp + 1, :], hp, d == 0, True)
                qk_ref[d, 0, ch, hp] = (qk * decay).astype(BF16)


def _gdn_uw(t, qkv, gc, gctp, bt):
    b, s, _ = qkv.shape
    nh = GDN_HEADS
    width = nh * GDN_DK
    cpt = TM // CHUNK
    gate_spec = pl.BlockSpec((2, 1, TM, nh), lambda bi, t_: (0, bi, t_, 0))
    sys_spec = pl.BlockSpec((2, 1, cpt, GDN_PAIRS, CHUNK, 2 * CHUNK), lambda bi, t_: (0, bi, t_, 0, 0, 0))
    tok_spec = pl.BlockSpec((2, 1, TM, width), lambda bi, t_: (0, bi, t_, 0))
    return pl.pallas_call(
        _gdn_uw_kernel,
        grid=(b, s // TM),
        in_specs=[sys_spec,
                  pl.BlockSpec((1, TM, width), lambda bi, t_: (bi, t_, 0)),
                  pl.BlockSpec((1, TM, width), lambda bi, t_: (bi, t_, 1)),
                  pl.BlockSpec((1, TM, width), lambda bi, t_: (bi, t_, 2)),
                  gate_spec,
                  pl.BlockSpec((2, 1, cpt, GDN_PAIRS, 2 * CHUNK), lambda bi, t_: (0, bi, t_, 0, 0)),
                  gate_spec],
        out_specs=[tok_spec, tok_spec, sys_spec],
        out_shape=[jax.ShapeDtypeStruct((2, b, s, width), F32), jax.ShapeDtypeStruct((2, b, s, width), BF16),
                   jax.ShapeDtypeStruct((2, b, s // CHUNK, GDN_PAIRS, CHUNK, 2 * CHUNK), BF16)],
        compiler_params=_params("parallel", "parallel"),
        name="gdn_uw",
    )(t, qkv, qkv, qkv, gc, gctp, bt)


def _gdn_scan_kernel(qf_ref, kf_ref, uf_ref, wf_ref, qkf_ref, gcf_ref, qb_ref, kb_ref, ub_ref, wb_ref, qkb_ref, gcb_ref,
                     of_ref, ob_ref, s_ref):
    @pl.when(pl.program_id(1) == 0)
    def _():
        s_ref[...] = jnp.zeros_like(s_ref)

    dk = GDN_DK
    zeros = jnp.zeros((dk, dk), BF16)
    directions = ((qf_ref, kf_ref, uf_ref, wf_ref, qkf_ref, gcf_ref, of_ref),
                  (qb_ref, kb_ref, ub_ref, wb_ref, qkb_ref, gcb_ref, ob_ref))
    for d, (q_ref, k_ref, u_ref, w_ref, qk_ref, gc_ref, o_ref) in enumerate(directions):
        gc_all = gc_ref[0, 0]
        glast = gc_all[CHUNK - 1:CHUNK] if d == 0 else gc_all[0:1]
        chunk_decay = jnp.exp(glast)
        for hp in range(GDN_PAIRS):
            cols = slice(hp * GDN_PAIR, (hp + 1) * GDN_PAIR)
            gcp = _pair_cols(gc_all, hp, GDN_PAIR)
            q_in = q_ref[0, :, cols] * jnp.exp(gcp)
            k_out = k_ref[0, :, cols] * jnp.exp(_pair_cols(glast, hp, GDN_PAIR) - gcp)
            sa = s_ref[d, 2 * hp]
            sb = s_ref[d, 2 * hp + 1]
            s_bd = jnp.concatenate([jnp.concatenate([sa.astype(BF16), zeros], axis=1),
                                    jnp.concatenate([zeros, sb.astype(BF16)], axis=1)], axis=0)
            both = jnp.dot(jnp.concatenate([w_ref[0, 0, :, cols], q_in.astype(BF16)], axis=0), s_bd,
                           preferred_element_type=F32)
            v_new = u_ref[0, 0, :, cols] - both[:CHUNK]
            intra = jnp.dot(qk_ref[0, 0, 0, hp], _pair_blockdiag_rows(v_new, dk).astype(BF16), preferred_element_type=F32)
            o_ref[0, :, cols] = both[CHUNK:] + intra
            upd = _dot_tn(k_out, v_new)
            s_ref[d, 2 * hp] = sa * chunk_decay[:, 2 * hp:2 * hp + 1] + upd[:dk, :dk]
            s_ref[d, 2 * hp + 1] = sb * chunk_decay[:, 2 * hp + 1:2 * hp + 2] + upd[dk:, dk:]


def _scan_tile(d, step, n_tiles, ctx_tiles):
    back = jnp.where(step < ctx_tiles, ctx_tiles - 1 - step, n_tiles + ctx_tiles - 1 - step)
    return jnp.where(d == 0, step, back)


def _gdn_scan(qkv, u, w, qk, gc):
    b, s, _ = qkv.shape
    nh, dk = GDN_HEADS, GDN_DK
    nc = s // CHUNK
    width = nh * dk
    tile = functools.partial(_scan_tile, n_tiles=nc, ctx_tiles=TM // CHUNK)

    def specs(d):
        return [pl.BlockSpec((1, CHUNK, width), lambda bi, i: (bi, tile(d, i), 0)),
                pl.BlockSpec((1, CHUNK, width), lambda bi, i: (bi, tile(d, i), 1)),
                pl.BlockSpec((1, 1, CHUNK, width), lambda bi, i: (d, bi, tile(d, i), 0)),
                pl.BlockSpec((1, 1, CHUNK, width), lambda bi, i: (d, bi, tile(d, i), 0)),
                pl.BlockSpec((1, 1, 1, GDN_PAIRS, CHUNK, 2 * CHUNK), lambda bi, i: (d, bi, tile(d, i), 0, 0, 0)),
                pl.BlockSpec((1, 1, CHUNK, nh), lambda bi, i: (d, bi, tile(d, i), 0))]

    out = jax.ShapeDtypeStruct((b, s, width), F32)
    return pl.pallas_call(
        _gdn_scan_kernel,
        grid=(b, nc),
        in_specs=specs(0) + specs(1),
        out_specs=[pl.BlockSpec((1, CHUNK, width), lambda bi, i: (bi, tile(0, i), 0)),
                   pl.BlockSpec((1, CHUNK, width), lambda bi, i: (bi, tile(1, i), 0))],
        out_shape=[out, out],
        scratch_shapes=[pltpu.VMEM((2, nh, dk, dk), F32)],
        compiler_params=_params("parallel", "arbitrary"),
        name="gdn_scan",
    )(qkv, qkv, u, w, qk, gc, qkv, qkv, u, w, qk, gc)


def _gdn_out_kernel(of_ref, ob_ref, z_ref, x_ref, mod_ref, ng_ref, w_ref, out_ref):
    m = mod_ref[0, 0]
    o = of_ref[0] + ob_ref[0]
    ng = ng_ref[...]
    parts = []
    for h in range(GDN_HEADS):
        seg = o[:, h * GDN_DK:(h + 1) * GDN_DK]
        parts.append(seg * lax.rsqrt(jnp.mean(seg * seg, axis=-1, keepdims=True) + EPS) * ng)
    y = _dot(jnp.concatenate(parts, axis=-1) * _silu(z_ref[0]), w_ref[...])
    out_ref[0] = x_ref[0] + m[2:3] * y


def _gdn_out(o_f, o_b, p, x, modv, norm_g, w_out_bf16):
    b, s, d = x.shape
    width = GDN_HEADS * GDN_DK
    return pl.pallas_call(
        _gdn_out_kernel,
        grid=(b, s // TM),
        in_specs=[pl.BlockSpec((1, TM, width), lambda bi, t: (bi, t, 0)),
                  pl.BlockSpec((1, TM, width), lambda bi, t: (bi, t, 0)),
                  pl.BlockSpec((1, TM, width), lambda bi, t: (bi, t, 3)),
                  pl.BlockSpec((1, TM, d), lambda bi, t: (bi, t, 0)),
                  pl.BlockSpec((1, 1, 6, d), _mod_spec()),
                  pl.BlockSpec((1, GDN_DK), lambda bi, t: (0, 0)),
                  pl.BlockSpec((width, d), lambda bi, t: (0, 0))],
        out_specs=pl.BlockSpec((1, TM, d), lambda bi, t: (bi, t, 0)),
        out_shape=jax.ShapeDtypeStruct((b, s, d), F32),
        compiler_params=_params("parallel", "parallel"),
        name="gdn_out",
    )(o_f, o_b, p, x, modv, norm_g.reshape(1, GDN_DK), w_out_bf16)


def _mixer_gdn(x, modv, norm_g, w_in, conv_w, a_log, dt_bias, out_norm_g, w_out):
    nh, dk = GDN_HEADS, GDN_DK
    n_main = 4 * nh * dk
    p = _in_proj(x, modv, norm_g, w_in[:, :n_main].astype(BF16), 2048)
    gc, bt = _gdn_gates(x, modv, norm_g, w_in[:, n_main:], a_log, dt_bias)
    b, s, _ = x.shape
    gctp = gc.reshape(2, b, s // CHUNK, CHUNK, GDN_PAIRS, 2).transpose(0, 1, 2, 4, 5, 3)
    gctp = gctp.reshape(2, b, s // CHUNK, GDN_PAIRS, 2 * CHUNK)
    qkv = _gdn_conv(p, conv_w)
    t = _gdn_inv(_gdn_a(qkv, gc, gctp, bt))
    u, w, qk = _gdn_uw(t, qkv, gc, gctp, bt)
    o_f, o_b = _gdn_scan(qkv, u, w, qk, gc)
    return _gdn_out(o_f, o_b, p, x, modv, out_norm_g, w_out.astype(BF16))


def _pool_kernel(x_ref, xp_ref, xn_ref, mod_ref, g_ref, w_ref, b_ref, sc_ref, o_ref, ext_ref):
    t = pl.program_id(1)
    nt = pl.num_programs(1)
    prev_ok, next_ok = _halo_valid()
    m = mod_ref[0, 0]
    g = g_ref[...]
    x = x_ref[0]
    h = _rms_mod(x, g, m[0:1], m[1:2])
    ext_ref[0:HALO] = jnp.where(prev_ok, _rms_mod(xp_ref[0], g, m[0:1], m[1:2]), 0.0)
    ext_ref[HALO:HALO + TM] = h
    ext_ref[HALO + TM:] = jnp.where(next_ok, _rms_mod(xn_ref[0], g, m[0:1], m[1:2]), 0.0)
    row = lax.broadcasted_iota(jnp.int32, (TM, 1), 0)
    pos = row + jnp.where(t == 0, 0, (t - 1) * TM)
    n_seq = jnp.where(t == 0, TM, (nt - 1) * TM)
    pg = POOL_GROUP
    for gi, win in enumerate(POOL_WINDOWS):
        lo_off = HALO - win // 2
        acc = ext_ref[lo_off:lo_off + TM, gi * pg:(gi + 1) * pg]
        for k in range(1, win):
            acc = acc + ext_ref[lo_off + k:lo_off + k + TM, gi * pg:(gi + 1) * pg]
        lo = jnp.clip(pos - win // 2, 0, n_seq)
        hi = jnp.clip(pos + win - win // 2, 0, n_seq)
        pooled = acc / (hi - lo).astype(F32) - h[:, gi * pg:(gi + 1) * pg]
        y = (_dot(pooled, w_ref[gi]) + b_ref[gi]) * sc_ref[:, gi * pg:(gi + 1) * pg]
        o_ref[0, :, gi * pg:(gi + 1) * pg] = x[:, gi * pg:(gi + 1) * pg] + m[2:3, gi * pg:(gi + 1) * pg] * y


def _mixer_pool(x, modv, norm_g, w_group, b_group, scale):
    b, s, d = x.shape
    ng, pg = len(POOL_WINDOWS), POOL_GROUP
    prev_spec, next_spec = _halo_specs(d, lambda: 0, s // TM)
    return pl.pallas_call(
        _pool_kernel,
        grid=(b, s // TM),
        in_specs=[pl.BlockSpec((1, TM, d), lambda bi, t: (bi, t, 0)),
                  prev_spec, next_spec,
                  pl.BlockSpec((1, 1, 6, d), _mod_spec()),
                  pl.BlockSpec((1, d), lambda bi, t: (0, 0)),
                  pl.BlockSpec((ng, pg, pg), lambda bi, t: (0, 0, 0)),
                  pl.BlockSpec((ng, 1, pg), lambda bi, t: (0, 0, 0)),
                  pl.BlockSpec((1, d), lambda bi, t: (0, 0))],
        out_specs=pl.BlockSpec((1, TM, d), lambda bi, t: (bi, t, 0)),
        out_shape=jax.ShapeDtypeStruct((b, s, d), F32),
        scratch_shapes=[pltpu.VMEM((TM + 2 * HALO, d), F32)],
        compiler_params=_params("parallel", "parallel"),
        name="pool_mixer",
    )(x, x, x, modv, norm_g.reshape(1, d), w_group.astype(BF16), b_group.reshape(ng, 1, pg), scale.reshape(1, d))


def _rotate(t, cos, sin_signed):
    half = RET_DK // 2
    swapped = jnp.concatenate([pltpu.roll(t[:, :half], half // 2, 1), pltpu.roll(t[:, half:], half // 2, 1)], axis=-1)
    return t * cos + swapped * sin_signed


def _ret_scan_kernel(lg_ref, q_ref, k_ref, v_ref, cos_ref, sin_ref, o_ref, s_ref):
    d = pl.program_id(1)
    step = pl.program_id(2)

    @pl.when(step == 0)
    def _():
        s_ref[...] = jnp.zeros_like(s_ref)

    fwd = d == 0
    r = lax.broadcasted_iota(jnp.int32, (TM, TM), 0)
    c = lax.broadcasted_iota(jnp.int32, (TM, TM), 1)
    rel = jnp.where(fwd, r - c, c - r).astype(F32)
    row = lax.broadcasted_iota(jnp.int32, (TM, 1), 0)
    q_pow = jnp.where(fwd, row + 1, TM - row).astype(F32)
    k_pow = jnp.where(fwd, TM - 1 - row, row).astype(F32)
    cos = cos_ref[...]
    sin = sin_ref[...]
    dk, dv = RET_DK, RET_DV
    for h in range(RET_HEADS):
        lg = jnp.full((1, 1), lg_ref[d, h], F32)
        q = _rotate(q_ref[0, :, h * dk:(h + 1) * dk], cos, sin)
        k = _rotate(k_ref[0, :, h * dk:(h + 1) * dk] * (dk ** -0.5), cos, sin)
        v = v_ref[0, :, h * dv:(h + 1) * dv]
        dmat = jnp.where(rel >= 0, jnp.exp(jnp.maximum(rel, 0.0) * lg), 0.0)
        inner = _dot_nt(q, k) * dmat
        s = s_ref[h]
        o_ref[0, 0, :, h * dv:(h + 1) * dv] = _dot(inner, v) + _dot(q * jnp.exp(q_pow * lg), s)
        s_ref[h] = s * jnp.exp(TM * lg) + _dot_tn(k * jnp.exp(k_pow * lg), v)


def _ret_scan(p, log_gamma, cos, sin):
    b, s, _ = p.shape
    nt = s // TM
    qw, vw = RET_HEADS * RET_DK, RET_HEADS * RET_DV
    tile = functools.partial(_scan_tile, n_tiles=nt, ctx_tiles=1)
    return pl.pallas_call(
        _ret_scan_kernel,
        grid=(b, 2, nt),
        in_specs=[pl.BlockSpec(memory_space=pltpu.SMEM),
                  pl.BlockSpec((1, TM, qw), lambda bi, d, i: (bi, tile(d, i), 0)),
                  pl.BlockSpec((1, TM, qw), lambda bi, d, i: (bi, tile(d, i), 1)),
                  pl.BlockSpec((1, TM, vw), lambda bi, d, i: (bi, tile(d, i), 1)),
                  pl.BlockSpec((TM, RET_DK), lambda bi, d, i: (tile(d, i), 0)),
                  pl.BlockSpec((TM, RET_DK), lambda bi, d, i: (tile(d, i), 0))],
        out_specs=pl.BlockSpec((1, 1, TM, vw), lambda bi, d, i: (d, bi, tile(d, i), 0)),
        out_shape=jax.ShapeDtypeStruct((2, b, s, vw), F32),
        scratch_shapes=[pltpu.VMEM((RET_HEADS, RET_DK, RET_DV), F32)],
        compiler_params=_params("parallel", "parallel", "arbitrary"),
        name="ret_scan",
    )(log_gamma, p, p, p, cos, sin)


def _ret_out_kernel(o_ref, gate_ref, x_ref, mod_ref, ng_ref, w_ref, out_ref):
    m = mod_ref[0, 0]
    o = o_ref[0, 0] + o_ref[1, 0]
    ng = ng_ref[...]
    parts = []
    for h in range(RET_HEADS):
        seg = o[:, h * RET_DV:(h + 1) * RET_DV]
        mu = jnp.mean(seg, axis=-1, keepdims=True)
        cen = seg - mu
        var = jnp.mean(cen * cen, axis=-1, keepdims=True)
        parts.append(cen * lax.rsqrt(var + EPS) * ng)
    y = _dot(_silu(gate_ref[0]) * jnp.concatenate(parts, axis=-1), w_ref[...])
    out_ref[0] = x_ref[0] + m[2:3] * y


def _ret_out(o, p, x, modv, norm_g, w_out_bf16):
    b, s, d = x.shape
    vw = RET_HEADS * RET_DV
    return pl.pallas_call(
        _ret_out_kernel,
        grid=(b, s // TM),
        in_specs=[pl.BlockSpec((2, 1, TM, vw), lambda bi, t: (0, bi, t, 0)),
                  pl.BlockSpec((1, TM, vw), lambda bi, t: (bi, t, 2)),
                  pl.BlockSpec((1, TM, d), lambda bi, t: (bi, t, 0)),
                  pl.BlockSpec((1, 1, 6, d), _mod_spec()),
                  pl.BlockSpec((1, RET_DV), lambda bi, t: (0, 0)),
                  pl.BlockSpec((vw, d), lambda bi, t: (0, 0))],
        out_specs=pl.BlockSpec((1, TM, d), lambda bi, t: (bi, t, 0)),
        out_shape=jax.ShapeDtypeStruct((b, s, d), F32),
        compiler_params=_params("parallel", "parallel"),
        name="ret_out",
    )(o, p, x, modv, norm_g.reshape(1, RET_DV), w_out_bf16)


def _rotary_tables(s):
    n_lat = s - TM
    pos = jnp.arange(n_lat, dtype=jnp.int32)
    rows = (pos // GRID_W).astype(F32)
    cols = (pos % GRID_W).astype(F32)
    quarter = RET_DK // 4
    inv_freq = ROPE_BASE ** (-jnp.arange(quarter, dtype=F32) / quarter)
    ang_r = rows[:, None] * inv_freq[None, :]
    ang_c = cols[:, None] * inv_freq[None, :]
    cos = jnp.concatenate([jnp.cos(ang_r), jnp.cos(ang_r), jnp.cos(ang_c), jnp.cos(ang_c)], axis=-1)
    sin = jnp.concatenate([-jnp.sin(ang_r), jnp.sin(ang_r), -jnp.sin(ang_c), jnp.sin(ang_c)], axis=-1)
    cos = jnp.concatenate([jnp.ones((TM, RET_DK), F32), cos], axis=0)
    sin = jnp.concatenate([jnp.zeros((TM, RET_DK), F32), sin], axis=0)
    return cos, sin


def _mixer_retention(x, modv, norm_g, w_in, decay_logit, out_norm_g, w_out):
    p = _in_proj(x, modv, norm_g, w_in.astype(BF16), 2048)
    cos, sin = _rotary_tables(x.shape[1])
    o = _ret_scan(p, jax.nn.log_sigmoid(decay_logit.astype(F32)), cos, sin)
    return _ret_out(o, p, x, modv, out_norm_g, w_out.astype(BF16))


ROUTE_LANES = 128
ROUTE_COLS = 8
ISSUE_UNROLL = 8


def _route_kernel(x_ref, mod_ref, g_ref, w_ref, b_ref, rt_ref, cnt_out_ref, cnt_ref):
    first = (pl.program_id(0) == 0) & (pl.program_id(1) == 0)

    @pl.when(first)
    def _():
        cnt_ref[...] = jnp.zeros_like(cnt_ref)

    m = mod_ref[0, 0]
    h = _rms_mod(x_ref[0], g_ref[...], m[3:4], m[4:5])
    logits = _dot_hi(h, w_ref[...]) + b_ref[...]
    lane = lax.broadcasted_iota(jnp.int32, (TM, ROUTE_LANES), 1)
    big = jnp.int32(ROUTE_LANES)
    neg = -jnp.inf
    glog = jnp.where((lane >= N_EXPERTS) & (lane < N_EXPERTS + N_GROUPS), logits, neg)
    gmax = jnp.max(glog, axis=-1, keepdims=True)
    gsel = jnp.min(jnp.where(glog == gmax, lane, big), axis=-1, keepdims=True) - N_EXPERTS
    p_group = 1.0 / jnp.sum(jnp.exp(glog - gmax), axis=-1, keepdims=True)
    elog = jnp.where((lane >= gsel * EXPERTS_PER_GROUP) & (lane < (gsel + 1) * EXPERTS_PER_GROUP), logits, neg)
    m1 = jnp.max(elog, axis=-1, keepdims=True)
    i1 = jnp.min(jnp.where(elog == m1, lane, big), axis=-1, keepdims=True)
    elog2 = jnp.where(lane == i1, neg, elog)
    m2 = jnp.max(elog2, axis=-1, keepdims=True)
    i2 = jnp.min(jnp.where(elog2 == m2, lane, big), axis=-1, keepdims=True)
    e2 = jnp.exp(m2 - m1)
    w1 = p_group / (1.0 + e2)
    w2 = p_group * e2 / (1.0 + e2)
    onehot = jnp.where((lane == i1) | (lane == i2), 1.0, 0.0)
    r = lax.broadcasted_iota(jnp.int32, (TM, TM), 0)
    c = lax.broadcasted_iota(jnp.int32, (TM, TM), 1)
    before = _dot(jnp.where(c < r, 1.0, 0.0), onehot) + cnt_ref[...]
    r1 = jnp.sum(jnp.where(lane == i1, before, 0.0), axis=-1, keepdims=True)
    r2 = jnp.sum(jnp.where(lane == i2, before, 0.0), axis=-1, keepdims=True)
    cnt_ref[...] = cnt_ref[...] + jnp.sum(onehot, axis=0, keepdims=True)
    cnt_out_ref[...] = cnt_ref[...]
    vals = (i1.astype(F32), i2.astype(F32), w1, w2, r1, r2)
    out = jnp.zeros((TM, ROUTE_LANES), F32)
    for pos_, val in enumerate(vals):
        out = jnp.where(lane == pos_, val, out)
    rt_ref[0] = out[:, :ROUTE_COLS]


def _route(x, modv, norm_g, w_route, b_route):
    b, s, d = x.shape
    return pl.pallas_call(
        _route_kernel,
        grid=(b, s // TM),
        in_specs=[pl.BlockSpec((1, TM, d), lambda bi, t: (bi, t, 0)),
                  pl.BlockSpec((1, 1, 6, d), _mod_spec()),
                  pl.BlockSpec((1, d), lambda bi, t: (0, 0)),
                  pl.BlockSpec((d, ROUTE_LANES), lambda bi, t: (0, 0)),
                  pl.BlockSpec((1, ROUTE_LANES), lambda bi, t: (0, 0))],
        out_specs=[pl.BlockSpec((1, TM, ROUTE_COLS), lambda bi, t: (bi, t, 0)),
                   pl.BlockSpec((1, ROUTE_LANES), lambda bi, t: (0, 0))],
        out_shape=[jax.ShapeDtypeStruct((b, s, ROUTE_COLS), F32), jax.ShapeDtypeStruct((1, ROUTE_LANES), F32)],
        scratch_shapes=[pltpu.VMEM((1, ROUTE_LANES), F32)],
        compiler_params=_params("arbitrary", "arbitrary"),
        name="moe_route",
    )(x, modv, norm_g.reshape(1, d), w_route, b_route)


def _issue_row_copies(make_copy):
    def body(i, carry):
        for k in range(2):
            make_copy(i, k).start()
        return carry

    lax.fori_loop(0, TM, body, 0, unroll=ISSUE_UNROLL)


def _dispatch_kernel(dest_ref, pe_ref, x_ref, mod_ref, g_ref, xb_ref, h_ref, z_ref, sem, zsem):
    @pl.when((pl.program_id(0) == 0) & (pl.program_id(1) == 0))
    def _():
        z_ref[...] = jnp.zeros_like(z_ref)

        def clear(e):
            start = pl.multiple_of(pe_ref[e] - MOE_BLOCK, MOE_BLOCK)
            return pltpu.make_async_copy(z_ref, xb_ref.at[pl.ds(start, MOE_BLOCK)], zsem.at[e])

        def used(e):
            return pe_ref[e] > (pe_ref[e - 1] if e else 0)

        for e in range(N_EXPERTS):
            @pl.when(used(e))
            def _():
                clear(e).start()
        for e in range(N_EXPERTS):
            @pl.when(used(e))
            def _():
                clear(e).wait()

        def clear_unused(blk, carry):
            cp = pltpu.make_async_copy(z_ref, xb_ref.at[pl.ds(pl.multiple_of(blk * MOE_BLOCK, MOE_BLOCK), MOE_BLOCK)], sem)
            cp.start()
            cp.wait()
            return carry

        lax.fori_loop(_block_of(pe_ref[N_EXPERTS - 1], MOE_BLOCK), xb_ref.shape[0] // MOE_BLOCK, clear_unused, 0)

    m = mod_ref[0, 0]
    h_ref[...] = _rms_mod(x_ref[0], g_ref[...], m[3:4], m[4:5])
    _issue_row_copies(lambda i, k: pltpu.make_async_copy(
        h_ref.at[pl.ds(i, 1)], xb_ref.at[pl.ds(dest_ref[0, 0, 2 * i + k], 1)], sem))
    for _ in range(2):
        pltpu.make_async_copy(h_ref, xb_ref.at[pl.ds(0, TM)], sem).wait()


def _dispatch(x, modv, norm_g, dest, pad_end, n_rows):
    b, s, d = x.shape
    nt = s // TM
    return pl.pallas_call(
        _dispatch_kernel,
        grid=(b, nt),
        in_specs=[pl.BlockSpec((1, 1, 2 * TM), lambda bi, t: (bi * nt + t, 0, 0), memory_space=pltpu.SMEM),
                  pl.BlockSpec(memory_space=pltpu.SMEM),
                  pl.BlockSpec((1, TM, d), lambda bi, t: (bi, t, 0)),
                  pl.BlockSpec((1, 1, 6, d), _mod_spec()),
                  pl.BlockSpec((1, d), lambda bi, t: (0, 0))],
        out_specs=pl.BlockSpec(memory_space=pl.ANY),
        out_shape=jax.ShapeDtypeStruct((n_rows, d), F32),
        scratch_shapes=[pltpu.VMEM((TM, d), F32), pltpu.VMEM((MOE_BLOCK, d), F32),
                        pltpu.SemaphoreType.DMA, pltpu.SemaphoreType.DMA((N_EXPERTS,))],
        compiler_params=_params("arbitrary", "arbitrary"),
        name="moe_dispatch",
    )(dest.reshape(b * nt, 1, 2 * TM), pad_end, x, modv, norm_g.reshape(1, d))


def _expert_kernel(be_ref, nu_ref, x_ref, wg_ref, wu_ref, wd_ref, o_ref, wg_bf, wu_bf, wd_bf):
    i = pl.program_id(0)

    @pl.when(i < nu_ref[0])
    def _():
        @pl.when((i == 0) | (be_ref[i] != be_ref[jnp.maximum(i - 1, 0)]))
        def _():
            wg_bf[...] = wg_ref[0, 0].astype(BF16)
            wu_bf[...] = wu_ref[0, 0].astype(BF16)
            wd_bf[...] = wd_ref[0, 0].astype(BF16)

        xb = x_ref[...].astype(BF16)
        act = _silu(jnp.dot(xb, wg_bf[...], preferred_element_type=F32)) * jnp.dot(xb, wu_bf[...], preferred_element_type=F32)
        o_ref[...] = jnp.dot(act.astype(BF16), wd_bf[...], preferred_element_type=F32)

    @pl.when(i >= nu_ref[0])
    def _():
        o_ref[...] = jnp.zeros_like(o_ref)


def _experts(xb, block_e, n_used, layer, w_gate, w_up, w_down):
    n_rows, d = xb.shape
    de = w_gate.shape[-1]
    nb = n_rows // MOE_BLOCK

    def last_used(i, nu):
        return jnp.minimum(i, nu[0] - 1)

    grid_spec = pltpu.PrefetchScalarGridSpec(
        num_scalar_prefetch=2,
        grid=(nb,),
        in_specs=[pl.BlockSpec((MOE_BLOCK, d), lambda i, be, nu: (last_used(i, nu), 0)),
                  pl.BlockSpec((1, 1, d, de), lambda i, be, nu: (layer, be[last_used(i, nu)], 0, 0)),
                  pl.BlockSpec((1, 1, d, de), lambda i, be, nu: (layer, be[last_used(i, nu)], 0, 0)),
                  pl.BlockSpec((1, 1, de, d), lambda i, be, nu: (layer, be[last_used(i, nu)], 0, 0))],
        out_specs=pl.BlockSpec((MOE_BLOCK, d), lambda i, be, nu: (i, 0)),
        scratch_shapes=[pltpu.VMEM((d, de), BF16), pltpu.VMEM((d, de), BF16), pltpu.VMEM((de, d), BF16)],
    )
    return pl.pallas_call(
        _expert_kernel,
        grid_spec=grid_spec,
        out_shape=jax.ShapeDtypeStruct((n_rows, d), F32),
        compiler_params=_params("arbitrary"),
        name="moe_experts",
    )(block_e, n_used, xb, w_gate, w_up, w_down)


def _combine_kernel(dest_ref, x_ref, mod_ref, rt_ref, fg_ref, yb_ref, o_ref, y_ref, sem, *, final_norm):
    _issue_row_copies(lambda i, k: pltpu.make_async_copy(
        yb_ref.at[pl.ds(dest_ref[0, 0, 2 * i + k], 1)], y_ref.at[pl.ds(k * TM + i, 1)], sem))
    pltpu.make_async_copy(yb_ref.at[pl.ds(0, 2 * TM)], y_ref, sem).wait()
    m = mod_ref[0, 0]
    rt = rt_ref[0]
    out = x_ref[0] + m[5:6] * (rt[:, 2:3] * y_ref[0:TM] + rt[:, 3:4] * y_ref[TM:2 * TM])
    if final_norm:
        out = out * lax.rsqrt(jnp.mean(out * out, axis=-1, keepdims=True) + EPS) * fg_ref[...]
    o_ref[0] = out


def _combine(x, modv, rt, dest, yb, final_g, final_norm):
    b, s, d = x.shape
    nt = s // TM
    skip = 1 if final_norm else 0
    return pl.pallas_call(
        functools.partial(_combine_kernel, final_norm=final_norm),
        grid=(b, nt - skip),
        in_specs=[pl.BlockSpec((1, 1, 2 * TM), lambda bi, t: (bi * nt + t + skip, 0, 0), memory_space=pltpu.SMEM),
                  pl.BlockSpec((1, TM, d), lambda bi, t: (bi, t + skip, 0)),
                  pl.BlockSpec((1, 1, 6, d), lambda bi, t: (bi, jnp.minimum(t + skip, 1), 0, 0)),
                  pl.BlockSpec((1, TM, ROUTE_COLS), lambda bi, t: (bi, t + skip, 0)),
                  pl.BlockSpec((1, d), lambda bi, t: (0, 0)),
                  pl.BlockSpec(memory_space=pl.ANY)],
        out_specs=pl.BlockSpec((1, TM, d), lambda bi, t: (bi, t, 0)),
        out_shape=jax.ShapeDtypeStruct((b, s - skip * TM, d), F32),
        scratch_shapes=[pltpu.VMEM((2 * TM, d), F32), pltpu.SemaphoreType.DMA],
        compiler_params=_params("arbitrary", "arbitrary"),
        name="moe_combine",
    )(dest.reshape(b * nt, 1, 2 * TM), x, modv, rt, final_g.reshape(1, d), yb)


def _hier_moe(x, modv, norm_g, layer, wg_r, bg_r, we_r, be_r, w_gate, w_up, w_down, final_g, final_norm):
    b, s, d = x.shape
    n_tok = b * s
    pad = ROUTE_LANES - N_EXPERTS - N_GROUPS
    w_route = jnp.concatenate([we_r, wg_r, jnp.zeros((d, pad), F32)], axis=1)
    b_route = jnp.concatenate([be_r, bg_r, jnp.zeros((pad,), F32)]).reshape(1, ROUTE_LANES)
    rt, cnt = _route(x, modv, norm_g, w_route, b_route)
    counts = cnt[0, :N_EXPERTS].astype(jnp.int32)
    padded = (counts + MOE_BLOCK - 1) // MOE_BLOCK * MOE_BLOCK
    pad_end = jnp.cumsum(padded)
    pad_start = pad_end - padded
    n_blocks = -(-(2 * n_tok + N_EXPERTS * (MOE_BLOCK - 1)) // MOE_BLOCK)
    rt2 = rt.reshape(n_tok, ROUTE_COLS)
    expert = rt2[:, 0:2].astype(jnp.int32)
    experts = jnp.arange(N_EXPERTS, dtype=jnp.int32)
    start_of = jnp.sum(jnp.where(expert[..., None] == experts, pad_start, 0), axis=-1)
    dest = (start_of + rt2[:, 4:6].astype(jnp.int32)).reshape(-1)
    block_start = jnp.arange(n_blocks, dtype=jnp.int32) * MOE_BLOCK
    block_e = jnp.minimum(jnp.sum((pad_end[None, :] <= block_start[:, None]).astype(jnp.int32), axis=1), N_EXPERTS - 1)
    n_used = (pad_end[-1:] // MOE_BLOCK).astype(jnp.int32)
    xb = _dispatch(x, modv, norm_g, dest, pad_end.astype(jnp.int32), n_blocks * MOE_BLOCK)
    yb = _experts(xb, block_e, n_used, layer, w_gate, w_up, w_down)
    return _combine(x, modv, rt, dest, yb, final_g, final_norm)


def kernel(x, c, ctx, c_ctx, mod_w, mod_b, norm1_g, norm2_g, final_g, gdn_w_in, gdn_conv_w, gdn_a_log, gdn_dt_bias, gdn_norm_g, gdn_w_out, pool_w, pool_b, pool_scale, ret_w_in, ret_decay_logit, ret_norm_g, ret_w_out, router_group_w, router_group_b, router_expert_w, router_expert_b, exp_w_gate, exp_w_up, exp_w_down):
    b, n_lat, d = x.shape
    depth = mod_w.shape[0]
    assert ctx.shape[1] == TM and n_lat % TM == 0 and b < 16
    xs = jnp.concatenate([ctx, x], axis=1)
    cvec = jnp.concatenate([c, c_ctx[None], jnp.zeros((15 - b, d), F32)], axis=0)
    mods = _modulation(cvec, mod_w, mod_b)
    for i in range(depth):
        j, kind = i // 3, i % 3
        lat = mods[i, :b].reshape(b, 1, 6, d)
        con = jnp.broadcast_to(mods[i, b].reshape(1, 1, 6, d), (b, 1, 6, d))
        modv = jnp.concatenate([con, lat], axis=1)
        if kind == 0:
            xs = _mixer_gdn(xs, modv, norm1_g[i], gdn_w_in[j], gdn_conv_w[j], gdn_a_log[j], gdn_dt_bias[j],
                            gdn_norm_g[j], gdn_w_out[j])
        elif kind == 1:
            xs = _mixer_pool(xs, modv, norm1_g[i], pool_w[j], pool_b[j], pool_scale[j])
        else:
            xs = _mixer_retention(xs, modv, norm1_g[i], ret_w_in[j], ret_decay_logit[j], ret_norm_g[j], ret_w_out[j])
        xs = _hier_moe(xs, modv, norm2_g[i], i, router_group_w[i], router_group_b[i], router_expert_w[i],
                       router_expert_b[i], exp_w_gate, exp_w_up, exp_w_down, final_g, i == depth - 1)
    return xs
```

```python
import functools
import math

import jax
import jax.numpy as jnp
from jax import lax
from jax.experimental import pallas as pl
from jax.experimental.pallas import tpu as pltpu

F32 = jnp.float32
BF16 = jnp.bfloat16
HIGHEST = lax.Precision.HIGHEST

EPS = 1e-6
TM = 256
HALO = 8
HALO_BF16 = 16
CHUNK = 64
GDN_HEADS = 8
GDN_DK = 128
GDN_CONV = 5
RET_HEADS = 4
RET_DK = 256
RET_DV = 512
ROPE_BASE = 10000.0
GRID_W = 64
POOL_WINDOWS = (2, 4, 8, 16)
POOL_GROUP = 256
N_GROUPS = 4
EXPERTS_PER_GROUP = 8
N_EXPERTS = 32
MOE_BLOCK = 256
V7X_VMEM_LIMIT_BYTES = 56 * 1024 * 1024


def _params(*sem):
    return pltpu.CompilerParams(dimension_semantics=sem, vmem_limit_bytes=V7X_VMEM_LIMIT_BYTES)


def _dot(a, b):
    return jnp.dot(a.astype(BF16), b.astype(BF16), preferred_element_type=F32)


def _dot_hi(a, b):
    return jnp.dot(a, b, precision=HIGHEST, preferred_element_type=F32)


def _dot_nt(a, b):
    return lax.dot_general(a.astype(BF16), b.astype(BF16), (((1,), (1,)), ((), ())), preferred_element_type=F32)


def _split(x):
    hi = x.astype(BF16)
    return hi, (x - hi.astype(F32)).astype(BF16)


def _dot3_parts(ah, al, bh, bl, dims):
    def d(p, q):
        return lax.dot_general(p, q, dims, preferred_element_type=F32)
    return d(ah, bh) + (d(ah, bl) + d(al, bh))


def _dot3(a, b):
    return _dot3_parts(*_split(a), *_split(b), (((1,), (0,)), ((), ())))


def _cumsum_dot(ones_mask, x):
    x1 = x.astype(BF16)
    r1 = x - x1.astype(F32)
    x2 = r1.astype(BF16)
    x3 = (r1 - x2.astype(F32)).astype(BF16)
    m = ones_mask.astype(BF16)
    return (jnp.dot(m, x1, preferred_element_type=F32) + jnp.dot(m, x2, preferred_element_type=F32)
            + jnp.dot(m, x3, preferred_element_type=F32))


def _dot_tn(a, b):
    return lax.dot_general(a.astype(BF16), b.astype(BF16), (((0,), (0,)), ((), ())), preferred_element_type=F32)


def _block_of(i, size):
    return jnp.right_shift(i, int(math.log2(size)))


def _silu(x):
    return x * jax.nn.sigmoid(x)


def _softplus(x):
    return jnp.maximum(x, 0.0) + jnp.log(1.0 + jnp.exp(-jnp.abs(x)))


def _rms_mod(x, g, shift, scale):
    y = x * lax.rsqrt(jnp.mean(x * x, axis=-1, keepdims=True) + EPS) * g
    return y * (1.0 + scale) + shift


def _mod_spec(grid_rank_prefix=0):
    def idx(*g):
        b, t = g[grid_rank_prefix], g[grid_rank_prefix + 1]
        return (b, jnp.minimum(t, 1), 0, 0)
    return idx


def _mod_kernel(c_ref, w_ref, b_ref, o_ref):
    o_ref[0] = _dot_hi(_silu(c_ref[...]), w_ref[0]) + b_ref[0]


def _modulation(cvec, mod_w, mod_b):
    n_layers, d, d6 = mod_w.shape
    return pl.pallas_call(
        _mod_kernel,
        grid=(n_layers, d6 // d),
        in_specs=[pl.BlockSpec((16, d), lambda l, j: (0, 0)),
                  pl.BlockSpec((1, d, d), lambda l, j: (l, 0, j)),
                  pl.BlockSpec((1, 1, d), lambda l, j: (l, 0, j))],
        out_specs=pl.BlockSpec((1, 16, d), lambda l, j: (l, 0, j)),
        out_shape=jax.ShapeDtypeStruct((n_layers, 16, d6), F32),
        compiler_params=_params("parallel", "parallel"),
        name="modulation",
    )(cvec, mod_w, mod_b.reshape(n_layers, 1, d6))


def _in_kernel(x_ref, mod_ref, g_ref, w_ref, o_ref):
    m = mod_ref[0, 0]
    h = _rms_mod(x_ref[0], g_ref[...], m[0:1], m[1:2])
    o_ref[0] = _dot(h, w_ref[...]).astype(o_ref.dtype)


def _in_proj(x, modv, g, w_bf16, tn):
    b, s, d = x.shape
    n = w_bf16.shape[1]
    return pl.pallas_call(
        _in_kernel,
        grid=(n // tn, b, s // TM),
        in_specs=[pl.BlockSpec((1, TM, d), lambda j, bi, t: (bi, t, 0)),
                  pl.BlockSpec((1, 1, 6, d), _mod_spec(1)),
                  pl.BlockSpec((1, d), lambda j, bi, t: (0, 0)),
                  pl.BlockSpec((d, tn), lambda j, bi, t: (0, j))],
        out_specs=pl.BlockSpec((1, TM, tn), lambda j, bi, t: (bi, t, j)),
        out_shape=jax.ShapeDtypeStruct((b, s, n), BF16),
        compiler_params=_params("parallel", "parallel", "parallel"),
        name="in_proj",
    )(x, modv, g.reshape(1, d), w_bf16)


def _gates_kernel(x_ref, mod_ref, g_ref, wab_ref, alog_ref, dtb_ref, gc_ref, bt_ref):
    m = mod_ref[0, 0]
    h = _rms_mod(x_ref[0], g_ref[...], m[0:1], m[1:2])
    ab = _dot3(h, wab_ref[...])
    nh = GDN_HEADS
    gate = -jnp.exp(alog_ref[...]) * _softplus(ab[:, :2 * nh] + dtb_ref[...])
    beta = jax.nn.sigmoid(ab[:, 2 * nh:])
    r = lax.broadcasted_iota(jnp.int32, (TM, TM), 0)
    c = lax.broadcasted_iota(jnp.int32, (TM, TM), 1)
    same = _block_of(r, CHUNK) == _block_of(c, CHUNK)
    cum_f = jnp.where(same & (c <= r), 1.0, 0.0)
    cum_b = jnp.where(same & (c >= r), 1.0, 0.0)
    gc_ref[0, 0] = _cumsum_dot(cum_f, gate[:, :nh])
    gc_ref[1, 0] = _cumsum_dot(cum_b, gate[:, nh:])
    bt_ref[0, 0] = beta[:, :nh]
    bt_ref[1, 0] = beta[:, nh:]


def _gdn_gates(x, modv, g, w_ab, a_log, dt_bias):
    b, s, d = x.shape
    nh = GDN_HEADS
    out = jax.ShapeDtypeStruct((2, b, s, nh), F32)
    return pl.pallas_call(
        _gates_kernel,
        grid=(b, s // TM),
        in_specs=[pl.BlockSpec((1, TM, d), lambda bi, t: (bi, t, 0)),
                  pl.BlockSpec((1, 1, 6, d), _mod_spec()),
                  pl.BlockSpec((1, d), lambda bi, t: (0, 0)),
                  pl.BlockSpec((d, 4 * nh), lambda bi, t: (0, 0)),
                  pl.BlockSpec((1, 2 * nh), lambda bi, t: (0, 0)),
                  pl.BlockSpec((1, 2 * nh), lambda bi, t: (0, 0))],
        out_specs=[pl.BlockSpec((2, 1, TM, nh), lambda bi, t: (0, bi, t, 0)),
                   pl.BlockSpec((2, 1, TM, nh), lambda bi, t: (0, bi, t, 0))],
        out_shape=[out, out],
        compiler_params=_params("parallel", "parallel"),
        name="gdn_gates",
    )(x, modv, g.reshape(1, d), w_ab, a_log.reshape(1, 2 * nh), dt_bias.reshape(1, 2 * nh))


def _halo_specs(width, col_of, n_tiles, halo=HALO):
    per = TM // halo

    def prev(bi, t, *rest):
        return (bi, jnp.maximum(t * per - 1, 0), col_of(*rest))

    def nxt(bi, t, *rest):
        return (bi, jnp.minimum((t + 1) * per, n_tiles * per - 1), col_of(*rest))

    return pl.BlockSpec((1, halo, width), prev), pl.BlockSpec((1, halo, width), nxt)


def _halo_valid():
    t = pl.program_id(1)
    nt = pl.num_programs(1)
    return t >= 2, (t >= 1) & (t < nt - 1)


def _conv_kernel(cur_ref, prev_ref, next_ref, w_ref, o_ref, ext_ref):
    j = pl.program_id(2)
    prev_ok, next_ok = _halo_valid()
    halo = HALO_BF16
    ext_ref[0:halo] = jnp.where(prev_ok, prev_ref[0].astype(F32), 0.0)
    ext_ref[halo:halo + TM] = cur_ref[0].astype(F32)
    ext_ref[halo + TM:] = jnp.where(next_ok, next_ref[0].astype(F32), 0.0)
    w = w_ref[...]
    base = halo - GDN_CONV // 2
    acc = w[0:1] * ext_ref[base:base + TM]
    for k in range(1, GDN_CONV):
        acc = acc + w[k:k + 1] * ext_ref[base + k:base + k + TM]
    y = _silu(acc)
    width = y.shape[1]
    is_v = j >= 2 * (GDN_HEADS * GDN_DK // width)
    is_q = j < (GDN_HEADS * GDN_DK // width)
    qscale = jnp.where(is_q, GDN_DK ** -0.5, 1.0)
    for hh in range(width // GDN_DK):
        seg = y[:, hh * GDN_DK:(hh + 1) * GDN_DK]
        nrm = seg * lax.rsqrt(jnp.sum(seg * seg, axis=-1, keepdims=True) + EPS) * qscale
        o_ref[0, :, hh * GDN_DK:(hh + 1) * GDN_DK] = jnp.where(is_v, seg, nrm)


def _gdn_conv(p, conv_w):
    b, s, _ = p.shape
    n = conv_w.shape[1]
    width = GDN_HEADS * GDN_DK
    prev_spec, next_spec = _halo_specs(width, lambda j: j, s // TM, HALO_BF16)
    return pl.pallas_call(
        _conv_kernel,
        grid=(b, s // TM, n // width),
        in_specs=[pl.BlockSpec((1, TM, width), lambda bi, t, j: (bi, t, j)),
                  prev_spec, next_spec,
                  pl.BlockSpec((GDN_CONV, width), lambda bi, t, j: (0, j))],
        out_specs=pl.BlockSpec((1, TM, width), lambda bi, t, j: (bi, t, j)),
        out_shape=jax.ShapeDtypeStruct((b, s, n), F32),
        scratch_shapes=[pltpu.VMEM((TM + 2 * HALO_BF16, width), F32)],
        compiler_params=_params("parallel", "parallel", "parallel"),
        name="gdn_conv",
    )(p, p, p, conv_w)


GDN_PAIR = 2 * GDN_DK
GDN_PAIRS = GDN_HEADS // 2
INV_LANES = 128
SCAN_BATCH = 2


def _pair_cols(cols, hp, width):
    lane = lax.broadcasted_iota(jnp.int32, (cols.shape[0], width), 1)
    return jnp.where(lane < width // 2, cols[:, 2 * hp:2 * hp + 1], cols[:, 2 * hp + 1:2 * hp + 2])


def _pair_blockdiag_rows(x, lane_block):
    n, w = x.shape
    r = lax.broadcasted_iota(jnp.int32, (2 * n, w), 0)
    c = lax.broadcasted_iota(jnp.int32, (2 * n, w), 1)
    same = _block_of(r, n) == jnp.bitwise_and(_block_of(c, lane_block), 1)
    return jnp.where(same, jnp.concatenate([x, x], axis=0), 0.0)


def _pair_decay(gc_cols, gct_row, hp, fwd, inclusive):
    r = lax.broadcasted_iota(jnp.int32, (CHUNK, 2 * CHUNK), 0)
    c = jnp.bitwise_and(lax.broadcasted_iota(jnp.int32, (CHUNK, 2 * CHUNK), 1), CHUNK - 1)
    ahead = (r - c) if fwd else (c - r)
    keep = (ahead >= 0) if inclusive else (ahead > 0)
    rel = _pair_cols(gc_cols, hp, 2 * CHUNK) - gct_row
    return jnp.where(keep, jnp.exp(jnp.where(keep, rel, 0.0)), 0.0)


def _gdn_a_kernel(k_ref, gc_ref, gctp_ref, bt_ref, a_ref):
    for ch in range(TM // CHUNK):
        rows = pl.ds(ch * CHUNK, CHUNK)
        for hp in range(GDN_PAIRS):
            kh, kl = _split(k_ref[0, rows, hp * GDN_PAIR:(hp + 1) * GDN_PAIR])
            kk = _dot3_parts(kh, kl, _pair_blockdiag_rows(kh, GDN_DK), _pair_blockdiag_rows(kl, GDN_DK),
                             (((1,), (1,)), ((), ())))
            for d in range(2):
                decay = _pair_decay(gc_ref[d, 0, rows, :], gctp_ref[d, 0, ch, hp:hp + 1, :], hp, d == 0, False)
                a_ref[d, 0, ch, hp] = _pair_cols(bt_ref[d, 0, rows, :], hp, 2 * CHUNK) * kk * decay


def _gdn_a(qkv, gc, gctp, bt):
    b, s, _ = qkv.shape
    nh = GDN_HEADS
    cpt = TM // CHUNK
    gate_spec = pl.BlockSpec((2, 1, TM, nh), lambda bi, t: (0, bi, t, 0))
    return pl.pallas_call(
        _gdn_a_kernel,
        grid=(b, s // TM),
        in_specs=[pl.BlockSpec((1, TM, nh * GDN_DK), lambda bi, t: (bi, t, 1)),
                  gate_spec,
                  pl.BlockSpec((2, 1, cpt, GDN_PAIRS, 2 * CHUNK), lambda bi, t: (0, bi, t, 0, 0)),
                  gate_spec],
        out_specs=pl.BlockSpec((2, 1, cpt, GDN_PAIRS, CHUNK, 2 * CHUNK), lambda bi, t: (0, bi, t, 0, 0, 0)),
        out_shape=jax.ShapeDtypeStruct((2, b, s // CHUNK, GDN_PAIRS, CHUNK, 2 * CHUNK), F32),
        compiler_params=_params("parallel", "parallel"),
        name="gdn_a",
    )(qkv, gc, gctp, bt)


def _substitute_rows(at_ref, x_ref, hh):
    n = CHUNK
    zero = jnp.zeros((8, INV_LANES), F32)
    for i in range(n):
        nb = (i + 7) // 8
        acc = [-at_ref[hh, pl.ds(i * n + jb * 8, 8), :] for jb in range(nb)]
        for m in range(1, i):
            a_im = jnp.broadcast_to(at_ref[hh, pl.ds(i * n + m, 1), :], (8, INV_LANES))
            for jb in range((m + 7) // 8):
                acc[jb] = acc[jb] - a_im * x_ref[hh, pl.ds(m * n + jb * 8, 8), :]
        for jb in range(n // 8):
            x_ref[hh, pl.ds(i * n + jb * 8, 8), :] = acc[jb] if jb < nb else zero


def _gdn_inv_kernel(a_ref, t_ref, at_ref, x_ref):
    fwd = pl.program_id(0) == 0
    n = CHUNK

    def load(r, transposed):
        slab = a_ref[0, pl.ds(r, INV_LANES, stride=n), :].T
        for hh in range(2):
            dst = pl.ds(r, n, stride=n) if transposed else pl.ds(r * n, n)
            at_ref[hh, dst, :] = slab[hh * n:(hh + 1) * n]

    def store(r, transposed):
        src = pl.ds(r, n, stride=n) if transposed else pl.ds(r * n, n)
        eye = jnp.where(lax.broadcasted_iota(jnp.int32, (n, INV_LANES), 0) == r, 1.0, 0.0)
        slab = jnp.concatenate([x_ref[0, src, :] + eye, x_ref[1, src, :] + eye], axis=0)
        t_ref[0, pl.ds(r, INV_LANES, stride=n), :] = slab.T

    def rows(fn, transposed):
        for r in range(n):
            fn(r, transposed)

    @pl.when(fwd)
    def _():
        rows(load, False)

    @pl.when(jnp.logical_not(fwd))
    def _():
        rows(load, True)

    def halves(hh, carry):
        _substitute_rows(at_ref, x_ref, hh)
        return carry

    lax.fori_loop(0, 2, halves, 0)

    @pl.when(fwd)
    def _():
        rows(store, False)

    @pl.when(jnp.logical_not(fwd))
    def _():
        rows(store, True)


def _gdn_inv(a):
    shape = a.shape
    n_sys = shape[1] * shape[2] * shape[3]
    assert n_sys % INV_LANES == 0
    rows_per_step = INV_LANES * CHUNK
    t = pl.pallas_call(
        _gdn_inv_kernel,
        grid=(2, n_sys // INV_LANES),
        in_specs=[pl.BlockSpec((1, rows_per_step, 2 * CHUNK), lambda d, g: (d, g, 0))],
        out_specs=pl.BlockSpec((1, rows_per_step, 2 * CHUNK), lambda d, g: (d, g, 0)),
        out_shape=jax.ShapeDtypeStruct((2, n_sys * CHUNK, 2 * CHUNK), F32),
        scratch_shapes=[pltpu.VMEM((2, CHUNK * CHUNK, INV_LANES), F32), pltpu.VMEM((2, CHUNK * CHUNK, INV_LANES), F32)],
        compiler_params=_params("parallel", "parallel"),
        name="gdn_inv",
    )(a.reshape(2, n_sys * CHUNK, 2 * CHUNK))
    return t.reshape(shape)


def _gdn_uw_kernel(t_ref, q_ref, k_ref, v_ref, gc_ref, gctp_ref, bt_ref, u_ref, w_ref, qk_ref):
    for ch in range(TM // CHUNK):
        rows = pl.ds(ch * CHUNK, CHUNK)
        for hp in range(GDN_PAIRS):
            cols = slice(hp * GDN_PAIR, (hp + 1) * GDN_PAIR)
            kp = k_ref[0, rows, cols]
            vp = v_ref[0, rows, cols]
            qk = _dot_nt(q_ref[0, rows, cols], _pair_blockdiag_rows(kp, GDN_DK))
            for d in range(2):
                gc = gc_ref[d, 0, rows, :]
                beta = _pair_cols(bt_ref[d, 0, rows, :], hp, GDN_PAIR)
                rhs = jnp.concatenate([vp * beta, kp * (beta * jnp.exp(_pair_cols(gc, hp, GDN_PAIR)))], axis=1)
                uw = _dot(t_ref[d, 0, ch, hp], _pair_blockdiag_rows(rhs, GDN_DK))
                u_ref[d, 0, rows, cols] = uw[:, :GDN_PAIR]
                w_ref[d, 0, rows, cols] = uw[:, GDN_PAIR:].astype(BF16)
                decay = _pair_decay(gc, gctp_ref[d, 0, ch, hp:hp + 1, :], hp, d == 0, True)
                qk_ref[d, 0, ch, hp] = (qk * decay).astype(BF16)


def _gdn_uw(t, qkv, gc, gctp, bt):
    b, s, _ = qkv.shape
    nh = GDN_HEADS
    width = nh * GDN_DK
    cpt = TM // CHUNK
    gate_spec = pl.BlockSpec((2, 1, TM, nh), lambda bi, t_: (0, bi, t_, 0))
    sys_spec = pl.BlockSpec((2, 1, cpt, GDN_PAIRS, CHUNK, 2 * CHUNK), lambda bi, t_: (0, bi, t_, 0, 0, 0))
    tok_spec = pl.BlockSpec((2, 1, TM, width), lambda bi, t_: (0, bi, t_, 0))
    return pl.pallas_call(
        _gdn_uw_kernel,
        grid=(b, s // TM),
        in_specs=[sys_spec,
                  pl.BlockSpec((1, TM, width), lambda bi, t_: (bi, t_, 0)),
                  pl.BlockSpec((1, TM, width), lambda bi, t_: (bi, t_, 1)),
                  pl.BlockSpec((1, TM, width), lambda bi, t_: (bi, t_, 2)),
                  gate_spec,
                  pl.BlockSpec((2, 1, cpt, GDN_PAIRS, 2 * CHUNK), lambda bi, t_: (0, bi, t_, 0, 0)),
                  gate_spec],
        out_specs=[tok_spec, tok_spec, sys_spec],
        out_shape=[jax.ShapeDtypeStruct((2, b, s, width), F32), jax.ShapeDtypeStruct((2, b, s, width), BF16),
                   jax.ShapeDtypeStruct((2, b, s // CHUNK, GDN_PAIRS, CHUNK, 2 * CHUNK), BF16)],
        compiler_params=_params("parallel", "parallel"),
        name="gdn_uw",
    )(t, qkv, qkv, qkv, gc, gctp, bt)


def _gdn_scan_kernel(qf_ref, kf_ref, uf_ref, wf_ref, qkf_ref, gcf_ref, qb_ref, kb_ref, ub_ref, wb_ref, qkb_ref, gcb_ref,
                     of_ref, ob_ref, s_ref):
    @pl.when(pl.program_id(1) == 0)
    def _():
        s_ref[...] = jnp.zeros_like(s_ref)

    dk = GDN_DK
    zeros = jnp.zeros((dk, dk), BF16)
    directions = ((qf_ref, kf_ref, uf_ref, wf_ref, qkf_ref, gcf_ref, of_ref),
                  (qb_ref, kb_ref, ub_ref, wb_ref, qkb_ref, gcb_ref, ob_ref))
    for bb in range(SCAN_BATCH):
        for d, (q_ref, k_ref, u_ref, w_ref, qk_ref, gc_ref, o_ref) in enumerate(directions):
            gc_all = gc_ref[0, bb]
            glast = gc_all[CHUNK - 1:CHUNK] if d == 0 else gc_all[0:1]
            chunk_decay = jnp.exp(glast)
            for hp in range(GDN_PAIRS):
                cols = slice(hp * GDN_PAIR, (hp + 1) * GDN_PAIR)
                gcp = _pair_cols(gc_all, hp, GDN_PAIR)
                q_in = q_ref[bb, :, cols] * jnp.exp(gcp)
                k_out = k_ref[bb, :, cols] * jnp.exp(_pair_cols(glast, hp, GDN_PAIR) - gcp)
                sa = s_ref[bb, d, 2 * hp]
                sb = s_ref[bb, d, 2 * hp + 1]
                s_bd = jnp.concatenate([jnp.concatenate([sa.astype(BF16), zeros], axis=1),
                                        jnp.concatenate([zeros, sb.astype(BF16)], axis=1)], axis=0)
                both = jnp.dot(jnp.concatenate([w_ref[0, bb, :, cols], q_in.astype(BF16)], axis=0), s_bd,
                               preferred_element_type=F32)
                v_new = u_ref[0, bb, :, cols] - both[:CHUNK]
                intra = jnp.dot(qk_ref[0, bb, 0, hp], _pair_blockdiag_rows(v_new, dk).astype(BF16),
                                preferred_element_type=F32)
                o_ref[bb, :, cols] = both[CHUNK:] + intra
                upd = _dot_tn(k_out, v_new)
                s_ref[bb, d, 2 * hp] = sa * chunk_decay[:, 2 * hp:2 * hp + 1] + upd[:dk, :dk]
                s_ref[bb, d, 2 * hp + 1] = sb * chunk_decay[:, 2 * hp + 1:2 * hp + 2] + upd[dk:, dk:]


def _scan_tile(d, step, n_tiles, ctx_tiles):
    back = jnp.where(step < ctx_tiles, ctx_tiles - 1 - step, n_tiles + ctx_tiles - 1 - step)
    return jnp.where(d == 0, step, back)


def _gdn_scan(qkv, u, w, qk, gc):
    b, s, _ = qkv.shape
    nh, dk = GDN_HEADS, GDN_DK
    nc = s // CHUNK
    width = nh * dk
    tile = functools.partial(_scan_tile, n_tiles=nc, ctx_tiles=TM // CHUNK)

    nb = SCAN_BATCH
    assert b % nb == 0

    def specs(d):
        return [pl.BlockSpec((nb, CHUNK, width), lambda bi, i: (bi, tile(d, i), 0)),
                pl.BlockSpec((nb, CHUNK, width), lambda bi, i: (bi, tile(d, i), 1)),
                pl.BlockSpec((1, nb, CHUNK, width), lambda bi, i: (d, bi, tile(d, i), 0)),
                pl.BlockSpec((1, nb, CHUNK, width), lambda bi, i: (d, bi, tile(d, i), 0)),
                pl.BlockSpec((1, nb, 1, GDN_PAIRS, CHUNK, 2 * CHUNK), lambda bi, i: (d, bi, tile(d, i), 0, 0, 0)),
                pl.BlockSpec((1, nb, CHUNK, nh), lambda bi, i: (d, bi, tile(d, i), 0))]

    out = jax.ShapeDtypeStruct((b, s, width), F32)
    return pl.pallas_call(
        _gdn_scan_kernel,
        grid=(b // nb, nc),
        in_specs=specs(0) + specs(1),
        out_specs=[pl.BlockSpec((nb, CHUNK, width), lambda bi, i: (bi, tile(0, i), 0)),
                   pl.BlockSpec((nb, CHUNK, width), lambda bi, i: (bi, tile(1, i), 0))],
        out_shape=[out, out],
        scratch_shapes=[pltpu.VMEM((nb, 2, nh, dk, dk), F32)],
        compiler_params=_params("parallel", "arbitrary"),
        name="gdn_scan",
    )(qkv, qkv, u, w, qk, gc, qkv, qkv, u, w, qk, gc)


def _gdn_out_kernel(of_ref, ob_ref, z_ref, x_ref, mod_ref, ng_ref, w_ref, out_ref):
    m = mod_ref[0, 0]
    o = of_ref[0] + ob_ref[0]
    ng = ng_ref[...]
    parts = []
    for h in range(GDN_HEADS):
        seg = o[:, h * GDN_DK:(h + 1) * GDN_DK]
        parts.append(seg * lax.rsqrt(jnp.mean(seg * seg, axis=-1, keepdims=True) + EPS) * ng)
    y = _dot(jnp.concatenate(parts, axis=-1) * _silu(z_ref[0].astype(F32)), w_ref[...])
    out_ref[0] = x_ref[0] + m[2:3] * y


def _gdn_out(o_f, o_b, p, x, modv, norm_g, w_out_bf16):
    b, s, d = x.shape
    width = GDN_HEADS * GDN_DK
    return pl.pallas_call(
        _gdn_out_kernel,
        grid=(b, s // TM),
        in_specs=[pl.BlockSpec((1, TM, width), lambda bi, t: (bi, t, 0)),
                  pl.BlockSpec((1, TM, width), lambda bi, t: (bi, t, 0)),
                  pl.BlockSpec((1, TM, width), lambda bi, t: (bi, t, 3)),
                  pl.BlockSpec((1, TM, d), lambda bi, t: (bi, t, 0)),
                  pl.BlockSpec((1, 1, 6, d), _mod_spec()),
                  pl.BlockSpec((1, GDN_DK), lambda bi, t: (0, 0)),
                  pl.BlockSpec((width, d), lambda bi, t: (0, 0))],
        out_specs=pl.BlockSpec((1, TM, d), lambda bi, t: (bi, t, 0)),
        out_shape=jax.ShapeDtypeStruct((b, s, d), F32),
        compiler_params=_params("parallel", "parallel"),
        name="gdn_out",
    )(o_f, o_b, p, x, modv, norm_g.reshape(1, GDN_DK), w_out_bf16)


def _mixer_gdn(x, modv, norm_g, w_in, conv_w, a_log, dt_bias, out_norm_g, w_out):
    nh, dk = GDN_HEADS, GDN_DK
    n_main = 4 * nh * dk
    p = _in_proj(x, modv, norm_g, w_in[:, :n_main].astype(BF16), 2048)
    gc, bt = _gdn_gates(x, modv, norm_g, w_in[:, n_main:], a_log, dt_bias)
    b, s, _ = x.shape
    gctp = gc.reshape(2, b, s // CHUNK, CHUNK, GDN_PAIRS, 2).transpose(0, 1, 2, 4, 5, 3)
    gctp = gctp.reshape(2, b, s // CHUNK, GDN_PAIRS, 2 * CHUNK)
    qkv = _gdn_conv(p, conv_w)
    t = _gdn_inv(_gdn_a(qkv, gc, gctp, bt))
    u, w, qk = _gdn_uw(t, qkv, gc, gctp, bt)
    o_f, o_b = _gdn_scan(qkv, u, w, qk, gc)
    return _gdn_out(o_f, o_b, p, x, modv, out_norm_g, w_out.astype(BF16))


def _pool_kernel(x_ref, xp_ref, xn_ref, mod_ref, g_ref, w_ref, b_ref, sc_ref, o_ref, ext_ref):
    t = pl.program_id(1)
    nt = pl.num_programs(1)
    prev_ok, next_ok = _halo_valid()
    m = mod_ref[0, 0]
    g = g_ref[...]
    x = x_ref[0]
    h = _rms_mod(x, g, m[0:1], m[1:2])
    ext_ref[0:HALO] = jnp.where(prev_ok, _rms_mod(xp_ref[0], g, m[0:1], m[1:2]), 0.0)
    ext_ref[HALO:HALO + TM] = h
    ext_ref[HALO + TM:] = jnp.where(next_ok, _rms_mod(xn_ref[0], g, m[0:1], m[1:2]), 0.0)
    row = lax.broadcasted_iota(jnp.int32, (TM, 1), 0)
    pos = row + jnp.where(t == 0, 0, (t - 1) * TM)
    n_seq = jnp.where(t == 0, TM, (nt - 1) * TM)
    pg = POOL_GROUP
    for gi, win in enumerate(POOL_WINDOWS):
        lo_off = HALO - win // 2
        acc = ext_ref[lo_off:lo_off + TM, gi * pg:(gi + 1) * pg]
        for k in range(1, win):
            acc = acc + ext_ref[lo_off + k:lo_off + k + TM, gi * pg:(gi + 1) * pg]
        lo = jnp.clip(pos - win // 2, 0, n_seq)
        hi = jnp.clip(pos + win - win // 2, 0, n_seq)
        pooled = acc / (hi - lo).astype(F32) - h[:, gi * pg:(gi + 1) * pg]
        y = (_dot(pooled, w_ref[gi]) + b_ref[gi]) * sc_ref[:, gi * pg:(gi + 1) * pg]
        o_ref[0, :, gi * pg:(gi + 1) * pg] = x[:, gi * pg:(gi + 1) * pg] + m[2:3, gi * pg:(gi + 1) * pg] * y


def _mixer_pool(x, modv, norm_g, w_group, b_group, scale):
    b, s, d = x.shape
    ng, pg = len(POOL_WINDOWS), POOL_GROUP
    prev_spec, next_spec = _halo_specs(d, lambda: 0, s // TM)
    return pl.pallas_call(
        _pool_kernel,
        grid=(b, s // TM),
        in_specs=[pl.BlockSpec((1, TM, d), lambda bi, t: (bi, t, 0)),
                  prev_spec, next_spec,
                  pl.BlockSpec((1, 1, 6, d), _mod_spec()),
                  pl.BlockSpec((1, d), lambda bi, t: (0, 0)),
                  pl.BlockSpec((ng, pg, pg), lambda bi, t: (0, 0, 0)),
                  pl.BlockSpec((ng, 1, pg), lambda bi, t: (0, 0, 0)),
                  pl.BlockSpec((1, d), lambda bi, t: (0, 0))],
        out_specs=pl.BlockSpec((1, TM, d), lambda bi, t: (bi, t, 0)),
        out_shape=jax.ShapeDtypeStruct((b, s, d), F32),
        scratch_shapes=[pltpu.VMEM((TM + 2 * HALO, d), F32)],
        compiler_params=_params("parallel", "parallel"),
        name="pool_mixer",
    )(x, x, x, modv, norm_g.reshape(1, d), w_group.astype(BF16), b_group.reshape(ng, 1, pg), scale.reshape(1, d))


def _rotate(t, cos, sin_signed):
    half = RET_DK // 2
    swapped = jnp.concatenate([pltpu.roll(t[:, :half], half // 2, 1), pltpu.roll(t[:, half:], half // 2, 1)], axis=-1)
    return t * cos + swapped * sin_signed


def _ret_scan_kernel(lg_ref, q_ref, k_ref, v_ref, cos_ref, sin_ref, o_ref, s_ref):
    d = pl.program_id(1)
    step = pl.program_id(2)

    @pl.when(step == 0)
    def _():
        s_ref[...] = jnp.zeros_like(s_ref)

    fwd = d == 0
    r = lax.broadcasted_iota(jnp.int32, (TM, TM), 0)
    c = lax.broadcasted_iota(jnp.int32, (TM, TM), 1)
    rel = jnp.where(fwd, r - c, c - r).astype(F32)
    row = lax.broadcasted_iota(jnp.int32, (TM, 1), 0)
    q_pow = jnp.where(fwd, row + 1, TM - row).astype(F32)
    k_pow = jnp.where(fwd, TM - 1 - row, row).astype(F32)
    cos = cos_ref[...]
    sin = sin_ref[...]
    dk, dv = RET_DK, RET_DV
    for h in range(RET_HEADS):
        lg = jnp.full((1, 1), lg_ref[d, h], F32)
        q = _rotate(q_ref[0, :, h * dk:(h + 1) * dk].astype(F32), cos, sin)
        k = _rotate(k_ref[0, :, h * dk:(h + 1) * dk].astype(F32) * (dk ** -0.5), cos, sin)
        v = v_ref[0, :, h * dv:(h + 1) * dv]
        dmat = jnp.where(rel >= 0, jnp.exp(jnp.maximum(rel, 0.0) * lg), 0.0)
        inner = _dot_nt(q, k) * dmat
        s = s_ref[h]
        o_ref[0, 0, :, h * dv:(h + 1) * dv] = _dot(inner, v) + _dot(q * jnp.exp(q_pow * lg), s)
        s_ref[h] = s * jnp.exp(TM * lg) + _dot_tn(k * jnp.exp(k_pow * lg), v)


def _ret_scan(p, log_gamma, cos, sin):
    b, s, _ = p.shape
    nt = s // TM
    qw, vw = RET_HEADS * RET_DK, RET_HEADS * RET_DV
    tile = functools.partial(_scan_tile, n_tiles=nt, ctx_tiles=1)
    return pl.pallas_call(
        _ret_scan_kernel,
        grid=(b, 2, nt),
        in_specs=[pl.BlockSpec(memory_space=pltpu.SMEM),
                  pl.BlockSpec((1, TM, qw), lambda bi, d, i: (bi, tile(d, i), 0)),
                  pl.BlockSpec((1, TM, qw), lambda bi, d, i: (bi, tile(d, i), 1)),
                  pl.BlockSpec((1, TM, vw), lambda bi, d, i: (bi, tile(d, i), 1)),
                  pl.BlockSpec((TM, RET_DK), lambda bi, d, i: (tile(d, i), 0)),
                  pl.BlockSpec((TM, RET_DK), lambda bi, d, i: (tile(d, i), 0))],
        out_specs=pl.BlockSpec((1, 1, TM, vw), lambda bi, d, i: (d, bi, tile(d, i), 0)),
        out_shape=jax.ShapeDtypeStruct((2, b, s, vw), F32),
        scratch_shapes=[pltpu.VMEM((RET_HEADS, RET_DK, RET_DV), F32)],
        compiler_params=_params("parallel", "parallel", "arbitrary"),
        name="ret_scan",
    )(log_gamma, p, p, p, cos, sin)


def _ret_out_kernel(o_ref, gate_ref, x_ref, mod_ref, ng_ref, w_ref, out_ref):
    m = mod_ref[0, 0]
    o = o_ref[0, 0] + o_ref[1, 0]
    ng = ng_ref[...]
    parts = []
    for h in range(RET_HEADS):
        seg = o[:, h * RET_DV:(h + 1) * RET_DV]
        mu = jnp.mean(seg, axis=-1, keepdims=True)
        cen = seg - mu
        var = jnp.mean(cen * cen, axis=-1, keepdims=True)
        parts.append(cen * lax.rsqrt(var + EPS) * ng)
    y = _dot(_silu(gate_ref[0].astype(F32)) * jnp.concatenate(parts, axis=-1), w_ref[...])
    out_ref[0] = x_ref[0] + m[2:3] * y


def _ret_out(o, p, x, modv, norm_g, w_out_bf16):
    b, s, d = x.shape
    vw = RET_HEADS * RET_DV
    return pl.pallas_call(
        _ret_out_kernel,
        grid=(b, s // TM),
        in_specs=[pl.BlockSpec((2, 1, TM, vw), lambda bi, t: (0, bi, t, 0)),
                  pl.BlockSpec((1, TM, vw), lambda bi, t: (bi, t, 2)),
                  pl.BlockSpec((1, TM, d), lambda bi, t: (bi, t, 0)),
                  pl.BlockSpec((1, 1, 6, d), _mod_spec()),
                  pl.BlockSpec((1, RET_DV), lambda bi, t: (0, 0)),
                  pl.BlockSpec((vw, d), lambda bi, t: (0, 0))],
        out_specs=pl.BlockSpec((1, TM, d), lambda bi, t: (bi, t, 0)),
        out_shape=jax.ShapeDtypeStruct((b, s, d), F32),
        compiler_params=_params("parallel", "parallel"),
        name="ret_out",
    )(o, p, x, modv, norm_g.reshape(1, RET_DV), w_out_bf16)


def _rotary_tables(s):
    n_lat = s - TM
    pos = jnp.arange(n_lat, dtype=jnp.int32)
    rows = (pos // GRID_W).astype(F32)
    cols = (pos % GRID_W).astype(F32)
    quarter = RET_DK // 4
    inv_freq = ROPE_BASE ** (-jnp.arange(quarter, dtype=F32) / quarter)
    ang_r = rows[:, None] * inv_freq[None, :]
    ang_c = cols[:, None] * inv_freq[None, :]
    cos = jnp.concatenate([jnp.cos(ang_r), jnp.cos(ang_r), jnp.cos(ang_c), jnp.cos(ang_c)], axis=-1)
    sin = jnp.concatenate([-jnp.sin(ang_r), jnp.sin(ang_r), -jnp.sin(ang_c), jnp.sin(ang_c)], axis=-1)
    cos = jnp.concatenate([jnp.ones((TM, RET_DK), F32), cos], axis=0)
    sin = jnp.concatenate([jnp.zeros((TM, RET_DK), F32), sin], axis=0)
    return cos, sin


def _mixer_retention(x, modv, norm_g, w_in, decay_logit, out_norm_g, w_out):
    p = _in_proj(x, modv, norm_g, w_in.astype(BF16), 2048)
    cos, sin = _rotary_tables(x.shape[1])
    o = _ret_scan(p, jax.nn.log_sigmoid(decay_logit.astype(F32)), cos, sin)
    return _ret_out(o, p, x, modv, out_norm_g, w_out.astype(BF16))


ROUTE_LANES = 128
ROUTE_COLS = 8
ISSUE_UNROLL = 8


def _route_kernel(x_ref, mod_ref, g_ref, w_ref, b_ref, rt_ref, cnt_out_ref, cnt_ref):
    first = (pl.program_id(0) == 0) & (pl.program_id(1) == 0)

    @pl.when(first)
    def _():
        cnt_ref[...] = jnp.zeros_like(cnt_ref)

    m = mod_ref[0, 0]
    h = _rms_mod(x_ref[0], g_ref[...], m[3:4], m[4:5])
    logits = _dot3(h, w_ref[...]) + b_ref[...]
    lane = lax.broadcasted_iota(jnp.int32, (TM, ROUTE_LANES), 1)
    big = jnp.int32(ROUTE_LANES)
    neg = -jnp.inf
    glog = jnp.where((lane >= N_EXPERTS) & (lane < N_EXPERTS + N_GROUPS), logits, neg)
    gmax = jnp.max(glog, axis=-1, keepdims=True)
    gsel = jnp.min(jnp.where(glog == gmax, lane, big), axis=-1, keepdims=True) - N_EXPERTS
    p_group = 1.0 / jnp.sum(jnp.exp(glog - gmax), axis=-1, keepdims=True)
    elog = jnp.where((lane >= gsel * EXPERTS_PER_GROUP) & (lane < (gsel + 1) * EXPERTS_PER_GROUP), logits, neg)
    m1 = jnp.max(elog, axis=-1, keepdims=True)
    i1 = jnp.min(jnp.where(elog == m1, lane, big), axis=-1, keepdims=True)
    elog2 = jnp.where(lane == i1, neg, elog)
    m2 = jnp.max(elog2, axis=-1, keepdims=True)
    i2 = jnp.min(jnp.where(elog2 == m2, lane, big), axis=-1, keepdims=True)
    e2 = jnp.exp(m2 - m1)
    w1 = p_group / (1.0 + e2)
    w2 = p_group * e2 / (1.0 + e2)
    onehot = jnp.where((lane == i1) | (lane == i2), 1.0, 0.0)
    r = lax.broadcasted_iota(jnp.int32, (TM, TM), 0)
    c = lax.broadcasted_iota(jnp.int32, (TM, TM), 1)
    before = _dot(jnp.where(c < r, 1.0, 0.0), onehot) + cnt_ref[...]
    r1 = jnp.sum(jnp.where(lane == i1, before, 0.0), axis=-1, keepdims=True)
    r2 = jnp.sum(jnp.where(lane == i2, before, 0.0), axis=-1, keepdims=True)
    cnt_ref[...] = cnt_ref[...] + jnp.sum(onehot, axis=0, keepdims=True)
    cnt_out_ref[...] = cnt_ref[...]
    vals = (i1.astype(F32), i2.astype(F32), w1, w2, r1, r2)
    out = jnp.zeros((TM, ROUTE_LANES), F32)
    for pos_, val in enumerate(vals):
        out = jnp.where(lane == pos_, val, out)
    rt_ref[0] = out[:, :ROUTE_COLS]


def _route(x, modv, norm_g, w_route, b_route):
    b, s, d = x.shape
    return pl.pallas_call(
        _route_kernel,
        grid=(b, s // TM),
        in_specs=[pl.BlockSpec((1, TM, d), lambda bi, t: (bi, t, 0)),
                  pl.BlockSpec((1, 1, 6, d), _mod_spec()),
                  pl.BlockSpec((1, d), lambda bi, t: (0, 0)),
                  pl.BlockSpec((d, ROUTE_LANES), lambda bi, t: (0, 0)),
                  pl.BlockSpec((1, ROUTE_LANES), lambda bi, t: (0, 0))],
        out_specs=[pl.BlockSpec((1, TM, ROUTE_COLS), lambda bi, t: (bi, t, 0)),
                   pl.BlockSpec((1, ROUTE_LANES), lambda bi, t: (0, 0))],
        out_shape=[jax.ShapeDtypeStruct((b, s, ROUTE_COLS), F32), jax.ShapeDtypeStruct((1, ROUTE_LANES), F32)],
        scratch_shapes=[pltpu.VMEM((1, ROUTE_LANES), F32)],
        compiler_params=_params("arbitrary", "arbitrary"),
        name="moe_route",
    )(x, modv, norm_g.reshape(1, d), w_route, b_route)


def _issue_row_copies(make_copy):
    def body(i, carry):
        for k in range(2):
            make_copy(i, k).start()
        return carry

    lax.fori_loop(0, TM, body, 0, unroll=ISSUE_UNROLL)


def _dispatch_kernel(dest_ref, pe_ref, x_ref, mod_ref, g_ref, xb_ref, h_ref, z_ref, sem, zsem):
    @pl.when((pl.program_id(0) == 0) & (pl.program_id(1) == 0))
    def _():
        z_ref[...] = jnp.zeros_like(z_ref)

        def clear(e):
            start = pl.multiple_of(pe_ref[e] - MOE_BLOCK, MOE_BLOCK)
            return pltpu.make_async_copy(z_ref, xb_ref.at[pl.ds(start, MOE_BLOCK)], zsem.at[e])

        def used(e):
            return pe_ref[e] > (pe_ref[e - 1] if e else 0)

        for e in range(N_EXPERTS):
            @pl.when(used(e))
            def _():
                clear(e).start()
        for e in range(N_EXPERTS):
            @pl.when(used(e))
            def _():
                clear(e).wait()

        def clear_unused(blk, carry):
            cp = pltpu.make_async_copy(z_ref, xb_ref.at[pl.ds(pl.multiple_of(blk * MOE_BLOCK, MOE_BLOCK), MOE_BLOCK)], sem)
            cp.start()
            cp.wait()
            return carry

        lax.fori_loop(_block_of(pe_ref[N_EXPERTS - 1], MOE_BLOCK), xb_ref.shape[0] // MOE_BLOCK, clear_unused, 0)

    m = mod_ref[0, 0]
    h_ref[...] = _rms_mod(x_ref[0], g_ref[...], m[3:4], m[4:5])
    _issue_row_copies(lambda i, k: pltpu.make_async_copy(
        h_ref.at[pl.ds(i, 1)], xb_ref.at[pl.ds(dest_ref[0, 0, 2 * i + k], 1)], sem))
    for _ in range(2):
        pltpu.make_async_copy(h_ref, xb_ref.at[pl.ds(0, TM)], sem).wait()


def _dispatch(x, modv, norm_g, dest, pad_end, n_rows):
    b, s, d = x.shape
    nt = s // TM
    return pl.pallas_call(
        _dispatch_kernel,
        grid=(b, nt),
        in_specs=[pl.BlockSpec((1, 1, 2 * TM), lambda bi, t: (bi * nt + t, 0, 0), memory_space=pltpu.SMEM),
                  pl.BlockSpec(memory_space=pltpu.SMEM),
                  pl.BlockSpec((1, TM, d), lambda bi, t: (bi, t, 0)),
                  pl.BlockSpec((1, 1, 6, d), _mod_spec()),
                  pl.BlockSpec((1, d), lambda bi, t: (0, 0))],
        out_specs=pl.BlockSpec(memory_space=pl.ANY),
        out_shape=jax.ShapeDtypeStruct((n_rows, d), F32),
        scratch_shapes=[pltpu.VMEM((TM, d), F32), pltpu.VMEM((MOE_BLOCK, d), F32),
                        pltpu.SemaphoreType.DMA, pltpu.SemaphoreType.DMA((N_EXPERTS,))],
        compiler_params=_params("arbitrary", "arbitrary"),
        name="moe_dispatch",
    )(dest.reshape(b * nt, 1, 2 * TM), pad_end, x, modv, norm_g.reshape(1, d))


def _expert_kernel(be_ref, nu_ref, x_ref, wg_ref, wu_ref, wd_ref, o_ref, wg_bf, wu_bf, wd_bf):
    i = pl.program_id(0)

    @pl.when(i < nu_ref[0])
    def _():
        @pl.when((i == 0) | (be_ref[i] != be_ref[jnp.maximum(i - 1, 0)]))
        def _():
            wg_bf[...] = wg_ref[0, 0].astype(BF16)
            wu_bf[...] = wu_ref[0, 0].astype(BF16)
            wd_bf[...] = wd_ref[0, 0].astype(BF16)

        xb = x_ref[...].astype(BF16)
        act = _silu(jnp.dot(xb, wg_bf[...], preferred_element_type=F32)) * jnp.dot(xb, wu_bf[...], preferred_element_type=F32)
        o_ref[...] = jnp.dot(act.astype(BF16), wd_bf[...], preferred_element_type=F32)

    @pl.when(i >= nu_ref[0])
    def _():
        o_ref[...] = jnp.zeros_like(o_ref)


def _experts(xb, block_e, n_used, layer, w_gate, w_up, w_down):
    n_rows, d = xb.shape
    de = w_gate.shape[-1]
    nb = n_rows // MOE_BLOCK

    def last_used(i, nu):
        return jnp.minimum(i, nu[0] - 1)

    grid_spec = pltpu.PrefetchScalarGridSpec(
        num_scalar_prefetch=2,
        grid=(nb,),
        in_specs=[pl.BlockSpec((MOE_BLOCK, d), lambda i, be, nu: (last_used(i, nu), 0)),
                  pl.BlockSpec((1, 1, d, de), lambda i, be, nu: (layer, be[last_used(i, nu)], 0, 0)),
                  pl.BlockSpec((1, 1, d, de), lambda i, be, nu: (layer, be[last_used(i, nu)], 0, 0)),
                  pl.BlockSpec((1, 1, de, d), lambda i, be, nu: (layer, be[last_used(i, nu)], 0, 0))],
        out_specs=pl.BlockSpec((MOE_BLOCK, d), lambda i, be, nu: (i, 0)),
        scratch_shapes=[pltpu.VMEM((d, de), BF16), pltpu.VMEM((d, de), BF16), pltpu.VMEM((de, d), BF16)],
    )
    return pl.pallas_call(
        _expert_kernel,
        grid_spec=grid_spec,
        out_shape=jax.ShapeDtypeStruct((n_rows, d), F32),
        compiler_params=_params("arbitrary"),
        name="moe_experts",
    )(block_e, n_used, xb, w_gate, w_up, w_down)


def _combine_kernel(dest_ref, x_ref, mod_ref, rt_ref, fg_ref, yb_ref, o_ref, y_ref, sem, *, final_norm):
    _issue_row_copies(lambda i, k: pltpu.make_async_copy(
        yb_ref.at[pl.ds(dest_ref[0, 0, 2 * i + k], 1)], y_ref.at[pl.ds(k * TM + i, 1)], sem))
    pltpu.make_async_copy(yb_ref.at[pl.ds(0, 2 * TM)], y_ref, sem).wait()
    m = mod_ref[0, 0]
    rt = rt_ref[0]
    out = x_ref[0] + m[5:6] * (rt[:, 2:3] * y_ref[0:TM] + rt[:, 3:4] * y_ref[TM:2 * TM])
    if final_norm:
        out = out * lax.rsqrt(jnp.mean(out * out, axis=-1, keepdims=True) + EPS) * fg_ref[...]
    o_ref[0] = out


def _combine(x, modv, rt, dest, yb, final_g, final_norm):
    b, s, d = x.shape
    nt = s // TM
    skip = 1 if final_norm else 0
    return pl.pallas_call(
        functools.partial(_combine_kernel, final_norm=final_norm),
        grid=(b, nt - skip),
        in_specs=[pl.BlockSpec((1, 1, 2 * TM), lambda bi, t: (bi * nt + t + skip, 0, 0), memory_space=pltpu.SMEM),
                  pl.BlockSpec((1, TM, d), lambda bi, t: (bi, t + skip, 0)),
                  pl.BlockSpec((1, 1, 6, d), lambda bi, t: (bi, jnp.minimum(t + skip, 1), 0, 0)),
                  pl.BlockSpec((1, TM, ROUTE_COLS), lambda bi, t: (bi, t + skip, 0)),
                  pl.BlockSpec((1, d), lambda bi, t: (0, 0)),
                  pl.BlockSpec(memory_space=pl.ANY)],
        out_specs=pl.BlockSpec((1, TM, d), lambda bi, t: (bi, t, 0)),
        out_shape=jax.ShapeDtypeStruct((b, s - skip * TM, d), F32),
        scratch_shapes=[pltpu.VMEM((2 * TM, d), F32), pltpu.SemaphoreType.DMA],
        compiler_params=_params("arbitrary", "arbitrary"),
        name="moe_combine",
    )(dest.reshape(b * nt, 1, 2 * TM), x, modv, rt, final_g.reshape(1, d), yb)


def _hier_moe(x, modv, norm_g, layer, wg_r, bg_r, we_r, be_r, w_gate, w_up, w_down, final_g, final_norm):
    b, s, d = x.shape
    n_tok = b * s
    pad = ROUTE_LANES - N_EXPERTS - N_GROUPS
    w_route = jnp.concatenate([we_r, wg_r, jnp.zeros((d, pad), F32)], axis=1)
    b_route = jnp.concatenate([be_r, bg_r, jnp.zeros((pad,), F32)]).reshape(1, ROUTE_LANES)
    rt, cnt = _route(x, modv, norm_g, w_route, b_route)
    counts = cnt[0, :N_EXPERTS].astype(jnp.int32)
    padded = (counts + MOE_BLOCK - 1) // MOE_BLOCK * MOE_BLOCK
    pad_end = jnp.cumsum(padded)
    pad_start = pad_end - padded
    n_blocks = -(-(2 * n_tok + N_EXPERTS * (MOE_BLOCK - 1)) // MOE_BLOCK)
    rt2 = rt.reshape(n_tok, ROUTE_COLS)
    expert = rt2[:, 0:2].astype(jnp.int32)
    experts = jnp.arange(N_EXPERTS, dtype=jnp.int32)
    start_of = jnp.sum(jnp.where(expert[..., None] == experts, pad_start, 0), axis=-1)
    dest = (start_of + rt2[:, 4:6].astype(jnp.int32)).reshape(-1)
    block_start = jnp.arange(n_blocks, dtype=jnp.int32) * MOE_BLOCK
    block_e = jnp.minimum(jnp.sum((pad_end[None, :] <= block_start[:, None]).astype(jnp.int32), axis=1), N_EXPERTS - 1)
    n_used = (pad_end[-1:] // MOE_BLOCK).astype(jnp.int32)
    xb = _dispatch(x, modv, norm_g, dest, pad_end.astype(jnp.int32), n_blocks * MOE_BLOCK)
    yb = _experts(xb, block_e, n_used, layer, w_gate, w_up, w_down)
    return _combine(x, modv, rt, dest, yb, final_g, final_norm)


def kernel(x, c, ctx, c_ctx, mod_w, mod_b, norm1_g, norm2_g, final_g, gdn_w_in, gdn_conv_w, gdn_a_log, gdn_dt_bias, gdn_norm_g, gdn_w_out, pool_w, pool_b, pool_scale, ret_w_in, ret_decay_logit, ret_norm_g, ret_w_out, router_group_w, router_group_b, router_expert_w, router_expert_b, exp_w_gate, exp_w_up, exp_w_down):
    b, n_lat, d = x.shape
    depth = mod_w.shape[0]
    assert ctx.shape[1] == TM and n_lat % TM == 0 and b < 16
    xs = jnp.concatenate([ctx, x], axis=1)
    cvec = jnp.concatenate([c, c_ctx[None], jnp.zeros((15 - b, d), F32)], axis=0)
    mods = _modulation(cvec, mod_w, mod_b)
    for i in range(depth):
        j, kind = i // 3, i % 3
        lat = mods[i, :b].reshape(b, 1, 6, d)
        con = jnp.broadcast_to(mods[i, b].reshape(1, 1, 6, d), (b, 1, 6, d))
        modv = jnp.concatenate([con, lat], axis=1)
        if kind == 0:
            xs = _mixer_gdn(xs, modv, norm1_g[i], gdn_w_in[j], gdn_conv_w[j], gdn_a_log[j], gdn_dt_bias[j],
                            gdn_norm_g[j], gdn_w_out[j])
        elif kind == 1:
            xs = _mixer_pool(xs, modv, norm1_g[i], pool_w[j], pool_b[j], pool_scale[j])
        else:
            xs = _mixer_retention(xs, modv, norm1_g[i], ret_w_in[j], ret_decay_logit[j], ret_norm_g[j], ret_w_out[j])
        xs = _hier_moe(xs, modv, norm2_g[i], i, router_group_w[i], router_group_b[i], router_expert_w[i],
                       router_expert_b[i], exp_w_gate, exp_w_up, exp_w_down, final_g, i == depth - 1)
    return xs
```

```python
import functools
import math

import jax
import jax.numpy as jnp
from jax import lax
from jax.experimental import pallas as pl
from jax.experimental.pallas import tpu as pltpu

F32 = jnp.float32
BF16 = jnp.bfloat16
HIGHEST = lax.Precision.HIGHEST

EPS = 1e-6
TM = 256
HALO = 8
HALO_BF16 = 16
CHUNK = 64
GDN_HEADS = 8
GDN_DK = 128
GDN_CONV = 5
RET_HEADS = 4
RET_DK = 256
RET_DV = 512
ROPE_BASE = 10000.0
GRID_W = 64
POOL_WINDOWS = (2, 4, 8, 16)
POOL_GROUP = 256
N_GROUPS = 4
EXPERTS_PER_GROUP = 8
N_EXPERTS = 32
MOE_BLOCK = 256
V7X_VMEM_LIMIT_BYTES = 56 * 1024 * 1024


def _params(*sem):
    return pltpu.CompilerParams(dimension_semantics=sem, vmem_limit_bytes=V7X_VMEM_LIMIT_BYTES)


def _dot(a, b):
    return jnp.dot(a.astype(BF16), b.astype(BF16), preferred_element_type=F32)


def _dot_hi(a, b):
    return jnp.dot(a, b, precision=HIGHEST, preferred_element_type=F32)


def _dot_nt(a, b):
    return lax.dot_general(a.astype(BF16), b.astype(BF16), (((1,), (1,)), ((), ())), preferred_element_type=F32)


def _split(x):
    hi = x.astype(BF16)
    return hi, (x - hi.astype(F32)).astype(BF16)


def _dot3_parts(ah, al, bh, bl, dims):
    def d(p, q):
        return lax.dot_general(p, q, dims, preferred_element_type=F32)
    return d(ah, bh) + (d(ah, bl) + d(al, bh))


def _dot3(a, b):
    return _dot3_parts(*_split(a), *_split(b), (((1,), (0,)), ((), ())))


def _cumsum_dot(ones_mask, x):
    x1 = x.astype(BF16)
    r1 = x - x1.astype(F32)
    x2 = r1.astype(BF16)
    x3 = (r1 - x2.astype(F32)).astype(BF16)
    m = ones_mask.astype(BF16)
    return (jnp.dot(m, x1, preferred_element_type=F32) + jnp.dot(m, x2, preferred_element_type=F32)
            + jnp.dot(m, x3, preferred_element_type=F32))


def _dot_tn(a, b):
    return lax.dot_general(a.astype(BF16), b.astype(BF16), (((0,), (0,)), ((), ())), preferred_element_type=F32)


def _block_of(i, size):
    return jnp.right_shift(i, int(math.log2(size)))


def _silu(x):
    return x * jax.nn.sigmoid(x)


def _softplus(x):
    return jnp.maximum(x, 0.0) + jnp.log(1.0 + jnp.exp(-jnp.abs(x)))


def _rms_mod(x, g, shift, scale):
    y = x * lax.rsqrt(jnp.mean(x * x, axis=-1, keepdims=True) + EPS) * g
    return y * (1.0 + scale) + shift


def _mod_spec(grid_rank_prefix=0):
    def idx(*g):
        b, t = g[grid_rank_prefix], g[grid_rank_prefix + 1]
        return (b, jnp.minimum(t, 1), 0, 0)
    return idx


def _mod_kernel(c_ref, w_ref, b_ref, o_ref):
    o_ref[0] = _dot_hi(_silu(c_ref[...]), w_ref[0]) + b_ref[0]


def _modulation(cvec, mod_w, mod_b):
    n_layers, d, d6 = mod_w.shape
    return pl.pallas_call(
        _mod_kernel,
        grid=(n_layers, d6 // d),
        in_specs=[pl.BlockSpec((16, d), lambda l, j: (0, 0)),
                  pl.BlockSpec((1, d, d), lambda l, j: (l, 0, j)),
                  pl.BlockSpec((1, 1, d), lambda l, j: (l, 0, j))],
        out_specs=pl.BlockSpec((1, 16, d), lambda l, j: (l, 0, j)),
        out_shape=jax.ShapeDtypeStruct((n_layers, 16, d6), F32),
        compiler_params=_params("parallel", "parallel"),
        name="modulation",
    )(cvec, mod_w, mod_b.reshape(n_layers, 1, d6))


def _in_kernel(x_ref, mod_ref, g_ref, w_ref, o_ref):
    m = mod_ref[0, 0]
    h = _rms_mod(x_ref[0], g_ref[...], m[0:1], m[1:2])
    o_ref[0] = _dot(h, w_ref[...]).astype(o_ref.dtype)


def _in_proj(x, modv, g, w_bf16, tn):
    b, s, d = x.shape
    n = w_bf16.shape[1]
    return pl.pallas_call(
        _in_kernel,
        grid=(n // tn, b, s // TM),
        in_specs=[pl.BlockSpec((1, TM, d), lambda j, bi, t: (bi, t, 0)),
                  pl.BlockSpec((1, 1, 6, d), _mod_spec(1)),
                  pl.BlockSpec((1, d), lambda j, bi, t: (0, 0)),
                  pl.BlockSpec((d, tn), lambda j, bi, t: (0, j))],
        out_specs=pl.BlockSpec((1, TM, tn), lambda j, bi, t: (bi, t, j)),
        out_shape=jax.ShapeDtypeStruct((b, s, n), BF16),
        compiler_params=_params("parallel", "parallel", "parallel"),
        name="in_proj",
    )(x, modv, g.reshape(1, d), w_bf16)


def _gates_kernel(x_ref, mod_ref, g_ref, wab_ref, alog_ref, dtb_ref, gc_ref, bt_ref):
    m = mod_ref[0, 0]
    h = _rms_mod(x_ref[0], g_ref[...], m[0:1], m[1:2])
    ab = _dot3(h, wab_ref[...])
    nh = GDN_HEADS
    gate = -jnp.exp(alog_ref[...]) * _softplus(ab[:, :2 * nh] + dtb_ref[...])
    beta = jax.nn.sigmoid(ab[:, 2 * nh:])
    r = lax.broadcasted_iota(jnp.int32, (TM, TM), 0)
    c = lax.broadcasted_iota(jnp.int32, (TM, TM), 1)
    same = _block_of(r, CHUNK) == _block_of(c, CHUNK)
    cum_f = jnp.where(same & (c <= r), 1.0, 0.0)
    cum_b = jnp.where(same & (c >= r), 1.0, 0.0)
    gc_ref[0, 0] = _cumsum_dot(cum_f, gate[:, :nh])
    gc_ref[1, 0] = _cumsum_dot(cum_b, gate[:, nh:])
    bt_ref[0, 0] = beta[:, :nh]
    bt_ref[1, 0] = beta[:, nh:]


def _gdn_gates(x, modv, g, w_ab, a_log, dt_bias):
    b, s, d = x.shape
    nh = GDN_HEADS
    out = jax.ShapeDtypeStruct((2, b, s, nh), F32)
    return pl.pallas_call(
        _gates_kernel,
        grid=(b, s // TM),
        in_specs=[pl.BlockSpec((1, TM, d), lambda bi, t: (bi, t, 0)),
                  pl.BlockSpec((1, 1, 6, d), _mod_spec()),
                  pl.BlockSpec((1, d), lambda bi, t: (0, 0)),
                  pl.BlockSpec((d, 4 * nh), lambda bi, t: (0, 0)),
                  pl.BlockSpec((1, 2 * nh), lambda bi, t: (0, 0)),
                  pl.BlockSpec((1, 2 * nh), lambda bi, t: (0, 0))],
        out_specs=[pl.BlockSpec((2, 1, TM, nh), lambda bi, t: (0, bi, t, 0)),
                   pl.BlockSpec((2, 1, TM, nh), lambda bi, t: (0, bi, t, 0))],
        out_shape=[out, out],
        compiler_params=_params("parallel", "parallel"),
        name="gdn_gates",
    )(x, modv, g.reshape(1, d), w_ab, a_log.reshape(1, 2 * nh), dt_bias.reshape(1, 2 * nh))


def _halo_specs(width, col_of, n_tiles, halo=HALO):
    per = TM // halo

    def prev(bi, t, *rest):
        return (bi, jnp.maximum(t * per - 1, 0), col_of(*rest))

    def nxt(bi, t, *rest):
        return (bi, jnp.minimum((t + 1) * per, n_tiles * per - 1), col_of(*rest))

    return pl.BlockSpec((1, halo, width), prev), pl.BlockSpec((1, halo, width), nxt)


def _halo_valid():
    t = pl.program_id(1)
    nt = pl.num_programs(1)
    return t >= 2, (t >= 1) & (t < nt - 1)


def _conv_kernel(cur_ref, prev_ref, next_ref, w_ref, o_ref, ext_ref):
    j = pl.program_id(2)
    prev_ok, next_ok = _halo_valid()
    halo = HALO_BF16
    ext_ref[0:halo] = jnp.where(prev_ok, prev_ref[0].astype(F32), 0.0)
    ext_ref[halo:halo + TM] = cur_ref[0].astype(F32)
    ext_ref[halo + TM:] = jnp.where(next_ok, next_ref[0].astype(F32), 0.0)
    w = w_ref[...]
    base = halo - GDN_CONV // 2
    acc = w[0:1] * ext_ref[base:base + TM]
    for k in range(1, GDN_CONV):
        acc = acc + w[k:k + 1] * ext_ref[base + k:base + k + TM]
    y = _silu(acc)
    width = y.shape[1]
    is_v = j >= 2 * (GDN_HEADS * GDN_DK // width)
    is_q = j < (GDN_HEADS * GDN_DK // width)
    qscale = jnp.where(is_q, GDN_DK ** -0.5, 1.0)
    for hh in range(width // GDN_DK):
        seg = y[:, hh * GDN_DK:(hh + 1) * GDN_DK]
        nrm = seg * lax.rsqrt(jnp.sum(seg * seg, axis=-1, keepdims=True) + EPS) * qscale
        o_ref[0, :, hh * GDN_DK:(hh + 1) * GDN_DK] = jnp.where(is_v, seg, nrm)


def _gdn_conv(p, conv_w):
    b, s, _ = p.shape
    n = conv_w.shape[1]
    width = GDN_HEADS * GDN_DK
    prev_spec, next_spec = _halo_specs(width, lambda j: j, s // TM, HALO_BF16)
    return pl.pallas_call(
        _conv_kernel,
        grid=(b, s // TM, n // width),
        in_specs=[pl.BlockSpec((1, TM, width), lambda bi, t, j: (bi, t, j)),
                  prev_spec, next_spec,
                  pl.BlockSpec((GDN_CONV, width), lambda bi, t, j: (0, j))],
        out_specs=pl.BlockSpec((1, TM, width), lambda bi, t, j: (bi, t, j)),
        out_shape=jax.ShapeDtypeStruct((b, s, n), F32),
        scratch_shapes=[pltpu.VMEM((TM + 2 * HALO_BF16, width), F32)],
        compiler_params=_params("parallel", "parallel", "parallel"),
        name="gdn_conv",
    )(p, p, p, conv_w)


GDN_PAIR = 2 * GDN_DK
GDN_PAIRS = GDN_HEADS // 2
INV_LANES = 128
SCAN_BATCH = 2


def _pair_cols(cols, hp, width):
    lane = lax.broadcasted_iota(jnp.int32, (cols.shape[0], width), 1)
    return jnp.where(lane < width // 2, cols[:, 2 * hp:2 * hp + 1], cols[:, 2 * hp + 1:2 * hp + 2])


def _pair_blockdiag_rows(x, lane_block):
    n, w = x.shape
    r = lax.broadcasted_iota(jnp.int32, (2 * n, w), 0)
    c = lax.broadcasted_iota(jnp.int32, (2 * n, w), 1)
    same = _block_of(r, n) == jnp.bitwise_and(_block_of(c, lane_block), 1)
    return jnp.where(same, jnp.concatenate([x, x], axis=0), 0.0)


def _pair_decay(gc_cols, gct_row, hp, fwd, inclusive):
    r = lax.broadcasted_iota(jnp.int32, (CHUNK, 2 * CHUNK), 0)
    c = jnp.bitwise_and(lax.broadcasted_iota(jnp.int32, (CHUNK, 2 * CHUNK), 1), CHUNK - 1)
    ahead = (r - c) if fwd else (c - r)
    keep = (ahead >= 0) if inclusive else (ahead > 0)
    rel = _pair_cols(gc_cols, hp, 2 * CHUNK) - gct_row
    return jnp.where(keep, jnp.exp(jnp.where(keep, rel, 0.0)), 0.0)


def _gdn_a_kernel(k_ref, gc_ref, gctp_ref, bt_ref, a_ref):
    for ch in range(TM // CHUNK):
        rows = pl.ds(ch * CHUNK, CHUNK)
        for hp in range(GDN_PAIRS):
            kh, kl = _split(k_ref[0, rows, hp * GDN_PAIR:(hp + 1) * GDN_PAIR])
            kk = _dot3_parts(kh, kl, _pair_blockdiag_rows(kh, GDN_DK), _pair_blockdiag_rows(kl, GDN_DK),
                             (((1,), (1,)), ((), ())))
            for d in range(2):
                decay = _pair_decay(gc_ref[d, 0, rows, :], gctp_ref[d, 0, ch, hp:hp + 1, :], hp, d == 0, False)
                a_ref[d, 0, ch, hp] = _pair_cols(bt_ref[d, 0, rows, :], hp, 2 * CHUNK) * kk * decay


def _gdn_a(qkv, gc, gctp, bt):
    b, s, _ = qkv.shape
    nh = GDN_HEADS
    cpt = TM // CHUNK
    gate_spec = pl.BlockSpec((2, 1, TM, nh), lambda bi, t: (0, bi, t, 0))
    return pl.pallas_call(
        _gdn_a_kernel,
        grid=(b, s // TM),
        in_specs=[pl.BlockSpec((1, TM, nh * GDN_DK), lambda bi, t: (bi, t, 1)),
                  gate_spec,
                  pl.BlockSpec((2, 1, cpt, GDN_PAIRS, 2 * CHUNK), lambda bi, t: (0, bi, t, 0, 0)),
                  gate_spec],
        out_specs=pl.BlockSpec((2, 1, cpt, GDN_PAIRS, CHUNK, 2 * CHUNK), lambda bi, t: (0, bi, t, 0, 0, 0)),
        out_shape=jax.ShapeDtypeStruct((2, b, s // CHUNK, GDN_PAIRS, CHUNK, 2 * CHUNK), F32),
        compiler_params=_params("parallel", "parallel"),
        name="gdn_a",
    )(qkv, gc, gctp, bt)


def _substitute_rows(at_ref, x_ref, hh):
    n = CHUNK
    zero = jnp.zeros((8, INV_LANES), F32)
    for i in range(n):
        nb = (i + 7) // 8
        acc = [-at_ref[hh, pl.ds(i * n + jb * 8, 8), :] for jb in range(nb)]
        for m in range(1, i):
            a_im = jnp.broadcast_to(at_ref[hh, pl.ds(i * n + m, 1), :], (8, INV_LANES))
            for jb in range((m + 7) // 8):
                acc[jb] = acc[jb] - a_im * x_ref[hh, pl.ds(m * n + jb * 8, 8), :]
        for jb in range(n // 8):
            x_ref[hh, pl.ds(i * n + jb * 8, 8), :] = acc[jb] if jb < nb else zero


def _gdn_inv_kernel(a_ref, t_ref, at_ref, x_ref):
    fwd = pl.program_id(0) == 0
    n = CHUNK

    def load(r, transposed):
        slab = a_ref[0, pl.ds(r, INV_LANES, stride=n), :].T
        for hh in range(2):
            dst = pl.ds(r, n, stride=n) if transposed else pl.ds(r * n, n)
            at_ref[hh, dst, :] = slab[hh * n:(hh + 1) * n]

    def store(r, transposed):
        src = pl.ds(r, n, stride=n) if transposed else pl.ds(r * n, n)
        eye = jnp.where(lax.broadcasted_iota(jnp.int32, (n, INV_LANES), 0) == r, 1.0, 0.0)
        slab = jnp.concatenate([x_ref[0, src, :] + eye, x_ref[1, src, :] + eye], axis=0)
        t_ref[0, pl.ds(r, INV_LANES, stride=n), :] = slab.T

    def rows(fn, transposed):
        for r in range(n):
            fn(r, transposed)

    @pl.when(fwd)
    def _():
        rows(load, False)

    @pl.when(jnp.logical_not(fwd))
    def _():
        rows(load, True)

    def halves(hh, carry):
        _substitute_rows(at_ref, x_ref, hh)
        return carry

    lax.fori_loop(0, 2, halves, 0)

    @pl.when(fwd)
    def _():
        rows(store, False)

    @pl.when(jnp.logical_not(fwd))
    def _():
        rows(store, True)


def _gdn_inv(a):
    shape = a.shape
    n_sys = shape[1] * shape[2] * shape[3]
    assert n_sys % INV_LANES == 0
    rows_per_step = INV_LANES * CHUNK
    t = pl.pallas_call(
        _gdn_inv_kernel,
        grid=(2, n_sys // INV_LANES),
        in_specs=[pl.BlockSpec((1, rows_per_step, 2 * CHUNK), lambda d, g: (d, g, 0))],
        out_specs=pl.BlockSpec((1, rows_per_step, 2 * CHUNK), lambda d, g: (d, g, 0)),
        out_shape=jax.ShapeDtypeStruct((2, n_sys * CHUNK, 2 * CHUNK), F32),
        scratch_shapes=[pltpu.VMEM((2, CHUNK * CHUNK, INV_LANES), F32), pltpu.VMEM((2, CHUNK * CHUNK, INV_LANES), F32)],
        compiler_params=_params("parallel", "parallel"),
        name="gdn_inv",
    )(a.reshape(2, n_sys * CHUNK, 2 * CHUNK))
    return t.reshape(shape)


def _gdn_uw_kernel(t_ref, q_ref, k_ref, v_ref, gc_ref, gctp_ref, bt_ref, u_ref, w_ref, qk_ref):
    for ch in range(TM // CHUNK):
        rows = pl.ds(ch * CHUNK, CHUNK)
        for hp in range(GDN_PAIRS):
            cols = slice(hp * GDN_PAIR, (hp + 1) * GDN_PAIR)
            kp = k_ref[0, rows, cols]
            vp = v_ref[0, rows, cols]
            qk = _dot_nt(q_ref[0, rows, cols], _pair_blockdiag_rows(kp, GDN_DK))
            for d in range(2):
                gc = gc_ref[d, 0, rows, :]
                beta = _pair_cols(bt_ref[d, 0, rows, :], hp, GDN_PAIR)
                rhs = jnp.concatenate([vp * beta, kp * (beta * jnp.exp(_pair_cols(gc, hp, GDN_PAIR)))], axis=1)
                uw = _dot(t_ref[d, 0, ch, hp], _pair_blockdiag_rows(rhs, GDN_DK))
                u_ref[d, 0, rows, cols] = uw[:, :GDN_PAIR]
                w_ref[d, 0, rows, cols] = uw[:, GDN_PAIR:].astype(BF16)
                decay = _pair_decay(gc, gctp_ref[d, 0, ch, hp:hp + 1, :], hp, d == 0, True)
                qk_ref[d, 0, ch, hp] = (qk * decay).astype(BF16)


def _gdn_uw(t, qkv, gc, gctp, bt):
    b, s, _ = qkv.shape
    nh = GDN_HEADS
    width = nh * GDN_DK
    cpt = TM // CHUNK
    gate_spec = pl.BlockSpec((2, 1, TM, nh), lambda bi, t_: (0, bi, t_, 0))
    sys_spec = pl.BlockSpec((2, 1, cpt, GDN_PAIRS, CHUNK, 2 * CHUNK), lambda bi, t_: (0, bi, t_, 0, 0, 0))
    tok_spec = pl.BlockSpec((2, 1, TM, width), lambda bi, t_: (0, bi, t_, 0))
    return pl.pallas_call(
        _gdn_uw_kernel,
        grid=(b, s // TM),
        in_specs=[sys_spec,
                  pl.BlockSpec((1, TM, width), lambda bi, t_: (bi, t_, 0)),
                  pl.BlockSpec((1, TM, width), lambda bi, t_: (bi, t_, 1)),
                  pl.BlockSpec((1, TM, width), lambda bi, t_: (bi, t_, 2)),
                  gate_spec,
                  pl.BlockSpec((2, 1, cpt, GDN_PAIRS, 2 * CHUNK), lambda bi, t_: (0, bi, t_, 0, 0)),
                  gate_spec],
        out_specs=[tok_spec, tok_spec, sys_spec],
        out_shape=[jax.ShapeDtypeStruct((2, b, s, width), F32), jax.ShapeDtypeStruct((2, b, s, width), BF16),
                   jax.ShapeDtypeStruct((2, b, s // CHUNK, GDN_PAIRS, CHUNK, 2 * CHUNK), BF16)],
        compiler_params=_params("parallel", "parallel"),
        name="gdn_uw",
    )(t, qkv, qkv, qkv, gc, gctp, bt)


def _gdn_scan_kernel(qf_ref, kf_ref, uf_ref, wf_ref, qkf_ref, gcf_ref, qb_ref, kb_ref, ub_ref, wb_ref, qkb_ref, gcb_ref,
                     of_ref, ob_ref, s_ref):
    @pl.when(pl.program_id(1) == 0)
    def _():
        s_ref[...] = jnp.zeros_like(s_ref)

    dk = GDN_DK
    zeros = jnp.zeros((dk, dk), BF16)
    directions = ((qf_ref, kf_ref, uf_ref, wf_ref, qkf_ref, gcf_ref, of_ref),
                  (qb_ref, kb_ref, ub_ref, wb_ref, qkb_ref, gcb_ref, ob_ref))
    for bb in range(SCAN_BATCH):
        for d, (q_ref, k_ref, u_ref, w_ref, qk_ref, gc_ref, o_ref) in enumerate(directions):
            gc_all = gc_ref[0, bb]
            glast = gc_all[CHUNK - 1:CHUNK] if d == 0 else gc_all[0:1]
            chunk_decay = jnp.exp(glast)
            for hp in range(GDN_PAIRS):
                cols = slice(hp * GDN_PAIR, (hp + 1) * GDN_PAIR)
                gcp = _pair_cols(gc_all, hp, GDN_PAIR)
                q_in = q_ref[bb, :, cols] * jnp.exp(gcp)
                k_out = k_ref[bb, :, cols] * jnp.exp(_pair_cols(glast, hp, GDN_PAIR) - gcp)
                sa = s_ref[bb, d, 2 * hp]
                sb = s_ref[bb, d, 2 * hp + 1]
                s_bd = jnp.concatenate([jnp.concatenate([sa.astype(BF16), zeros], axis=1),
                                        jnp.concatenate([zeros, sb.astype(BF16)], axis=1)], axis=0)
                both = jnp.dot(jnp.concatenate([w_ref[0, bb, :, cols], q_in.astype(BF16)], axis=0), s_bd,
                               preferred_element_type=F32)
                v_new = u_ref[0, bb, :, cols] - both[:CHUNK]
                intra = jnp.dot(qk_ref[0, bb, 0, hp], _pair_blockdiag_rows(v_new, dk).astype(BF16),
                                preferred_element_type=F32)
                o_ref[bb, :, cols] = both[CHUNK:] + intra
                upd = _dot_tn(k_out, v_new)
                s_ref[bb, d, 2 * hp] = sa * chunk_decay[:, 2 * hp:2 * hp + 1] + upd[:dk, :dk]
                s_ref[bb, d, 2 * hp + 1] = sb * chunk_decay[:, 2 * hp + 1:2 * hp + 2] + upd[dk:, dk:]


def _scan_tile(d, step, n_tiles, ctx_tiles):
    back = jnp.where(step < ctx_tiles, ctx_tiles - 1 - step, n_tiles + ctx_tiles - 1 - step)
    return jnp.where(d == 0, step, back)


def _gdn_scan(qkv, u, w, qk, gc):
    b, s, _ = qkv.shape
    nh, dk = GDN_HEADS, GDN_DK
    nc = s // CHUNK
    width = nh * dk
    tile = functools.partial(_scan_tile, n_tiles=nc, ctx_tiles=TM // CHUNK)

    nb = SCAN_BATCH
    assert b % nb == 0

    def specs(d):
        return [pl.BlockSpec((nb, CHUNK, width), lambda bi, i: (bi, tile(d, i), 0)),
                pl.BlockSpec((nb, CHUNK, width), lambda bi, i: (bi, tile(d, i), 1)),
                pl.BlockSpec((1, nb, CHUNK, width), lambda bi, i: (d, bi, tile(d, i), 0)),
                pl.BlockSpec((1, nb, CHUNK, width), lambda bi, i: (d, bi, tile(d, i), 0)),
                pl.BlockSpec((1, nb, 1, GDN_PAIRS, CHUNK, 2 * CHUNK), lambda bi, i: (d, bi, tile(d, i), 0, 0, 0)),
                pl.BlockSpec((1, nb, CHUNK, nh), lambda bi, i: (d, bi, tile(d, i), 0))]

    out = jax.ShapeDtypeStruct((b, s, width), F32)
    return pl.pallas_call(
        _gdn_scan_kernel,
        grid=(b // nb, nc),
        in_specs=specs(0) + specs(1),
        out_specs=[pl.BlockSpec((nb, CHUNK, width), lambda bi, i: (bi, tile(0, i), 0)),
                   pl.BlockSpec((nb, CHUNK, width), lambda bi, i: (bi, tile(1, i), 0))],
        out_shape=[out, out],
        scratch_shapes=[pltpu.VMEM((nb, 2, nh, dk, dk), F32)],
        compiler_params=_params("parallel", "arbitrary"),
        name="gdn_scan",
    )(qkv, qkv, u, w, qk, gc, qkv, qkv, u, w, qk, gc)


def _gdn_out_kernel(of_ref, ob_ref, z_ref, x_ref, mod_ref, ng_ref, w_ref, out_ref):
    m = mod_ref[0, 0]
    o = of_ref[0] + ob_ref[0]
    ng = ng_ref[...]
    parts = []
    for h in range(GDN_HEADS):
        seg = o[:, h * GDN_DK:(h + 1) * GDN_DK]
        parts.append(seg * lax.rsqrt(jnp.mean(seg * seg, axis=-1, keepdims=True) + EPS) * ng)
    y = _dot(jnp.concatenate(parts, axis=-1) * _silu(z_ref[0].astype(F32)), w_ref[...])
    out_ref[0] = x_ref[0] + m[2:3] * y


def _gdn_out(o_f, o_b, p, x, modv, norm_g, w_out_bf16):
    b, s, d = x.shape
    width = GDN_HEADS * GDN_DK
    return pl.pallas_call(
        _gdn_out_kernel,
        grid=(b, s // TM),
        in_specs=[pl.BlockSpec((1, TM, width), lambda bi, t: (bi, t, 0)),
                  pl.BlockSpec((1, TM, width), lambda bi, t: (bi, t, 0)),
                  pl.BlockSpec((1, TM, width), lambda bi, t: (bi, t, 3)),
                  pl.BlockSpec((1, TM, d), lambda bi, t: (bi, t, 0)),
                  pl.BlockSpec((1, 1, 6, d), _mod_spec()),
                  pl.BlockSpec((1, GDN_DK), lambda bi, t: (0, 0)),
                  pl.BlockSpec((width, d), lambda bi, t: (0, 0))],
        out_specs=pl.BlockSpec((1, TM, d), lambda bi, t: (bi, t, 0)),
        out_shape=jax.ShapeDtypeStruct((b, s, d), F32),
        compiler_params=_params("parallel", "parallel"),
        name="gdn_out",
    )(o_f, o_b, p, x, modv, norm_g.reshape(1, GDN_DK), w_out_bf16)


def _mixer_gdn(x, modv, norm_g, w_in, conv_w, a_log, dt_bias, out_norm_g, w_out):
    nh, dk = GDN_HEADS, GDN_DK
    n_main = 4 * nh * dk
    p = _in_proj(x, modv, norm_g, w_in[:, :n_main].astype(BF16), n_main)
    gc, bt = _gdn_gates(x, modv, norm_g, w_in[:, n_main:], a_log, dt_bias)
    b, s, _ = x.shape
    gctp = gc.reshape(2, b, s // CHUNK, CHUNK, GDN_PAIRS, 2).transpose(0, 1, 2, 4, 5, 3)
    gctp = gctp.reshape(2, b, s // CHUNK, GDN_PAIRS, 2 * CHUNK)
    qkv = _gdn_conv(p, conv_w)
    t = _gdn_inv(_gdn_a(qkv, gc, gctp, bt))
    u, w, qk = _gdn_uw(t, qkv, gc, gctp, bt)
    o_f, o_b = _gdn_scan(qkv, u, w, qk, gc)
    return _gdn_out(o_f, o_b, p, x, modv, out_norm_g, w_out.astype(BF16))


def _pool_kernel(x_ref, xp_ref, xn_ref, mod_ref, g_ref, w_ref, b_ref, sc_ref, o_ref, ext_ref):
    t = pl.program_id(1)
    nt = pl.num_programs(1)
    prev_ok, next_ok = _halo_valid()
    m = mod_ref[0, 0]
    g = g_ref[...]
    x = x_ref[0]
    h = _rms_mod(x, g, m[0:1], m[1:2])
    ext_ref[0:HALO] = jnp.where(prev_ok, _rms_mod(xp_ref[0], g, m[0:1], m[1:2]), 0.0)
    ext_ref[HALO:HALO + TM] = h
    ext_ref[HALO + TM:] = jnp.where(next_ok, _rms_mod(xn_ref[0], g, m[0:1], m[1:2]), 0.0)
    row = lax.broadcasted_iota(jnp.int32, (TM, 1), 0)
    pos = row + jnp.where(t == 0, 0, (t - 1) * TM)
    n_seq = jnp.where(t == 0, TM, (nt - 1) * TM)
    pg = POOL_GROUP
    for gi, win in enumerate(POOL_WINDOWS):
        lo_off = HALO - win // 2
        acc = ext_ref[lo_off:lo_off + TM, gi * pg:(gi + 1) * pg]
        for k in range(1, win):
            acc = acc + ext_ref[lo_off + k:lo_off + k + TM, gi * pg:(gi + 1) * pg]
        lo = jnp.clip(pos - win // 2, 0, n_seq)
        hi = jnp.clip(pos + win - win // 2, 0, n_seq)
        pooled = acc / (hi - lo).astype(F32) - h[:, gi * pg:(gi + 1) * pg]
        y = (_dot(pooled, w_ref[gi]) + b_ref[gi]) * sc_ref[:, gi * pg:(gi + 1) * pg]
        o_ref[0, :, gi * pg:(gi + 1) * pg] = x[:, gi * pg:(gi + 1) * pg] + m[2:3, gi * pg:(gi + 1) * pg] * y


def _mixer_pool(x, modv, norm_g, w_group, b_group, scale):
    b, s, d = x.shape
    ng, pg = len(POOL_WINDOWS), POOL_GROUP
    prev_spec, next_spec = _halo_specs(d, lambda: 0, s // TM)
    return pl.pallas_call(
        _pool_kernel,
        grid=(b, s // TM),
        in_specs=[pl.BlockSpec((1, TM, d), lambda bi, t: (bi, t, 0)),
                  prev_spec, next_spec,
                  pl.BlockSpec((1, 1, 6, d), _mod_spec()),
                  pl.BlockSpec((1, d), lambda bi, t: (0, 0)),
                  pl.BlockSpec((ng, pg, pg), lambda bi, t: (0, 0, 0)),
                  pl.BlockSpec((ng, 1, pg), lambda bi, t: (0, 0, 0)),
                  pl.BlockSpec((1, d), lambda bi, t: (0, 0))],
        out_specs=pl.BlockSpec((1, TM, d), lambda bi, t: (bi, t, 0)),
        out_shape=jax.ShapeDtypeStruct((b, s, d), F32),
        scratch_shapes=[pltpu.VMEM((TM + 2 * HALO, d), F32)],
        compiler_params=_params("parallel", "parallel"),
        name="pool_mixer",
    )(x, x, x, modv, norm_g.reshape(1, d), w_group.astype(BF16), b_group.reshape(ng, 1, pg), scale.reshape(1, d))


def _rotate(t, cos, sin_signed):
    half = RET_DK // 2
    swapped = jnp.concatenate([pltpu.roll(t[:, :half], half // 2, 1), pltpu.roll(t[:, half:], half // 2, 1)], axis=-1)
    return t * cos + swapped * sin_signed


def _ret_scan_kernel(lg_ref, q_ref, k_ref, v_ref, cos_ref, sin_ref, o_ref, s_ref):
    d = pl.program_id(1)
    step = pl.program_id(2)

    @pl.when(step == 0)
    def _():
        s_ref[...] = jnp.zeros_like(s_ref)

    fwd = d == 0
    r = lax.broadcasted_iota(jnp.int32, (TM, TM), 0)
    c = lax.broadcasted_iota(jnp.int32, (TM, TM), 1)
    rel = jnp.where(fwd, r - c, c - r).astype(F32)
    row = lax.broadcasted_iota(jnp.int32, (TM, 1), 0)
    q_pow = jnp.where(fwd, row + 1, TM - row).astype(F32)
    k_pow = jnp.where(fwd, TM - 1 - row, row).astype(F32)
    cos = cos_ref[...]
    sin = sin_ref[...]
    dk, dv = RET_DK, RET_DV
    for h in range(RET_HEADS):
        lg = jnp.full((1, 1), lg_ref[d, h], F32)
        q = _rotate(q_ref[0, :, h * dk:(h + 1) * dk].astype(F32), cos, sin)
        k = _rotate(k_ref[0, :, h * dk:(h + 1) * dk].astype(F32) * (dk ** -0.5), cos, sin)
        v = v_ref[0, :, h * dv:(h + 1) * dv]
        dmat = jnp.where(rel >= 0, jnp.exp(jnp.maximum(rel, 0.0) * lg), 0.0)
        inner = _dot_nt(q, k) * dmat
        s = s_ref[h]
        o_ref[0, 0, :, h * dv:(h + 1) * dv] = _dot(inner, v) + _dot(q * jnp.exp(q_pow * lg), s)
        s_ref[h] = s * jnp.exp(TM * lg) + _dot_tn(k * jnp.exp(k_pow * lg), v)


def _ret_scan(p, log_gamma, cos, sin):
    b, s, _ = p.shape
    nt = s // TM
    qw, vw = RET_HEADS * RET_DK, RET_HEADS * RET_DV
    tile = functools.partial(_scan_tile, n_tiles=nt, ctx_tiles=1)
    return pl.pallas_call(
        _ret_scan_kernel,
        grid=(b, 2, nt),
        in_specs=[pl.BlockSpec(memory_space=pltpu.SMEM),
                  pl.BlockSpec((1, TM, qw), lambda bi, d, i: (bi, tile(d, i), 0)),
                  pl.BlockSpec((1, TM, qw), lambda bi, d, i: (bi, tile(d, i), 1)),
                  pl.BlockSpec((1, TM, vw), lambda bi, d, i: (bi, tile(d, i), 1)),
                  pl.BlockSpec((TM, RET_DK), lambda bi, d, i: (tile(d, i), 0)),
                  pl.BlockSpec((TM, RET_DK), lambda bi, d, i: (tile(d, i), 0))],
        out_specs=pl.BlockSpec((1, 1, TM, vw), lambda bi, d, i: (d, bi, tile(d, i), 0)),
        out_shape=jax.ShapeDtypeStruct((2, b, s, vw), F32),
        scratch_shapes=[pltpu.VMEM((RET_HEADS, RET_DK, RET_DV), F32)],
        compiler_params=_params("parallel", "parallel", "arbitrary"),
        name="ret_scan",
    )(log_gamma, p, p, p, cos, sin)


def _ret_out_kernel(o_ref, gate_ref, x_ref, mod_ref, ng_ref, w_ref, out_ref):
    m = mod_ref[0, 0]
    o = o_ref[0, 0] + o_ref[1, 0]
    ng = ng_ref[...]
    parts = []
    for h in range(RET_HEADS):
        seg = o[:, h * RET_DV:(h + 1) * RET_DV]
        mu = jnp.mean(seg, axis=-1, keepdims=True)
        cen = seg - mu
        var = jnp.mean(cen * cen, axis=-1, keepdims=True)
        parts.append(cen * lax.rsqrt(var + EPS) * ng)
    y = _dot(_silu(gate_ref[0].astype(F32)) * jnp.concatenate(parts, axis=-1), w_ref[...])
    out_ref[0] = x_ref[0] + m[2:3] * y


def _ret_out(o, p, x, modv, norm_g, w_out_bf16):
    b, s, d = x.shape
    vw = RET_HEADS * RET_DV
    return pl.pallas_call(
        _ret_out_kernel,
        grid=(b, s // TM),
        in_specs=[pl.BlockSpec((2, 1, TM, vw), lambda bi, t: (0, bi, t, 0)),
                  pl.BlockSpec((1, TM, vw), lambda bi, t: (bi, t, 2)),
                  pl.BlockSpec((1, TM, d), lambda bi, t: (bi, t, 0)),
                  pl.BlockSpec((1, 1, 6, d), _mod_spec()),
                  pl.BlockSpec((1, RET_DV), lambda bi, t: (0, 0)),
                  pl.BlockSpec((vw, d), lambda bi, t: (0, 0))],
        out_specs=pl.BlockSpec((1, TM, d), lambda bi, t: (bi, t, 0)),
        out_shape=jax.ShapeDtypeStruct((b, s, d), F32),
        compiler_params=_params("parallel", "parallel"),
        name="ret_out",
    )(o, p, x, modv, norm_g.reshape(1, RET_DV), w_out_bf16)


def _rotary_tables(s):
    n_lat = s - TM
    pos = jnp.arange(n_lat, dtype=jnp.int32)
    rows = (pos // GRID_W).astype(F32)
    cols = (pos % GRID_W).astype(F32)
    quarter = RET_DK // 4
    inv_freq = ROPE_BASE ** (-jnp.arange(quarter, dtype=F32) / quarter)
    ang_r = rows[:, None] * inv_freq[None, :]
    ang_c = cols[:, None] * inv_freq[None, :]
    cos = jnp.concatenate([jnp.cos(ang_r), jnp.cos(ang_r), jnp.cos(ang_c), jnp.cos(ang_c)], axis=-1)
    sin = jnp.concatenate([-jnp.sin(ang_r), jnp.sin(ang_r), -jnp.sin(ang_c), jnp.sin(ang_c)], axis=-1)
    cos = jnp.concatenate([jnp.ones((TM, RET_DK), F32), cos], axis=0)
    sin = jnp.concatenate([jnp.zeros((TM, RET_DK), F32), sin], axis=0)
    return cos, sin


def _mixer_retention(x, modv, norm_g, w_in, decay_logit, out_norm_g, w_out):
    p = _in_proj(x, modv, norm_g, w_in.astype(BF16), w_in.shape[1] // 2)
    cos, sin = _rotary_tables(x.shape[1])
    o = _ret_scan(p, jax.nn.log_sigmoid(decay_logit.astype(F32)), cos, sin)
    return _ret_out(o, p, x, modv, out_norm_g, w_out.astype(BF16))


ROUTE_LANES = 128
ROUTE_COLS = 8
ISSUE_UNROLL = 8


def _route_kernel(x_ref, mod_ref, g_ref, w_ref, b_ref, rt_ref, cnt_out_ref, cnt_ref):
    first = (pl.program_id(0) == 0) & (pl.program_id(1) == 0)

    @pl.when(first)
    def _():
        cnt_ref[...] = jnp.zeros_like(cnt_ref)

    m = mod_ref[0, 0]
    h = _rms_mod(x_ref[0], g_ref[...], m[3:4], m[4:5])
    logits = _dot3(h, w_ref[...]) + b_ref[...]
    lane = lax.broadcasted_iota(jnp.int32, (TM, ROUTE_LANES), 1)
    big = jnp.int32(ROUTE_LANES)
    neg = -jnp.inf
    glog = jnp.where((lane >= N_EXPERTS) & (lane < N_EXPERTS + N_GROUPS), logits, neg)
    gmax = jnp.max(glog, axis=-1, keepdims=True)
    gsel = jnp.min(jnp.where(glog == gmax, lane, big), axis=-1, keepdims=True) - N_EXPERTS
    p_group = 1.0 / jnp.sum(jnp.exp(glog - gmax), axis=-1, keepdims=True)
    elog = jnp.where((lane >= gsel * EXPERTS_PER_GROUP) & (lane < (gsel + 1) * EXPERTS_PER_GROUP), logits, neg)
    m1 = jnp.max(elog, axis=-1, keepdims=True)
    i1 = jnp.min(jnp.where(elog == m1, lane, big), axis=-1, keepdims=True)
    elog2 = jnp.where(lane == i1, neg, elog)
    m2 = jnp.max(elog2, axis=-1, keepdims=True)
    i2 = jnp.min(jnp.where(elog2 == m2, lane, big), axis=-1, keepdims=True)
    e2 = jnp.exp(m2 - m1)
    w1 = p_group / (1.0 + e2)
    w2 = p_group * e2 / (1.0 + e2)
    onehot = jnp.where((lane == i1) | (lane == i2), 1.0, 0.0)
    r = lax.broadcasted_iota(jnp.int32, (TM, TM), 0)
    c = lax.broadcasted_iota(jnp.int32, (TM, TM), 1)
    before = _dot(jnp.where(c < r, 1.0, 0.0), onehot) + cnt_ref[...]
    r1 = jnp.sum(jnp.where(lane == i1, before, 0.0), axis=-1, keepdims=True)
    r2 = jnp.sum(jnp.where(lane == i2, before, 0.0), axis=-1, keepdims=True)
    cnt_ref[...] = cnt_ref[...] + jnp.sum(onehot, axis=0, keepdims=True)
    cnt_out_ref[...] = cnt_ref[...]
    vals = (i1.astype(F32), i2.astype(F32), w1, w2, r1, r2)
    out = jnp.zeros((TM, ROUTE_LANES), F32)
    for pos_, val in enumerate(vals):
        out = jnp.where(lane == pos_, val, out)
    rt_ref[0] = out[:, :ROUTE_COLS]


def _route(x, modv, norm_g, w_route, b_route):
    b, s, d = x.shape
    return pl.pallas_call(
        _route_kernel,
        grid=(b, s // TM),
        in_specs=[pl.BlockSpec((1, TM, d), lambda bi, t: (bi, t, 0)),
                  pl.BlockSpec((1, 1, 6, d), _mod_spec()),
                  pl.BlockSpec((1, d), lambda bi, t: (0, 0)),
                  pl.BlockSpec((d, ROUTE_LANES), lambda bi, t: (0, 0)),
                  pl.BlockSpec((1, ROUTE_LANES), lambda bi, t: (0, 0))],
        out_specs=[pl.BlockSpec((1, TM, ROUTE_COLS), lambda bi, t: (bi, t, 0)),
                   pl.BlockSpec((1, ROUTE_LANES), lambda bi, t: (0, 0))],
        out_shape=[jax.ShapeDtypeStruct((b, s, ROUTE_COLS), F32), jax.ShapeDtypeStruct((1, ROUTE_LANES), F32)],
        scratch_shapes=[pltpu.VMEM((1, ROUTE_LANES), F32)],
        compiler_params=_params("arbitrary", "arbitrary"),
        name="moe_route",
    )(x, modv, norm_g.reshape(1, d), w_route, b_route)


def _issue_row_copies(make_copy):
    def body(i, carry):
        for k in range(2):
            make_copy(i, k).start()
        return carry

    lax.fori_loop(0, TM, body, 0, unroll=ISSUE_UNROLL)


def _tile_step():
    step = pl.program_id(0) * pl.num_programs(1) + pl.program_id(1)
    return step, pl.num_programs(0) * pl.num_programs(1), jnp.bitwise_and(step, 1)


def _pack_bf16_pairs(h):
    bits = pltpu.bitcast(h.astype(BF16).astype(F32), jnp.uint32)
    half = h.shape[1] // 2
    return jnp.bitwise_or(jnp.right_shift(bits[:, :half], jnp.uint32(16)),
                          jnp.bitwise_and(bits[:, half:], jnp.uint32(0xFFFF0000)))


def _unpack_bf16_pairs(w):
    lo = pltpu.bitcast(jnp.left_shift(w, jnp.uint32(16)), F32)
    hi = pltpu.bitcast(jnp.bitwise_and(w, jnp.uint32(0xFFFF0000)), F32)
    return jnp.concatenate([lo.astype(BF16), hi.astype(BF16)], axis=1)


def _dispatch_kernel(dest_ref, pe_ref, x_ref, mod_ref, g_ref, xb_ref, h_ref, z_ref, sem, zsem):
    step, n_steps, slot = _tile_step()

    @pl.when(step == 0)
    def _():
        z_ref[...] = jnp.zeros_like(z_ref)

        def clear(e):
            start = pl.multiple_of(pe_ref[e] - MOE_BLOCK, MOE_BLOCK)
            return pltpu.make_async_copy(z_ref, xb_ref.at[pl.ds(start, MOE_BLOCK)], zsem.at[e])

        def used(e):
            return pe_ref[e] > (pe_ref[e - 1] if e else 0)

        for e in range(N_EXPERTS):
            @pl.when(used(e))
            def _():
                clear(e).start()
        for e in range(N_EXPERTS):
            @pl.when(used(e))
            def _():
                clear(e).wait()

        def clear_unused(blk, carry):
            cp = pltpu.make_async_copy(z_ref, xb_ref.at[pl.ds(pl.multiple_of(blk * MOE_BLOCK, MOE_BLOCK), MOE_BLOCK)],
                                       zsem.at[0])
            cp.start()
            cp.wait()
            return carry

        lax.fori_loop(_block_of(pe_ref[N_EXPERTS - 1], MOE_BLOCK), xb_ref.shape[0] // MOE_BLOCK, clear_unused, 0)

    m = mod_ref[0, 0]
    h_ref[slot] = _pack_bf16_pairs(_rms_mod(x_ref[0], g_ref[...], m[3:4], m[4:5]))
    _issue_row_copies(lambda i, k: pltpu.make_async_copy(
        h_ref.at[slot, pl.ds(i, 1)], xb_ref.at[pl.ds(dest_ref[0, 0, 2 * i + k], 1)], sem.at[slot]))

    def wait_tile(sl):
        for _ in range(2):
            pltpu.make_async_copy(h_ref.at[sl], xb_ref.at[pl.ds(0, TM)], sem.at[sl]).wait()

    @pl.when(step > 0)
    def _():
        wait_tile(1 - slot)

    @pl.when(step == n_steps - 1)
    def _():
        wait_tile(slot)


def _dispatch(x, modv, norm_g, dest, pad_end, n_rows):
    b, s, d = x.shape
    nt = s // TM
    return pl.pallas_call(
        _dispatch_kernel,
        grid=(b, nt),
        in_specs=[pl.BlockSpec((1, 1, 2 * TM), lambda bi, t: (bi * nt + t, 0, 0), memory_space=pltpu.SMEM),
                  pl.BlockSpec(memory_space=pltpu.SMEM),
                  pl.BlockSpec((1, TM, d), lambda bi, t: (bi, t, 0)),
                  pl.BlockSpec((1, 1, 6, d), _mod_spec()),
                  pl.BlockSpec((1, d), lambda bi, t: (0, 0))],
        out_specs=pl.BlockSpec(memory_space=pl.ANY),
        out_shape=jax.ShapeDtypeStruct((n_rows, d // 2), jnp.uint32),
        scratch_shapes=[pltpu.VMEM((2, TM, d // 2), jnp.uint32), pltpu.VMEM((MOE_BLOCK, d // 2), jnp.uint32),
                        pltpu.SemaphoreType.DMA((2,)), pltpu.SemaphoreType.DMA((N_EXPERTS,))],
        compiler_params=_params("arbitrary", "arbitrary"),
        name="moe_dispatch",
    )(dest.reshape(b * nt, 1, 2 * TM), pad_end, x, modv, norm_g.reshape(1, d))


def _expert_kernel(be_ref, nu_ref, x_ref, wg_ref, wu_ref, wd_ref, o_ref, wg_bf, wu_bf, wd_bf):
    i = pl.program_id(0)

    @pl.when(i < nu_ref[0])
    def _():
        @pl.when((i == 0) | (be_ref[i] != be_ref[jnp.maximum(i - 1, 0)]))
        def _():
            wg_bf[...] = wg_ref[0, 0].astype(BF16)
            wu_bf[...] = wu_ref[0, 0].astype(BF16)
            wd_bf[...] = wd_ref[0, 0].astype(BF16)

        xb = _unpack_bf16_pairs(x_ref[...])
        act = _silu(jnp.dot(xb, wg_bf[...], preferred_element_type=F32)) * jnp.dot(xb, wu_bf[...], preferred_element_type=F32)
        o_ref[...] = jnp.dot(act.astype(BF16), wd_bf[...], preferred_element_type=F32)

    @pl.when(i >= nu_ref[0])
    def _():
        o_ref[...] = jnp.zeros_like(o_ref)


def _experts(xb, block_e, n_used, layer, w_gate, w_up, w_down):
    n_rows = xb.shape[0]
    d, de = w_gate.shape[-2:]
    nb = n_rows // MOE_BLOCK

    def last_used(i, nu):
        return jnp.minimum(i, nu[0] - 1)

    grid_spec = pltpu.PrefetchScalarGridSpec(
        num_scalar_prefetch=2,
        grid=(nb,),
        in_specs=[pl.BlockSpec((MOE_BLOCK, d // 2), lambda i, be, nu: (last_used(i, nu), 0)),
                  pl.BlockSpec((1, 1, d, de), lambda i, be, nu: (layer, be[last_used(i, nu)], 0, 0)),
                  pl.BlockSpec((1, 1, d, de), lambda i, be, nu: (layer, be[last_used(i, nu)], 0, 0)),
                  pl.BlockSpec((1, 1, de, d), lambda i, be, nu: (layer, be[last_used(i, nu)], 0, 0))],
        out_specs=pl.BlockSpec((MOE_BLOCK, d), lambda i, be, nu: (i, 0)),
        scratch_shapes=[pltpu.VMEM((d, de), BF16), pltpu.VMEM((d, de), BF16), pltpu.VMEM((de, d), BF16)],
    )
    return pl.pallas_call(
        _expert_kernel,
        grid_spec=grid_spec,
        out_shape=jax.ShapeDtypeStruct((n_rows, d), F32),
        compiler_params=_params("arbitrary"),
        name="moe_experts",
    )(block_e, n_used, xb, w_gate, w_up, w_down)


def _combine_kernel(dest_ref, next_dest_ref, x_ref, mod_ref, rt_ref, fg_ref, yb_ref, o_ref, y_ref, sem, *, final_norm):
    step, n_steps, slot = _tile_step()

    def gather(d_ref, sl):
        _issue_row_copies(lambda i, k: pltpu.make_async_copy(
            yb_ref.at[pl.ds(d_ref[0, 0, 2 * i + k], 1)], y_ref.at[sl, pl.ds(k * TM + i, 1)], sem.at[sl]))

    @pl.when(step == 0)
    def _():
        gather(dest_ref, 0)

    @pl.when(step + 1 < n_steps)
    def _():
        gather(next_dest_ref, 1 - slot)

    pltpu.make_async_copy(yb_ref.at[pl.ds(0, 2 * TM)], y_ref.at[slot], sem.at[slot]).wait()
    m = mod_ref[0, 0]
    rt = rt_ref[0]
    out = x_ref[0] + m[5:6] * (rt[:, 2:3] * y_ref[slot, 0:TM] + rt[:, 3:4] * y_ref[slot, TM:2 * TM])
    if final_norm:
        out = out * lax.rsqrt(jnp.mean(out * out, axis=-1, keepdims=True) + EPS) * fg_ref[...]
    o_ref[0] = out


def _combine(x, modv, rt, dest, yb, final_g, final_norm):
    b, s, d = x.shape
    nt = s // TM
    skip = 1 if final_norm else 0

    def next_tile(bi, t):
        wrap = t + 1 >= nt - skip
        nb_ = jnp.minimum(jnp.where(wrap, bi + 1, bi), b - 1)
        return (nb_ * nt + jnp.where(wrap, 0, t + 1) + skip, 0, 0)

    return pl.pallas_call(
        functools.partial(_combine_kernel, final_norm=final_norm),
        grid=(b, nt - skip),
        in_specs=[pl.BlockSpec((1, 1, 2 * TM), lambda bi, t: (bi * nt + t + skip, 0, 0), memory_space=pltpu.SMEM),
                  pl.BlockSpec((1, 1, 2 * TM), next_tile, memory_space=pltpu.SMEM),
                  pl.BlockSpec((1, TM, d), lambda bi, t: (bi, t + skip, 0)),
                  pl.BlockSpec((1, 1, 6, d), lambda bi, t: (bi, jnp.minimum(t + skip, 1), 0, 0)),
                  pl.BlockSpec((1, TM, ROUTE_COLS), lambda bi, t: (bi, t + skip, 0)),
                  pl.BlockSpec((1, d), lambda bi, t: (0, 0)),
                  pl.BlockSpec(memory_space=pl.ANY)],
        out_specs=pl.BlockSpec((1, TM, d), lambda bi, t: (bi, t, 0)),
        out_shape=jax.ShapeDtypeStruct((b, s - skip * TM, d), F32),
        scratch_shapes=[pltpu.VMEM((2, 2 * TM, d), F32), pltpu.SemaphoreType.DMA((2,))],
        compiler_params=_params("arbitrary", "arbitrary"),
        name="moe_combine",
    )(dest.reshape(b * nt, 1, 2 * TM), dest.reshape(b * nt, 1, 2 * TM), x, modv, rt, final_g.reshape(1, d), yb)


def _hier_moe(x, modv, norm_g, layer, wg_r, bg_r, we_r, be_r, w_gate, w_up, w_down, final_g, final_norm):
    b, s, d = x.shape
    n_tok = b * s
    pad = ROUTE_LANES - N_EXPERTS - N_GROUPS
    w_route = jnp.concatenate([we_r, wg_r, jnp.zeros((d, pad), F32)], axis=1)
    b_route = jnp.concatenate([be_r, bg_r, jnp.zeros((pad,), F32)]).reshape(1, ROUTE_LANES)
    rt, cnt = _route(x, modv, norm_g, w_route, b_route)
    counts = cnt[0, :N_EXPERTS].astype(jnp.int32)
    padded = (counts + MOE_BLOCK - 1) // MOE_BLOCK * MOE_BLOCK
    pad_end = jnp.cumsum(padded)
    pad_start = pad_end - padded
    n_blocks = -(-(2 * n_tok + N_EXPERTS * (MOE_BLOCK - 1)) // MOE_BLOCK)
    rt2 = rt.reshape(n_tok, ROUTE_COLS)
    expert = rt2[:, 0:2].astype(jnp.int32)
    experts = jnp.arange(N_EXPERTS, dtype=jnp.int32)
    start_of = jnp.sum(jnp.where(expert[..., None] == experts, pad_start, 0), axis=-1)
    dest = (start_of + rt2[:, 4:6].astype(jnp.int32)).reshape(-1)
    block_start = jnp.arange(n_blocks, dtype=jnp.int32) * MOE_BLOCK
    block_e = jnp.minimum(jnp.sum((pad_end[None, :] <= block_start[:, None]).astype(jnp.int32), axis=1), N_EXPERTS - 1)
    n_used = (pad_end[-1:] // MOE_BLOCK).astype(jnp.int32)
    xb = _dispatch(x, modv, norm_g, dest, pad_end.astype(jnp.int32), n_blocks * MOE_BLOCK)
    yb = _experts(xb, block_e, n_used, layer, w_gate, w_up, w_down)
    return _combine(x, modv, rt, dest, yb, final_g, final_norm)


def kernel(x, c, ctx, c_ctx, mod_w, mod_b, norm1_g, norm2_g, final_g, gdn_w_in, gdn_conv_w, gdn_a_log, gdn_dt_bias, gdn_norm_g, gdn_w_out, pool_w, pool_b, pool_scale, ret_w_in, ret_decay_logit, ret_norm_g, ret_w_out, router_group_w, router_group_b, router_expert_w, router_expert_b, exp_w_gate, exp_w_up, exp_w_down):
    b, n_lat, d = x.shape
    depth = mod_w.shape[0]
    assert ctx.shape[1] == TM and n_lat % TM == 0 and b < 16
    xs = jnp.concatenate([ctx, x], axis=1)
    cvec = jnp.concatenate([c, c_ctx[None], jnp.zeros((15 - b, d), F32)], axis=0)
    mods = _modulation(cvec, mod_w, mod_b)
    for i in range(depth):
        j, kind = i // 3, i % 3
        lat = mods[i, :b].reshape(b, 1, 6, d)
        con = jnp.broadcast_to(mods[i, b].reshape(1, 1, 6, d), (b, 1, 6, d))
        modv = jnp.concatenate([con, lat], axis=1)
        if kind == 0:
            xs = _mixer_gdn(xs, modv, norm1_g[i], gdn_w_in[j], gdn_conv_w[j], gdn_a_log[j], gdn_dt_bias[j],
                            gdn_norm_g[j], gdn_w_out[j])
        elif kind == 1:
            xs = _mixer_pool(xs, modv, norm1_g[i], pool_w[j], pool_b[j], pool_scale[j])
        else:
            xs = _mixer_retention(xs, modv, norm1_g[i], ret_w_in[j], ret_decay_logit[j], ret_norm_g[j], ret_w_out[j])
        xs = _hier_moe(xs, modv, norm2_g[i], i, router_group_w[i], router_group_b[i], router_expert_w[i],
                       router_expert_b[i], exp_w_gate, exp_w_up, exp_w_down, final_g, i == depth - 1)
    return xs
```

```python
import functools
import math

import jax
import jax.numpy as jnp
from jax import lax
from jax.experimental import pallas as pl
from jax.experimental.pallas import tpu as pltpu

F32 = jnp.float32
BF16 = jnp.bfloat16
HIGHEST = lax.Precision.HIGHEST

EPS = 1e-6
TM = 256
HALO = 8
HALO_BF16 = 16
CHUNK = 64
GDN_HEADS = 8
GDN_DK = 128
GDN_CONV = 5
RET_HEADS = 4
RET_DK = 256
RET_DV = 512
ROPE_BASE = 10000.0
GRID_W = 64
POOL_WINDOWS = (2, 4, 8, 16)
POOL_GROUP = 256
N_GROUPS = 4
EXPERTS_PER_GROUP = 8
N_EXPERTS = 32
MOE_BLOCK = 256
V7X_VMEM_LIMIT_BYTES = 56 * 1024 * 1024


def _params(*sem):
    return pltpu.CompilerParams(dimension_semantics=sem, vmem_limit_bytes=V7X_VMEM_LIMIT_BYTES)


def _dot(a, b):
    return jnp.dot(a.astype(BF16), b.astype(BF16), preferred_element_type=F32)


def _dot_hi(a, b):
    return jnp.dot(a, b, precision=HIGHEST, preferred_element_type=F32)


def _dot_nt(a, b):
    return lax.dot_general(a.astype(BF16), b.astype(BF16), (((1,), (1,)), ((), ())), preferred_element_type=F32)


def _split(x):
    hi = x.astype(BF16)
    return hi, (x - hi.astype(F32)).astype(BF16)


def _dot3_parts(ah, al, bh, bl, dims):
    def d(p, q):
        return lax.dot_general(p, q, dims, preferred_element_type=F32)
    return d(ah, bh) + (d(ah, bl) + d(al, bh))


def _dot3(a, b):
    return _dot3_parts(*_split(a), *_split(b), (((1,), (0,)), ((), ())))


def _cumsum_dot(ones_mask, x):
    x1 = x.astype(BF16)
    r1 = x - x1.astype(F32)
    x2 = r1.astype(BF16)
    x3 = (r1 - x2.astype(F32)).astype(BF16)
    m = ones_mask.astype(BF16)
    return (jnp.dot(m, x1, preferred_element_type=F32) + jnp.dot(m, x2, preferred_element_type=F32)
            + jnp.dot(m, x3, preferred_element_type=F32))


def _dot_tn(a, b):
    return lax.dot_general(a.astype(BF16), b.astype(BF16), (((0,), (0,)), ((), ())), preferred_element_type=F32)


def _block_of(i, size):
    return jnp.right_shift(i, int(math.log2(size)))


def _silu(x):
    return x * jax.nn.sigmoid(x)


def _softplus(x):
    return jnp.maximum(x, 0.0) + jnp.log(1.0 + jnp.exp(-jnp.abs(x)))


def _rms_mod(x, g, shift, scale):
    y = x * lax.rsqrt(jnp.mean(x * x, axis=-1, keepdims=True) + EPS) * g
    return y * (1.0 + scale) + shift


def _mod_spec(grid_rank_prefix=0):
    def idx(*g):
        b, t = g[grid_rank_prefix], g[grid_rank_prefix + 1]
        return (b, jnp.minimum(t, 1), 0, 0)
    return idx


def _mod_kernel(c_ref, w_ref, b_ref, o_ref):
    o_ref[0] = _dot_hi(_silu(c_ref[...]), w_ref[0]) + b_ref[0]


def _modulation(cvec, mod_w, mod_b):
    n_layers, d, d6 = mod_w.shape
    return pl.pallas_call(
        _mod_kernel,
        grid=(n_layers, d6 // d),
        in_specs=[pl.BlockSpec((16, d), lambda l, j: (0, 0)),
                  pl.BlockSpec((1, d, d), lambda l, j: (l, 0, j)),
                  pl.BlockSpec((1, 1, d), lambda l, j: (l, 0, j))],
        out_specs=pl.BlockSpec((1, 16, d), lambda l, j: (l, 0, j)),
        out_shape=jax.ShapeDtypeStruct((n_layers, 16, d6), F32),
        compiler_params=_params("parallel", "parallel"),
        name="modulation",
    )(cvec, mod_w, mod_b.reshape(n_layers, 1, d6))


def _in_kernel(x_ref, mod_ref, g_ref, w_ref, o_ref):
    m = mod_ref[0, 0]
    h = _rms_mod(x_ref[0], g_ref[...], m[0:1], m[1:2])
    o_ref[0] = _dot(h, w_ref[...]).astype(o_ref.dtype)


def _in_proj(x, modv, g, w_bf16, tn):
    b, s, d = x.shape
    n = w_bf16.shape[1]
    return pl.pallas_call(
        _in_kernel,
        grid=(n // tn, b, s // TM),
        in_specs=[pl.BlockSpec((1, TM, d), lambda j, bi, t: (bi, t, 0)),
                  pl.BlockSpec((1, 1, 6, d), _mod_spec(1)),
                  pl.BlockSpec((1, d), lambda j, bi, t: (0, 0)),
                  pl.BlockSpec((d, tn), lambda j, bi, t: (0, j))],
        out_specs=pl.BlockSpec((1, TM, tn), lambda j, bi, t: (bi, t, j)),
        out_shape=jax.ShapeDtypeStruct((b, s, n), BF16),
        compiler_params=_params("parallel", "parallel", "parallel"),
        name="in_proj",
    )(x, modv, g.reshape(1, d), w_bf16)


def _gates_kernel(x_ref, mod_ref, g_ref, wab_ref, alog_ref, dtb_ref, gc_ref, bt_ref):
    m = mod_ref[0, 0]
    h = _rms_mod(x_ref[0], g_ref[...], m[0:1], m[1:2])
    ab = _dot3(h, wab_ref[...])
    nh = GDN_HEADS
    gate = -jnp.exp(alog_ref[...]) * _softplus(ab[:, :2 * nh] + dtb_ref[...])
    beta = jax.nn.sigmoid(ab[:, 2 * nh:])
    r = lax.broadcasted_iota(jnp.int32, (TM, TM), 0)
    c = lax.broadcasted_iota(jnp.int32, (TM, TM), 1)
    same = _block_of(r, CHUNK) == _block_of(c, CHUNK)
    cum_f = jnp.where(same & (c <= r), 1.0, 0.0)
    cum_b = jnp.where(same & (c >= r), 1.0, 0.0)
    gc_ref[0, 0] = _cumsum_dot(cum_f, gate[:, :nh])
    gc_ref[1, 0] = _cumsum_dot(cum_b, gate[:, nh:])
    bt_ref[0, 0] = beta[:, :nh]
    bt_ref[1, 0] = beta[:, nh:]


def _gdn_gates(x, modv, g, w_ab, a_log, dt_bias):
    b, s, d = x.shape
    nh = GDN_HEADS
    out = jax.ShapeDtypeStruct((2, b, s, nh), F32)
    return pl.pallas_call(
        _gates_kernel,
        grid=(b, s // TM),
        in_specs=[pl.BlockSpec((1, TM, d), lambda bi, t: (bi, t, 0)),
                  pl.BlockSpec((1, 1, 6, d), _mod_spec()),
                  pl.BlockSpec((1, d), lambda bi, t: (0, 0)),
                  pl.BlockSpec((d, 4 * nh), lambda bi, t: (0, 0)),
                  pl.BlockSpec((1, 2 * nh), lambda bi, t: (0, 0)),
                  pl.BlockSpec((1, 2 * nh), lambda bi, t: (0, 0))],
        out_specs=[pl.BlockSpec((2, 1, TM, nh), lambda bi, t: (0, bi, t, 0)),
                   pl.BlockSpec((2, 1, TM, nh), lambda bi, t: (0, bi, t, 0))],
        out_shape=[out, out],
        compiler_params=_params("parallel", "parallel"),
        name="gdn_gates",
    )(x, modv, g.reshape(1, d), w_ab, a_log.reshape(1, 2 * nh), dt_bias.reshape(1, 2 * nh))


def _halo_specs(width, col_of, n_tiles, halo=HALO):
    per = TM // halo

    def prev(bi, t, *rest):
        return (bi, jnp.maximum(t * per - 1, 0), col_of(*rest))

    def nxt(bi, t, *rest):
        return (bi, jnp.minimum((t + 1) * per, n_tiles * per - 1), col_of(*rest))

    return pl.BlockSpec((1, halo, width), prev), pl.BlockSpec((1, halo, width), nxt)


def _halo_valid():
    t = pl.program_id(1)
    nt = pl.num_programs(1)
    return t >= 2, (t >= 1) & (t < nt - 1)


def _conv_kernel(cur_ref, prev_ref, next_ref, w_ref, o_ref, ext_ref):
    j = pl.program_id(2)
    prev_ok, next_ok = _halo_valid()
    halo = HALO_BF16
    ext_ref[0:halo] = jnp.where(prev_ok, prev_ref[0].astype(F32), 0.0)
    ext_ref[halo:halo + TM] = cur_ref[0].astype(F32)
    ext_ref[halo + TM:] = jnp.where(next_ok, next_ref[0].astype(F32), 0.0)
    w = w_ref[...]
    base = halo - GDN_CONV // 2
    acc = w[0:1] * ext_ref[base:base + TM]
    for k in range(1, GDN_CONV):
        acc = acc + w[k:k + 1] * ext_ref[base + k:base + k + TM]
    y = _silu(acc)
    width = y.shape[1]
    is_v = j >= 2 * (GDN_HEADS * GDN_DK // width)
    is_q = j < (GDN_HEADS * GDN_DK // width)
    qscale = jnp.where(is_q, GDN_DK ** -0.5, 1.0)
    for hh in range(width // GDN_DK):
        seg = y[:, hh * GDN_DK:(hh + 1) * GDN_DK]
        nrm = seg * lax.rsqrt(jnp.sum(seg * seg, axis=-1, keepdims=True) + EPS) * qscale
        o_ref[0, :, hh * GDN_DK:(hh + 1) * GDN_DK] = jnp.where(is_v, seg, nrm)


def _gdn_conv(p, conv_w):
    b, s, _ = p.shape
    n = conv_w.shape[1]
    width = GDN_HEADS * GDN_DK
    prev_spec, next_spec = _halo_specs(width, lambda j: j, s // TM, HALO_BF16)
    return pl.pallas_call(
        _conv_kernel,
        grid=(b, s // TM, n // width),
        in_specs=[pl.BlockSpec((1, TM, width), lambda bi, t, j: (bi, t, j)),
                  prev_spec, next_spec,
                  pl.BlockSpec((GDN_CONV, width), lambda bi, t, j: (0, j))],
        out_specs=pl.BlockSpec((1, TM, width), lambda bi, t, j: (bi, t, j)),
        out_shape=jax.ShapeDtypeStruct((b, s, n), F32),
        scratch_shapes=[pltpu.VMEM((TM + 2 * HALO_BF16, width), F32)],
        compiler_params=_params("parallel", "parallel", "parallel"),
        name="gdn_conv",
    )(p, p, p, conv_w)


GDN_PAIR = 2 * GDN_DK
GDN_PAIRS = GDN_HEADS // 2
INV_LANES = 128
SCAN_BATCH = 4


def _pair_cols(cols, hp, width):
    lane = lax.broadcasted_iota(jnp.int32, (cols.shape[0], width), 1)
    return jnp.where(lane < width // 2, cols[:, 2 * hp:2 * hp + 1], cols[:, 2 * hp + 1:2 * hp + 2])


def _pair_blockdiag_rows(x, lane_block):
    n, w = x.shape
    r = lax.broadcasted_iota(jnp.int32, (2 * n, w), 0)
    c = lax.broadcasted_iota(jnp.int32, (2 * n, w), 1)
    same = _block_of(r, n) == jnp.bitwise_and(_block_of(c, lane_block), 1)
    return jnp.where(same, jnp.concatenate([x, x], axis=0), 0.0)


def _pair_decay(gc_cols, gct_row, hp, fwd, inclusive):
    r = lax.broadcasted_iota(jnp.int32, (CHUNK, 2 * CHUNK), 0)
    c = jnp.bitwise_and(lax.broadcasted_iota(jnp.int32, (CHUNK, 2 * CHUNK), 1), CHUNK - 1)
    ahead = (r - c) if fwd else (c - r)
    keep = (ahead >= 0) if inclusive else (ahead > 0)
    rel = _pair_cols(gc_cols, hp, 2 * CHUNK) - gct_row
    return jnp.where(keep, jnp.exp(jnp.where(keep, rel, 0.0)), 0.0)


def _gdn_a_kernel(k_ref, gc_ref, gctp_ref, bt_ref, a_ref):
    for ch in range(TM // CHUNK):
        rows = pl.ds(ch * CHUNK, CHUNK)
        for hp in range(GDN_PAIRS):
            kh, kl = _split(k_ref[0, rows, hp * GDN_PAIR:(hp + 1) * GDN_PAIR])
            kk = _dot3_parts(kh, kl, _pair_blockdiag_rows(kh, GDN_DK), _pair_blockdiag_rows(kl, GDN_DK),
                             (((1,), (1,)), ((), ())))
            for d in range(2):
                decay = _pair_decay(gc_ref[d, 0, rows, :], gctp_ref[d, 0, ch, hp:hp + 1, :], hp, d == 0, False)
                a_ref[d, 0, ch, hp] = _pair_cols(bt_ref[d, 0, rows, :], hp, 2 * CHUNK) * kk * decay


def _gdn_a(qkv, gc, gctp, bt):
    b, s, _ = qkv.shape
    nh = GDN_HEADS
    cpt = TM // CHUNK
    gate_spec = pl.BlockSpec((2, 1, TM, nh), lambda bi, t: (0, bi, t, 0))
    return pl.pallas_call(
        _gdn_a_kernel,
        grid=(b, s // TM),
        in_specs=[pl.BlockSpec((1, TM, nh * GDN_DK), lambda bi, t: (bi, t, 1)),
                  gate_spec,
                  pl.BlockSpec((2, 1, cpt, GDN_PAIRS, 2 * CHUNK), lambda bi, t: (0, bi, t, 0, 0)),
                  gate_spec],
        out_specs=pl.BlockSpec((2, 1, cpt, GDN_PAIRS, CHUNK, 2 * CHUNK), lambda bi, t: (0, bi, t, 0, 0, 0)),
        out_shape=jax.ShapeDtypeStruct((2, b, s // CHUNK, GDN_PAIRS, CHUNK, 2 * CHUNK), F32),
        compiler_params=_params("parallel", "parallel"),
        name="gdn_a",
    )(qkv, gc, gctp, bt)


def _substitute_rows(at_ref, x_ref, hh):
    n = CHUNK
    zero = jnp.zeros((8, INV_LANES), F32)
    for i in range(n):
        nb = (i + 7) // 8
        acc = [-at_ref[hh, pl.ds(i * n + jb * 8, 8), :] for jb in range(nb)]
        for m in range(1, i):
            a_im = jnp.broadcast_to(at_ref[hh, pl.ds(i * n + m, 1), :], (8, INV_LANES))
            for jb in range((m + 7) // 8):
                acc[jb] = acc[jb] - a_im * x_ref[hh, pl.ds(m * n + jb * 8, 8), :]
        for jb in range(n // 8):
            x_ref[hh, pl.ds(i * n + jb * 8, 8), :] = acc[jb] if jb < nb else zero


def _gdn_inv_kernel(a_ref, t_ref, at_ref, x_ref):
    fwd = pl.program_id(0) == 0
    n = CHUNK

    def load(r, transposed):
        slab = a_ref[0, pl.ds(r, INV_LANES, stride=n), :].T
        for hh in range(2):
            dst = pl.ds(r, n, stride=n) if transposed else pl.ds(r * n, n)
            at_ref[hh, dst, :] = slab[hh * n:(hh + 1) * n]

    def store(r, transposed):
        src = pl.ds(r, n, stride=n) if transposed else pl.ds(r * n, n)
        eye = jnp.where(lax.broadcasted_iota(jnp.int32, (n, INV_LANES), 0) == r, 1.0, 0.0)
        slab = jnp.concatenate([x_ref[0, src, :] + eye, x_ref[1, src, :] + eye], axis=0)
        t_ref[0, pl.ds(r, INV_LANES, stride=n), :] = slab.T

    def rows(fn, transposed):
        for r in range(n):
            fn(r, transposed)

    @pl.when(fwd)
    def _():
        rows(load, False)

    @pl.when(jnp.logical_not(fwd))
    def _():
        rows(load, True)

    def halves(hh, carry):
        _substitute_rows(at_ref, x_ref, hh)
        return carry

    lax.fori_loop(0, 2, halves, 0)

    @pl.when(fwd)
    def _():
        rows(store, False)

    @pl.when(jnp.logical_not(fwd))
    def _():
        rows(store, True)


def _gdn_inv(a):
    shape = a.shape
    n_sys = shape[1] * shape[2] * shape[3]
    assert n_sys % INV_LANES == 0
    rows_per_step = INV_LANES * CHUNK
    t = pl.pallas_call(
        _gdn_inv_kernel,
        grid=(2, n_sys // INV_LANES),
        in_specs=[pl.BlockSpec((1, rows_per_step, 2 * CHUNK), lambda d, g: (d, g, 0))],
        out_specs=pl.BlockSpec((1, rows_per_step, 2 * CHUNK), lambda d, g: (d, g, 0)),
        out_shape=jax.ShapeDtypeStruct((2, n_sys * CHUNK, 2 * CHUNK), F32),
        scratch_shapes=[pltpu.VMEM((2, CHUNK * CHUNK, INV_LANES), F32), pltpu.VMEM((2, CHUNK * CHUNK, INV_LANES), F32)],
        compiler_params=_params("parallel", "parallel"),
        name="gdn_inv",
    )(a.reshape(2, n_sys * CHUNK, 2 * CHUNK))
    return t.reshape(shape)


def _gdn_uw_kernel(t_ref, q_ref, k_ref, v_ref, gc_ref, gctp_ref, bt_ref, u_ref, w_ref, qk_ref):
    for ch in range(TM // CHUNK):
        rows = pl.ds(ch * CHUNK, CHUNK)
        for hp in range(GDN_PAIRS):
            cols = slice(hp * GDN_PAIR, (hp + 1) * GDN_PAIR)
            kp = k_ref[0, rows, cols]
            vp = v_ref[0, rows, cols]
            qk = _dot_nt(q_ref[0, rows, cols], _pair_blockdiag_rows(kp, GDN_DK))
            for d in range(2):
                gc = gc_ref[d, 0, rows, :]
                beta = _pair_cols(bt_ref[d, 0, rows, :], hp, GDN_PAIR)
                rhs = jnp.concatenate([vp * beta, kp * (beta * jnp.exp(_pair_cols(gc, hp, GDN_PAIR)))], axis=1)
                uw = _dot(t_ref[d, 0, ch, hp], _pair_blockdiag_rows(rhs, GDN_DK))
                u_ref[d, 0, rows, cols] = uw[:, :GDN_PAIR]
                w_ref[d, 0, rows, cols] = uw[:, GDN_PAIR:].astype(BF16)
                decay = _pair_decay(gc, gctp_ref[d, 0, ch, hp:hp + 1, :], hp, d == 0, True)
                qk_ref[d, 0, ch, hp] = (qk * decay).astype(BF16)


def _gdn_uw(t, qkv, gc, gctp, bt):
    b, s, _ = qkv.shape
    nh = GDN_HEADS
    width = nh * GDN_DK
    cpt = TM // CHUNK
    gate_spec = pl.BlockSpec((2, 1, TM, nh), lambda bi, t_: (0, bi, t_, 0))
    sys_spec = pl.BlockSpec((2, 1, cpt, GDN_PAIRS, CHUNK, 2 * CHUNK), lambda bi, t_: (0, bi, t_, 0, 0, 0))
    tok_spec = pl.BlockSpec((2, 1, TM, width), lambda bi, t_: (0, bi, t_, 0))
    return pl.pallas_call(
        _gdn_uw_kernel,
        grid=(b, s // TM),
        in_specs=[sys_spec,
                  pl.BlockSpec((1, TM, width), lambda bi, t_: (bi, t_, 0)),
                  pl.BlockSpec((1, TM, width), lambda bi, t_: (bi, t_, 1)),
                  pl.BlockSpec((1, TM, width), lambda bi, t_: (bi, t_, 2)),
                  gate_spec,
                  pl.BlockSpec((2, 1, cpt, GDN_PAIRS, 2 * CHUNK), lambda bi, t_: (0, bi, t_, 0, 0)),
                  gate_spec],
        out_specs=[tok_spec, tok_spec, sys_spec],
        out_shape=[jax.ShapeDtypeStruct((2, b, s, width), F32), jax.ShapeDtypeStruct((2, b, s, width), BF16),
                   jax.ShapeDtypeStruct((2, b, s // CHUNK, GDN_PAIRS, CHUNK, 2 * CHUNK), BF16)],
        compiler_params=_params("parallel", "parallel"),
        name="gdn_uw",
    )(t, qkv, qkv, qkv, gc, gctp, bt)


def _gdn_scan_kernel(qf_ref, kf_ref, uf_ref, wf_ref, qkf_ref, gcf_ref, qb_ref, kb_ref, ub_ref, wb_ref, qkb_ref, gcb_ref,
                     of_ref, ob_ref, s_ref):
    @pl.when(pl.program_id(1) == 0)
    def _():
        s_ref[...] = jnp.zeros_like(s_ref)

    dk = GDN_DK
    zeros = jnp.zeros((dk, dk), BF16)
    directions = ((qf_ref, kf_ref, uf_ref, wf_ref, qkf_ref, gcf_ref, of_ref),
                  (qb_ref, kb_ref, ub_ref, wb_ref, qkb_ref, gcb_ref, ob_ref))
    chains = [(bb, d, hp) for bb in range(SCAN_BATCH) for d in range(2) for hp in range(GDN_PAIRS)]
    stage1 = []
    for bb, d, hp in chains:
        q_ref, k_ref, u_ref, w_ref, qk_ref, gc_ref, o_ref = directions[d]
        cols = slice(hp * GDN_PAIR, (hp + 1) * GDN_PAIR)
        gc_all = gc_ref[0, bb]
        glast = gc_all[CHUNK - 1:CHUNK] if d == 0 else gc_all[0:1]
        gcp = _pair_cols(gc_all, hp, GDN_PAIR)
        q_in = q_ref[bb, :, cols] * jnp.exp(gcp)
        k_out = (k_ref[bb, :, cols] * jnp.exp(_pair_cols(glast, hp, GDN_PAIR) - gcp)).astype(BF16)
        sa = s_ref[bb, d, 2 * hp]
        sb = s_ref[bb, d, 2 * hp + 1]
        s_bd = jnp.concatenate([jnp.concatenate([sa.astype(BF16), zeros], axis=1),
                                jnp.concatenate([zeros, sb.astype(BF16)], axis=1)], axis=0)
        both = jnp.dot(jnp.concatenate([w_ref[0, bb, :, cols], q_in.astype(BF16)], axis=0), s_bd,
                       preferred_element_type=F32)
        stage1.append((both, k_out, jnp.exp(glast)))
    stage2 = []
    for (bb, d, hp), (both, k_out, _) in zip(chains, stage1):
        q_ref, k_ref, u_ref, w_ref, qk_ref, gc_ref, o_ref = directions[d]
        cols = slice(hp * GDN_PAIR, (hp + 1) * GDN_PAIR)
        v_new = u_ref[0, bb, :, cols] - both[:CHUNK]
        intra = jnp.dot(qk_ref[0, bb, 0, hp], _pair_blockdiag_rows(v_new, dk).astype(BF16), preferred_element_type=F32)
        upd = _dot_tn(k_out, v_new)
        stage2.append((intra, upd))
    for (bb, d, hp), (both, _, chunk_decay), (intra, upd) in zip(chains, stage1, stage2):
        o_ref = directions[d][6]
        cols = slice(hp * GDN_PAIR, (hp + 1) * GDN_PAIR)
        o_ref[bb, :, cols] = both[CHUNK:] + intra
        s_ref[bb, d, 2 * hp] = s_ref[bb, d, 2 * hp] * chunk_decay[:, 2 * hp:2 * hp + 1] + upd[:dk, :dk]
        s_ref[bb, d, 2 * hp + 1] = s_ref[bb, d, 2 * hp + 1] * chunk_decay[:, 2 * hp + 1:2 * hp + 2] + upd[dk:, dk:]


def _scan_tile(d, step, n_tiles, ctx_tiles):
    back = jnp.where(step < ctx_tiles, ctx_tiles - 1 - step, n_tiles + ctx_tiles - 1 - step)
    return jnp.where(d == 0, step, back)


def _gdn_scan(qkv, u, w, qk, gc):
    b, s, _ = qkv.shape
    nh, dk = GDN_HEADS, GDN_DK
    nc = s // CHUNK
    width = nh * dk
    tile = functools.partial(_scan_tile, n_tiles=nc, ctx_tiles=TM // CHUNK)

    nb = SCAN_BATCH
    assert b % nb == 0

    def specs(d):
        return [pl.BlockSpec((nb, CHUNK, width), lambda bi, i: (bi, tile(d, i), 0)),
                pl.BlockSpec((nb, CHUNK, width), lambda bi, i: (bi, tile(d, i), 1)),
                pl.BlockSpec((1, nb, CHUNK, width), lambda bi, i: (d, bi, tile(d, i), 0)),
                pl.BlockSpec((1, nb, CHUNK, width), lambda bi, i: (d, bi, tile(d, i), 0)),
                pl.BlockSpec((1, nb, 1, GDN_PAIRS, CHUNK, 2 * CHUNK), lambda bi, i: (d, bi, tile(d, i), 0, 0, 0)),
                pl.BlockSpec((1, nb, CHUNK, nh), lambda bi, i: (d, bi, tile(d, i), 0))]

    out = jax.ShapeDtypeStruct((b, s, width), F32)
    return pl.pallas_call(
        _gdn_scan_kernel,
        grid=(b // nb, nc),
        in_specs=specs(0) + specs(1),
        out_specs=[pl.BlockSpec((nb, CHUNK, width), lambda bi, i: (bi, tile(0, i), 0)),
                   pl.BlockSpec((nb, CHUNK, width), lambda bi, i: (bi, tile(1, i), 0))],
        out_shape=[out, out],
        scratch_shapes=[pltpu.VMEM((nb, 2, nh, dk, dk), F32)],
        compiler_params=_params("parallel", "arbitrary"),
        name="gdn_scan",
    )(qkv, qkv, u, w, qk, gc, qkv, qkv, u, w, qk, gc)


def _gdn_out_kernel(of_ref, ob_ref, z_ref, x_ref, mod_ref, ng_ref, w_ref, out_ref):
    m = mod_ref[0, 0]
    o = of_ref[0] + ob_ref[0]
    ng = ng_ref[...]
    parts = []
    for h in range(GDN_HEADS):
        seg = o[:, h * GDN_DK:(h + 1) * GDN_DK]
        parts.append(seg * lax.rsqrt(jnp.mean(seg * seg, axis=-1, keepdims=True) + EPS) * ng)
    y = _dot(jnp.concatenate(parts, axis=-1) * _silu(z_ref[0].astype(F32)), w_ref[...])
    out_ref[0] = x_ref[0] + m[2:3] * y


def _gdn_out(o_f, o_b, p, x, modv, norm_g, w_out_bf16):
    b, s, d = x.shape
    width = GDN_HEADS * GDN_DK
    return pl.pallas_call(
        _gdn_out_kernel,
        grid=(b, s // TM),
        in_specs=[pl.BlockSpec((1, TM, width), lambda bi, t: (bi, t, 0)),
                  pl.BlockSpec((1, TM, width), lambda bi, t: (bi, t, 0)),
                  pl.BlockSpec((1, TM, width), lambda bi, t: (bi, t, 3)),
                  pl.BlockSpec((1, TM, d), lambda bi, t: (bi, t, 0)),
                  pl.BlockSpec((1, 1, 6, d), _mod_spec()),
                  pl.BlockSpec((1, GDN_DK), lambda bi, t: (0, 0)),
                  pl.BlockSpec((width, d), lambda bi, t: (0, 0))],
        out_specs=pl.BlockSpec((1, TM, d), lambda bi, t: (bi, t, 0)),
        out_shape=jax.ShapeDtypeStruct((b, s, d), F32),
        compiler_params=_params("parallel", "parallel"),
        name="gdn_out",
    )(o_f, o_b, p, x, modv, norm_g.reshape(1, GDN_DK), w_out_bf16)


def _mixer_gdn(x, modv, norm_g, w_in, conv_w, a_log, dt_bias, out_norm_g, w_out):
    nh, dk = GDN_HEADS, GDN_DK
    n_main = 4 * nh * dk
    p = _in_proj(x, modv, norm_g, w_in[:, :n_main].astype(BF16), n_main)
    gc, bt = _gdn_gates(x, modv, norm_g, w_in[:, n_main:], a_log, dt_bias)
    b, s, _ = x.shape
    gctp = gc.reshape(2, b, s // CHUNK, CHUNK, GDN_PAIRS, 2).transpose(0, 1, 2, 4, 5, 3)
    gctp = gctp.reshape(2, b, s // CHUNK, GDN_PAIRS, 2 * CHUNK)
    qkv = _gdn_conv(p, conv_w)
    t = _gdn_inv(_gdn_a(qkv, gc, gctp, bt))
    u, w, qk = _gdn_uw(t, qkv, gc, gctp, bt)
    o_f, o_b = _gdn_scan(qkv, u, w, qk, gc)
    return _gdn_out(o_f, o_b, p, x, modv, out_norm_g, w_out.astype(BF16))


def _pool_kernel(x_ref, xp_ref, xn_ref, mod_ref, g_ref, w_ref, b_ref, sc_ref, o_ref, ext_ref):
    t = pl.program_id(1)
    nt = pl.num_programs(1)
    prev_ok, next_ok = _halo_valid()
    m = mod_ref[0, 0]
    g = g_ref[...]
    x = x_ref[0]
    h = _rms_mod(x, g, m[0:1], m[1:2])
    ext_ref[0:HALO] = jnp.where(prev_ok, _rms_mod(xp_ref[0], g, m[0:1], m[1:2]), 0.0)
    ext_ref[HALO:HALO + TM] = h
    ext_ref[HALO + TM:] = jnp.where(next_ok, _rms_mod(xn_ref[0], g, m[0:1], m[1:2]), 0.0)
    row = lax.broadcasted_iota(jnp.int32, (TM, 1), 0)
    pos = row + jnp.where(t == 0, 0, (t - 1) * TM)
    n_seq = jnp.where(t == 0, TM, (nt - 1) * TM)
    pg = POOL_GROUP
    for gi, win in enumerate(POOL_WINDOWS):
        lo_off = HALO - win // 2
        acc = ext_ref[lo_off:lo_off + TM, gi * pg:(gi + 1) * pg]
        for k in range(1, win):
            acc = acc + ext_ref[lo_off + k:lo_off + k + TM, gi * pg:(gi + 1) * pg]
        lo = jnp.clip(pos - win // 2, 0, n_seq)
        hi = jnp.clip(pos + win - win // 2, 0, n_seq)
        pooled = acc / (hi - lo).astype(F32) - h[:, gi * pg:(gi + 1) * pg]
        y = (_dot(pooled, w_ref[gi]) + b_ref[gi]) * sc_ref[:, gi * pg:(gi + 1) * pg]
        o_ref[0, :, gi * pg:(gi + 1) * pg] = x[:, gi * pg:(gi + 1) * pg] + m[2:3, gi * pg:(gi + 1) * pg] * y


def _mixer_pool(x, modv, norm_g, w_group, b_group, scale):
    b, s, d = x.shape
    ng, pg = len(POOL_WINDOWS), POOL_GROUP
    prev_spec, next_spec = _halo_specs(d, lambda: 0, s // TM)
    return pl.pallas_call(
        _pool_kernel,
        grid=(b, s // TM),
        in_specs=[pl.BlockSpec((1, TM, d), lambda bi, t: (bi, t, 0)),
                  prev_spec, next_spec,
                  pl.BlockSpec((1, 1, 6, d), _mod_spec()),
                  pl.BlockSpec((1, d), lambda bi, t: (0, 0)),
                  pl.BlockSpec((ng, pg, pg), lambda bi, t: (0, 0, 0)),
                  pl.BlockSpec((ng, 1, pg), lambda bi, t: (0, 0, 0)),
                  pl.BlockSpec((1, d), lambda bi, t: (0, 0))],
        out_specs=pl.BlockSpec((1, TM, d), lambda bi, t: (bi, t, 0)),
        out_shape=jax.ShapeDtypeStruct((b, s, d), F32),
        scratch_shapes=[pltpu.VMEM((TM + 2 * HALO, d), F32)],
        compiler_params=_params("parallel", "parallel"),
        name="pool_mixer",
    )(x, x, x, modv, norm_g.reshape(1, d), w_group.astype(BF16), b_group.reshape(ng, 1, pg), scale.reshape(1, d))


def _rotate(t, cos, sin_signed):
    half = RET_DK // 2
    swapped = jnp.concatenate([pltpu.roll(t[:, :half], half // 2, 1), pltpu.roll(t[:, half:], half // 2, 1)], axis=-1)
    return t * cos + swapped * sin_signed


def _ret_scan_kernel(lg_ref, q_ref, k_ref, v_ref, cos_ref, sin_ref, o_ref, s_ref):
    d = pl.program_id(1)
    step = pl.program_id(2)

    @pl.when(step == 0)
    def _():
        s_ref[...] = jnp.zeros_like(s_ref)

    fwd = d == 0
    r = lax.broadcasted_iota(jnp.int32, (TM, TM), 0)
    c = lax.broadcasted_iota(jnp.int32, (TM, TM), 1)
    rel = jnp.where(fwd, r - c, c - r).astype(F32)
    row = lax.broadcasted_iota(jnp.int32, (TM, 1), 0)
    q_pow = jnp.where(fwd, row + 1, TM - row).astype(F32)
    k_pow = jnp.where(fwd, TM - 1 - row, row).astype(F32)
    cos = cos_ref[...]
    sin = sin_ref[...]
    dk, dv = RET_DK, RET_DV
    for h in range(RET_HEADS):
        lg = jnp.full((1, 1), lg_ref[d, h], F32)
        q = _rotate(q_ref[0, :, h * dk:(h + 1) * dk].astype(F32), cos, sin)
        k = _rotate(k_ref[0, :, h * dk:(h + 1) * dk].astype(F32) * (dk ** -0.5), cos, sin)
        v = v_ref[0, :, h * dv:(h + 1) * dv]
        dmat = jnp.where(rel >= 0, jnp.exp(jnp.maximum(rel, 0.0) * lg), 0.0)
        inner = _dot_nt(q, k) * dmat
        s = s_ref[h]
        o_ref[0, 0, :, h * dv:(h + 1) * dv] = _dot(inner, v) + _dot(q * jnp.exp(q_pow * lg), s)
        s_ref[h] = s * jnp.exp(TM * lg) + _dot_tn(k * jnp.exp(k_pow * lg), v)


def _ret_scan(p, log_gamma, cos, sin):
    b, s, _ = p.shape
    nt = s // TM
    qw, vw = RET_HEADS * RET_DK, RET_HEADS * RET_DV
    tile = functools.partial(_scan_tile, n_tiles=nt, ctx_tiles=1)
    return pl.pallas_call(
        _ret_scan_kernel,
        grid=(b, 2, nt),
        in_specs=[pl.BlockSpec(memory_space=pltpu.SMEM),
                  pl.BlockSpec((1, TM, qw), lambda bi, d, i: (bi, tile(d, i), 0)),
                  pl.BlockSpec((1, TM, qw), lambda bi, d, i: (bi, tile(d, i), 1)),
                  pl.BlockSpec((1, TM, vw), lambda bi, d, i: (bi, tile(d, i), 1)),
                  pl.BlockSpec((TM, RET_DK), lambda bi, d, i: (tile(d, i), 0)),
                  pl.BlockSpec((TM, RET_DK), lambda bi, d, i: (tile(d, i), 0))],
        out_specs=pl.BlockSpec((1, 1, TM, vw), lambda bi, d, i: (d, bi, tile(d, i), 0)),
        out_shape=jax.ShapeDtypeStruct((2, b, s, vw), F32),
        scratch_shapes=[pltpu.VMEM((RET_HEADS, RET_DK, RET_DV), F32)],
        compiler_params=_params("parallel", "parallel", "arbitrary"),
        name="ret_scan",
    )(log_gamma, p, p, p, cos, sin)


def _ret_out_kernel(o_ref, gate_ref, x_ref, mod_ref, ng_ref, w_ref, out_ref):
    m = mod_ref[0, 0]
    o = o_ref[0, 0] + o_ref[1, 0]
    ng = ng_ref[...]
    parts = []
    for h in range(RET_HEADS):
        seg = o[:, h * RET_DV:(h + 1) * RET_DV]
        mu = jnp.mean(seg, axis=-1, keepdims=True)
        cen = seg - mu
        var = jnp.mean(cen * cen, axis=-1, keepdims=True)
        parts.append(cen * lax.rsqrt(var + EPS) * ng)
    y = _dot(_silu(gate_ref[0].astype(F32)) * jnp.concatenate(parts, axis=-1), w_ref[...])
    out_ref[0] = x_ref[0] + m[2:3] * y


def _ret_out(o, p, x, modv, norm_g, w_out_bf16):
    b, s, d = x.shape
    vw = RET_HEADS * RET_DV
    return pl.pallas_call(
        _ret_out_kernel,
        grid=(b, s // TM),
        in_specs=[pl.BlockSpec((2, 1, TM, vw), lambda bi, t: (0, bi, t, 0)),
                  pl.BlockSpec((1, TM, vw), lambda bi, t: (bi, t, 2)),
                  pl.BlockSpec((1, TM, d), lambda bi, t: (bi, t, 0)),
                  pl.BlockSpec((1, 1, 6, d), _mod_spec()),
                  pl.BlockSpec((1, RET_DV), lambda bi, t: (0, 0)),
                  pl.BlockSpec((vw, d), lambda bi, t: (0, 0))],
        out_specs=pl.BlockSpec((1, TM, d), lambda bi, t: (bi, t, 0)),
        out_shape=jax.ShapeDtypeStruct((b, s, d), F32),
        compiler_params=_params("parallel", "parallel"),
        name="ret_out",
    )(o, p, x, modv, norm_g.reshape(1, RET_DV), w_out_bf16)


def _rotary_tables(s):
    n_lat = s - TM
    pos = jnp.arange(n_lat, dtype=jnp.int32)
    rows = (pos // GRID_W).astype(F32)
    cols = (pos % GRID_W).astype(F32)
    quarter = RET_DK // 4
    inv_freq = ROPE_BASE ** (-jnp.arange(quarter, dtype=F32) / quarter)
    ang_r = rows[:, None] * inv_freq[None, :]
    ang_c = cols[:, None] * inv_freq[None, :]
    cos = jnp.concatenate([jnp.cos(ang_r), jnp.cos(ang_r), jnp.cos(ang_c), jnp.cos(ang_c)], axis=-1)
    sin = jnp.concatenate([-jnp.sin(ang_r), jnp.sin(ang_r), -jnp.sin(ang_c), jnp.sin(ang_c)], axis=-1)
    cos = jnp.concatenate([jnp.ones((TM, RET_DK), F32), cos], axis=0)
    sin = jnp.concatenate([jnp.zeros((TM, RET_DK), F32), sin], axis=0)
    return cos, sin


def _mixer_retention(x, modv, norm_g, w_in, decay_logit, out_norm_g, w_out):
    p = _in_proj(x, modv, norm_g, w_in.astype(BF16), w_in.shape[1] // 2)
    cos, sin = _rotary_tables(x.shape[1])
    o = _ret_scan(p, jax.nn.log_sigmoid(decay_logit.astype(F32)), cos, sin)
    return _ret_out(o, p, x, modv, out_norm_g, w_out.astype(BF16))


ROUTE_LANES = 128
ROUTE_COLS = 8
ISSUE_UNROLL = 8


def _route_kernel(x_ref, mod_ref, g_ref, w_ref, b_ref, rt_ref, cnt_out_ref, cnt_ref):
    first = (pl.program_id(0) == 0) & (pl.program_id(1) == 0)

    @pl.when(first)
    def _():
        cnt_ref[...] = jnp.zeros_like(cnt_ref)

    m = mod_ref[0, 0]
    h = _rms_mod(x_ref[0], g_ref[...], m[3:4], m[4:5])
    logits = _dot3(h, w_ref[...]) + b_ref[...]
    lane = lax.broadcasted_iota(jnp.int32, (TM, ROUTE_LANES), 1)
    big = jnp.int32(ROUTE_LANES)
    neg = -jnp.inf
    glog = jnp.where((lane >= N_EXPERTS) & (lane < N_EXPERTS + N_GROUPS), logits, neg)
    gmax = jnp.max(glog, axis=-1, keepdims=True)
    gsel = jnp.min(jnp.where(glog == gmax, lane, big), axis=-1, keepdims=True) - N_EXPERTS
    p_group = 1.0 / jnp.sum(jnp.exp(glog - gmax), axis=-1, keepdims=True)
    elog = jnp.where((lane >= gsel * EXPERTS_PER_GROUP) & (lane < (gsel + 1) * EXPERTS_PER_GROUP), logits, neg)
    m1 = jnp.max(elog, axis=-1, keepdims=True)
    i1 = jnp.min(jnp.where(elog == m1, lane, big), axis=-1, keepdims=True)
    elog2 = jnp.where(lane == i1, neg, elog)
    m2 = jnp.max(elog2, axis=-1, keepdims=True)
    i2 = jnp.min(jnp.where(elog2 == m2, lane, big), axis=-1, keepdims=True)
    e2 = jnp.exp(m2 - m1)
    w1 = p_group / (1.0 + e2)
    w2 = p_group * e2 / (1.0 + e2)
    onehot = jnp.where((lane == i1) | (lane == i2), 1.0, 0.0)
    r = lax.broadcasted_iota(jnp.int32, (TM, TM), 0)
    c = lax.broadcasted_iota(jnp.int32, (TM, TM), 1)
    before = _dot(jnp.where(c < r, 1.0, 0.0), onehot) + cnt_ref[...]
    r1 = jnp.sum(jnp.where(lane == i1, before, 0.0), axis=-1, keepdims=True)
    r2 = jnp.sum(jnp.where(lane == i2, before, 0.0), axis=-1, keepdims=True)
    cnt_ref[...] = cnt_ref[...] + jnp.sum(onehot, axis=0, keepdims=True)
    cnt_out_ref[...] = cnt_ref[...]
    vals = (i1.astype(F32), i2.astype(F32), w1, w2, r1, r2)
    out = jnp.zeros((TM, ROUTE_LANES), F32)
    for pos_, val in enumerate(vals):
        out = jnp.where(lane == pos_, val, out)
    rt_ref[0] = out[:, :ROUTE_COLS]


def _route(x, modv, norm_g, w_route, b_route):
    b, s, d = x.shape
    return pl.pallas_call(
        _route_kernel,
        grid=(b, s // TM),
        in_specs=[pl.BlockSpec((1, TM, d), lambda bi, t: (bi, t, 0)),
                  pl.BlockSpec((1, 1, 6, d), _mod_spec()),
                  pl.BlockSpec((1, d), lambda bi, t: (0, 0)),
                  pl.BlockSpec((d, ROUTE_LANES), lambda bi, t: (0, 0)),
                  pl.BlockSpec((1, ROUTE_LANES), lambda bi, t: (0, 0))],
        out_specs=[pl.BlockSpec((1, TM, ROUTE_COLS), lambda bi, t: (bi, t, 0)),
                   pl.BlockSpec((1, ROUTE_LANES), lambda bi, t: (0, 0))],
        out_shape=[jax.ShapeDtypeStruct((b, s, ROUTE_COLS), F32), jax.ShapeDtypeStruct((1, ROUTE_LANES), F32)],
        scratch_shapes=[pltpu.VMEM((1, ROUTE_LANES), F32)],
        compiler_params=_params("arbitrary", "arbitrary"),
        name="moe_route",
    )(x, modv, norm_g.reshape(1, d), w_route, b_route)


def _issue_row_copies(make_copy):
    def body(i, carry):
        for k in range(2):
            make_copy(i, k).start()
        return carry

    lax.fori_loop(0, TM, body, 0, unroll=ISSUE_UNROLL)


def _tile_step():
    step = pl.program_id(0) * pl.num_programs(1) + pl.program_id(1)
    return step, pl.num_programs(0) * pl.num_programs(1), jnp.bitwise_and(step, 1)


def _pack_bf16_pairs(h):
    bits = pltpu.bitcast(h.astype(BF16).astype(F32), jnp.uint32)
    half = h.shape[1] // 2
    return jnp.bitwise_or(jnp.right_shift(bits[:, :half], jnp.uint32(16)),
                          jnp.bitwise_and(bits[:, half:], jnp.uint32(0xFFFF0000)))


def _unpack_bf16_pairs(w):
    lo = pltpu.bitcast(jnp.left_shift(w, jnp.uint32(16)), F32)
    hi = pltpu.bitcast(jnp.bitwise_and(w, jnp.uint32(0xFFFF0000)), F32)
    return jnp.concatenate([lo.astype(BF16), hi.astype(BF16)], axis=1)


def _dispatch_kernel(dest_ref, pe_ref, x_ref, mod_ref, g_ref, xb_ref, h_ref, z_ref, sem, zsem):
    step, n_steps, slot = _tile_step()

    @pl.when(step == 0)
    def _():
        z_ref[...] = jnp.zeros_like(z_ref)

        def clear(e):
            start = pl.multiple_of(pe_ref[e] - MOE_BLOCK, MOE_BLOCK)
            return pltpu.make_async_copy(z_ref, xb_ref.at[pl.ds(start, MOE_BLOCK)], zsem.at[e])

        def used(e):
            return pe_ref[e] > (pe_ref[e - 1] if e else 0)

        for e in range(N_EXPERTS):
            @pl.when(used(e))
            def _():
                clear(e).start()
        for e in range(N_EXPERTS):
            @pl.when(used(e))
            def _():
                clear(e).wait()

        def clear_unused(blk, carry):
            cp = pltpu.make_async_copy(z_ref, xb_ref.at[pl.ds(pl.multiple_of(blk * MOE_BLOCK, MOE_BLOCK), MOE_BLOCK)],
                                       zsem.at[0])
            cp.start()
            cp.wait()
            return carry

        lax.fori_loop(_block_of(pe_ref[N_EXPERTS - 1], MOE_BLOCK), xb_ref.shape[0] // MOE_BLOCK, clear_unused, 0)

    m = mod_ref[0, 0]
    h_ref[slot] = _pack_bf16_pairs(_rms_mod(x_ref[0], g_ref[...], m[3:4], m[4:5]))
    _issue_row_copies(lambda i, k: pltpu.make_async_copy(
        h_ref.at[slot, pl.ds(i, 1)], xb_ref.at[pl.ds(dest_ref[0, 0, 2 * i + k], 1)], sem.at[slot]))

    def wait_tile(sl):
        for _ in range(2):
            pltpu.make_async_copy(h_ref.at[sl], xb_ref.at[pl.ds(0, TM)], sem.at[sl]).wait()

    @pl.when(step > 0)
    def _():
        wait_tile(1 - slot)

    @pl.when(step == n_steps - 1)
    def _():
        wait_tile(slot)


def _dispatch(x, modv, norm_g, dest, pad_end, n_rows):
    b, s, d = x.shape
    nt = s // TM
    return pl.pallas_call(
        _dispatch_kernel,
        grid=(b, nt),
        in_specs=[pl.BlockSpec((1, 1, 2 * TM), lambda bi, t: (bi * nt + t, 0, 0), memory_space=pltpu.SMEM),
                  pl.BlockSpec(memory_space=pltpu.SMEM),
                  pl.BlockSpec((1, TM, d), lambda bi, t: (bi, t, 0)),
                  pl.BlockSpec((1, 1, 6, d), _mod_spec()),
                  pl.BlockSpec((1, d), lambda bi, t: (0, 0))],
        out_specs=pl.BlockSpec(memory_space=pl.ANY),
        out_shape=jax.ShapeDtypeStruct((n_rows, d // 2), jnp.uint32),
        scratch_shapes=[pltpu.VMEM((2, TM, d // 2), jnp.uint32), pltpu.VMEM((MOE_BLOCK, d // 2), jnp.uint32),
                        pltpu.SemaphoreType.DMA((2,)), pltpu.SemaphoreType.DMA((N_EXPERTS,))],
        compiler_params=_params("arbitrary", "arbitrary"),
        name="moe_dispatch",
    )(dest.reshape(b * nt, 1, 2 * TM), pad_end, x, modv, norm_g.reshape(1, d))


def _expert_kernel(be_ref, nu_ref, x_ref, wg_ref, wu_ref, wd_ref, o_ref, wg_bf, wu_bf, wd_bf):
    i = pl.program_id(0)

    @pl.when(i < nu_ref[0])
    def _():
        @pl.when((i == 0) | (be_ref[i] != be_ref[jnp.maximum(i - 1, 0)]))
        def _():
            wg_bf[...] = wg_ref[0, 0].astype(BF16)
            wu_bf[...] = wu_ref[0, 0].astype(BF16)
            wd_bf[...] = wd_ref[0, 0].astype(BF16)

        xb = _unpack_bf16_pairs(x_ref[...])
        act = _silu(jnp.dot(xb, wg_bf[...], preferred_element_type=F32)) * jnp.dot(xb, wu_bf[...], preferred_element_type=F32)
        o_ref[...] = jnp.dot(act.astype(BF16), wd_bf[...], preferred_element_type=F32)

    @pl.when(i >= nu_ref[0])
    def _():
        o_ref[...] = jnp.zeros_like(o_ref)


def _experts(xb, block_e, n_used, layer, w_gate, w_up, w_down):
    n_rows = xb.shape[0]
    d, de = w_gate.shape[-2:]
    nb = n_rows // MOE_BLOCK

    def last_used(i, nu):
        return jnp.minimum(i, nu[0] - 1)

    grid_spec = pltpu.PrefetchScalarGridSpec(
        num_scalar_prefetch=2,
        grid=(nb,),
        in_specs=[pl.BlockSpec((MOE_BLOCK, d // 2), lambda i, be, nu: (last_used(i, nu), 0)),
                  pl.BlockSpec((1, 1, d, de), lambda i, be, nu: (layer, be[last_used(i, nu)], 0, 0)),
                  pl.BlockSpec((1, 1, d, de), lambda i, be, nu: (layer, be[last_used(i, nu)], 0, 0)),
                  pl.BlockSpec((1, 1, de, d), lambda i, be, nu: (layer, be[last_used(i, nu)], 0, 0))],
        out_specs=pl.BlockSpec((MOE_BLOCK, d), lambda i, be, nu: (i, 0)),
        scratch_shapes=[pltpu.VMEM((d, de), BF16), pltpu.VMEM((d, de), BF16), pltpu.VMEM((de, d), BF16)],
    )
    return pl.pallas_call(
        _expert_kernel,
        grid_spec=grid_spec,
        out_shape=jax.ShapeDtypeStruct((n_rows, d), F32),
        compiler_params=_params("arbitrary"),
        name="moe_experts",
    )(block_e, n_used, xb, w_gate, w_up, w_down)


def _combine_kernel(dest_ref, next_dest_ref, x_ref, mod_ref, rt_ref, fg_ref, yb_ref, o_ref, y_ref, sem, *, final_norm):
    step, n_steps, slot = _tile_step()

    def gather(d_ref, sl):
        _issue_row_copies(lambda i, k: pltpu.make_async_copy(
            yb_ref.at[pl.ds(d_ref[0, 0, 2 * i + k], 1)], y_ref.at[sl, pl.ds(k * TM + i, 1)], sem.at[sl]))

    @pl.when(step == 0)
    def _():
        gather(dest_ref, 0)

    @pl.when(step + 1 < n_steps)
    def _():
        gather(next_dest_ref, 1 - slot)

    pltpu.make_async_copy(yb_ref.at[pl.ds(0, 2 * TM)], y_ref.at[slot], sem.at[slot]).wait()
    m = mod_ref[0, 0]
    rt = rt_ref[0]
    out = x_ref[0] + m[5:6] * (rt[:, 2:3] * y_ref[slot, 0:TM] + rt[:, 3:4] * y_ref[slot, TM:2 * TM])
    if final_norm:
        out = out * lax.rsqrt(jnp.mean(out * out, axis=-1, keepdims=True) + EPS) * fg_ref[...]
    o_ref[0] = out


def _combine(x, modv, rt, dest, yb, final_g, final_norm):
    b, s, d = x.shape
    nt = s // TM
    skip = 1 if final_norm else 0

    def next_tile(bi, t):
        wrap = t + 1 >= nt - skip
        nb_ = jnp.minimum(jnp.where(wrap, bi + 1, bi), b - 1)
        return (nb_ * nt + jnp.where(wrap, 0, t + 1) + skip, 0, 0)

    return pl.pallas_call(
        functools.partial(_combine_kernel, final_norm=final_norm),
        grid=(b, nt - skip),
        in_specs=[pl.BlockSpec((1, 1, 2 * TM), lambda bi, t: (bi * nt + t + skip, 0, 0), memory_space=pltpu.SMEM),
                  pl.BlockSpec((1, 1, 2 * TM), next_tile, memory_space=pltpu.SMEM),
                  pl.BlockSpec((1, TM, d), lambda bi, t: (bi, t + skip, 0)),
                  pl.BlockSpec((1, 1, 6, d), lambda bi, t: (bi, jnp.minimum(t + skip, 1), 0, 0)),
                  pl.BlockSpec((1, TM, ROUTE_COLS), lambda bi, t: (bi, t + skip, 0)),
                  pl.BlockSpec((1, d), lambda bi, t: (0, 0)),
                  pl.BlockSpec(memory_space=pl.ANY)],
        out_specs=pl.BlockSpec((1, TM, d), lambda bi, t: (bi, t, 0)),
        out_shape=jax.ShapeDtypeStruct((b, s - skip * TM, d), F32),
        scratch_shapes=[pltpu.VMEM((2, 2 * TM, d), F32), pltpu.SemaphoreType.DMA((2,))],
        compiler_params=_params("arbitrary", "arbitrary"),
        name="moe_combine",
    )(dest.reshape(b * nt, 1, 2 * TM), dest.reshape(b * nt, 1, 2 * TM), x, modv, rt, final_g.reshape(1, d), yb)


def _hier_moe(x, modv, norm_g, layer, wg_r, bg_r, we_r, be_r, w_gate, w_up, w_down, final_g, final_norm):
    b, s, d = x.shape
    n_tok = b * s
    pad = ROUTE_LANES - N_EXPERTS - N_GROUPS
    w_route = jnp.concatenate([we_r, wg_r, jnp.zeros((d, pad), F32)], axis=1)
    b_route = jnp.concatenate([be_r, bg_r, jnp.zeros((pad,), F32)]).reshape(1, ROUTE_LANES)
    rt, cnt = _route(x, modv, norm_g, w_route, b_route)
    counts = cnt[0, :N_EXPERTS].astype(jnp.int32)
    padded = (counts + MOE_BLOCK - 1) // MOE_BLOCK * MOE_BLOCK
    pad_end = jnp.cumsum(padded)
    pad_start = pad_end - padded
    n_blocks = -(-(2 * n_tok + N_EXPERTS * (MOE_BLOCK - 1)) // MOE_BLOCK)
    rt2 = rt.reshape(n_tok, ROUTE_COLS)
    expert = rt2[:, 0:2].astype(jnp.int32)
    experts = jnp.arange(N_EXPERTS, dtype=jnp.int32)
    start_of = jnp.sum(jnp.where(expert[..., None] == experts, pad_start, 0), axis=-1)
    dest = (start_of + rt2[:, 4:6].astype(jnp.int32)).reshape(-1)
    block_start = jnp.arange(n_blocks, dtype=jnp.int32) * MOE_BLOCK
    block_e = jnp.minimum(jnp.sum((pad_end[None, :] <= block_start[:, None]).astype(jnp.int32), axis=1), N_EXPERTS - 1)
    n_used = (pad_end[-1:] // MOE_BLOCK).astype(jnp.int32)
    xb = _dispatch(x, modv, norm_g, dest, pad_end.astype(jnp.int32), n_blocks * MOE_BLOCK)
    yb = _experts(xb, block_e, n_used, layer, w_gate, w_up, w_down)
    return _combine(x, modv, rt, dest, yb, final_g, final_norm)


def kernel(x, c, ctx, c_ctx, mod_w, mod_b, norm1_g, norm2_g, final_g, gdn_w_in, gdn_conv_w, gdn_a_log, gdn_dt_bias, gdn_norm_g, gdn_w_out, pool_w, pool_b, pool_scale, ret_w_in, ret_decay_logit, ret_norm_g, ret_w_out, router_group_w, router_group_b, router_expert_w, router_expert_b, exp_w_gate, exp_w_up, exp_w_down):
    b, n_lat, d = x.shape
    depth = mod_w.shape[0]
    assert ctx.shape[1] == TM and n_lat % TM == 0 and b < 16
    xs = jnp.concatenate([ctx, x], axis=1)
    cvec = jnp.concatenate([c, c_ctx[None], jnp.zeros((15 - b, d), F32)], axis=0)
    mods = _modulation(cvec, mod_w, mod_b)
    for i in range(depth):
        j, kind = i // 3, i % 3
        lat = mods[i, :b].reshape(b, 1, 6, d)
        con = jnp.broadcast_to(mods[i, b].reshape(1, 1, 6, d), (b, 1, 6, d))
        modv = jnp.concatenate([con, lat], axis=1)
        if kind == 0:
            xs = _mixer_gdn(xs, modv, norm1_g[i], gdn_w_in[j], gdn_conv_w[j], gdn_a_log[j], gdn_dt_bias[j],
                            gdn_norm_g[j], gdn_w_out[j])
        elif kind == 1:
            xs = _mixer_pool(xs, modv, norm1_g[i], pool_w[j], pool_b[j], pool_scale[j])
        else:
            xs = _mixer_retention(xs, modv, norm1_g[i], ret_w_in[j], ret_decay_logit[j], ret_norm_g[j], ret_w_out[j])
        xs = _hier_moe(xs, modv, norm2_g[i], i, router_group_w[i], router_group_b[i], router_expert_w[i],
                       router_expert_b[i], exp_w_gate, exp_w_up, exp_w_down, final_g, i == depth - 1)
    return xs
```

```python
import functools
import math

import jax
import jax.numpy as jnp
from jax import lax
from jax.experimental import pallas as pl
from jax.experimental.pallas import tpu as pltpu

F32 = jnp.float32
BF16 = jnp.bfloat16
HIGHEST = lax.Precision.HIGHEST

EPS = 1e-6
TM = 256
HALO = 8
HALO_BF16 = 16
CHUNK = 64
GDN_HEADS = 8
GDN_DK = 128
GDN_CONV = 5
RET_HEADS = 4
RET_DK = 256
RET_DV = 512
ROPE_BASE = 10000.0
GRID_W = 64
POOL_WINDOWS = (2, 4, 8, 16)
POOL_GROUP = 256
N_GROUPS = 4
EXPERTS_PER_GROUP = 8
N_EXPERTS = 32
MOE_BLOCK = 512
V7X_VMEM_LIMIT_BYTES = 56 * 1024 * 1024


def _params(*sem):
    return pltpu.CompilerParams(dimension_semantics=sem, vmem_limit_bytes=V7X_VMEM_LIMIT_BYTES)


def _dot(a, b):
    return jnp.dot(a.astype(BF16), b.astype(BF16), preferred_element_type=F32)


def _dot_hi(a, b):
    return jnp.dot(a, b, precision=HIGHEST, preferred_element_type=F32)


def _dot_nt(a, b):
    return lax.dot_general(a.astype(BF16), b.astype(BF16), (((1,), (1,)), ((), ())), preferred_element_type=F32)


def _split(x):
    hi = x.astype(BF16)
    return hi, (x - hi.astype(F32)).astype(BF16)


def _dot3_parts(ah, al, bh, bl, dims):
    def d(p, q):
        return lax.dot_general(p, q, dims, preferred_element_type=F32)
    return d(ah, bh) + (d(ah, bl) + d(al, bh))


def _dot3(a, b):
    return _dot3_parts(*_split(a), *_split(b), (((1,), (0,)), ((), ())))


def _cumsum_dot(ones_mask, x):
    x1 = x.astype(BF16)
    r1 = x - x1.astype(F32)
    x2 = r1.astype(BF16)
    x3 = (r1 - x2.astype(F32)).astype(BF16)
    m = ones_mask.astype(BF16)
    return (jnp.dot(m, x1, preferred_element_type=F32) + jnp.dot(m, x2, preferred_element_type=F32)
            + jnp.dot(m, x3, preferred_element_type=F32))


def _dot_tn(a, b):
    return lax.dot_general(a.astype(BF16), b.astype(BF16), (((0,), (0,)), ((), ())), preferred_element_type=F32)


def _block_of(i, size):
    return jnp.right_shift(i, int(math.log2(size)))


def _silu(x):
    return x * jax.nn.sigmoid(x)


def _softplus(x):
    return jnp.maximum(x, 0.0) + jnp.log(1.0 + jnp.exp(-jnp.abs(x)))


def _rms_mod(x, g, shift, scale):
    y = x * lax.rsqrt(jnp.mean(x * x, axis=-1, keepdims=True) + EPS) * g
    return y * (1.0 + scale) + shift


def _mod_spec(grid_rank_prefix=0):
    def idx(*g):
        b, t = g[grid_rank_prefix], g[grid_rank_prefix + 1]
        return (b, jnp.minimum(t, 1), 0, 0)
    return idx


def _mod_kernel(c_ref, w_ref, b_ref, o_ref):
    o_ref[0] = _dot_hi(_silu(c_ref[...]), w_ref[0]) + b_ref[0]


def _modulation(cvec, mod_w, mod_b):
    n_layers, d, d6 = mod_w.shape
    return pl.pallas_call(
        _mod_kernel,
        grid=(n_layers, d6 // d),
        in_specs=[pl.BlockSpec((16, d), lambda l, j: (0, 0)),
                  pl.BlockSpec((1, d, d), lambda l, j: (l, 0, j)),
                  pl.BlockSpec((1, 1, d), lambda l, j: (l, 0, j))],
        out_specs=pl.BlockSpec((1, 16, d), lambda l, j: (l, 0, j)),
        out_shape=jax.ShapeDtypeStruct((n_layers, 16, d6), F32),
        compiler_params=_params("parallel", "parallel"),
        name="modulation",
    )(cvec, mod_w, mod_b.reshape(n_layers, 1, d6))


def _in_kernel(x_ref, mod_ref, g_ref, w_ref, o_ref):
    m = mod_ref[0, 0]
    h = _rms_mod(x_ref[0], g_ref[...], m[0:1], m[1:2])
    o_ref[0] = _dot(h, w_ref[...]).astype(o_ref.dtype)


def _in_proj(x, modv, g, w_bf16, tn):
    b, s, d = x.shape
    n = w_bf16.shape[1]
    return pl.pallas_call(
        _in_kernel,
        grid=(n // tn, b, s // TM),
        in_specs=[pl.BlockSpec((1, TM, d), lambda j, bi, t: (bi, t, 0)),
                  pl.BlockSpec((1, 1, 6, d), _mod_spec(1)),
                  pl.BlockSpec((1, d), lambda j, bi, t: (0, 0)),
                  pl.BlockSpec((d, tn), lambda j, bi, t: (0, j))],
        out_specs=pl.BlockSpec((1, TM, tn), lambda j, bi, t: (bi, t, j)),
        out_shape=jax.ShapeDtypeStruct((b, s, n), BF16),
        compiler_params=_params("parallel", "parallel", "parallel"),
        name="in_proj",
    )(x, modv, g.reshape(1, d), w_bf16)


def _gates_kernel(x_ref, mod_ref, g_ref, wab_ref, alog_ref, dtb_ref, gc_ref, bt_ref):
    m = mod_ref[0, 0]
    h = _rms_mod(x_ref[0], g_ref[...], m[0:1], m[1:2])
    ab = _dot3(h, wab_ref[...])
    nh = GDN_HEADS
    gate = -jnp.exp(alog_ref[...]) * _softplus(ab[:, :2 * nh] + dtb_ref[...])
    beta = jax.nn.sigmoid(ab[:, 2 * nh:])
    r = lax.broadcasted_iota(jnp.int32, (TM, TM), 0)
    c = lax.broadcasted_iota(jnp.int32, (TM, TM), 1)
    same = _block_of(r, CHUNK) == _block_of(c, CHUNK)
    cum_f = jnp.where(same & (c <= r), 1.0, 0.0)
    cum_b = jnp.where(same & (c >= r), 1.0, 0.0)
    gc_ref[0, 0] = _cumsum_dot(cum_f, gate[:, :nh])
    gc_ref[1, 0] = _cumsum_dot(cum_b, gate[:, nh:])
    bt_ref[0, 0] = beta[:, :nh]
    bt_ref[1, 0] = beta[:, nh:]


def _gdn_gates(x, modv, g, w_ab, a_log, dt_bias):
    b, s, d = x.shape
    nh = GDN_HEADS
    out = jax.ShapeDtypeStruct((2, b, s, nh), F32)
    return pl.pallas_call(
        _gates_kernel,
        grid=(b, s // TM),
        in_specs=[pl.BlockSpec((1, TM, d), lambda bi, t: (bi, t, 0)),
                  pl.BlockSpec((1, 1, 6, d), _mod_spec()),
                  pl.BlockSpec((1, d), lambda bi, t: (0, 0)),
                  pl.BlockSpec((d, 4 * nh), lambda bi, t: (0, 0)),
                  pl.BlockSpec((1, 2 * nh), lambda bi, t: (0, 0)),
                  pl.BlockSpec((1, 2 * nh), lambda bi, t: (0, 0))],
        out_specs=[pl.BlockSpec((2, 1, TM, nh), lambda bi, t: (0, bi, t, 0)),
                   pl.BlockSpec((2, 1, TM, nh), lambda bi, t: (0, bi, t, 0))],
        out_shape=[out, out],
        compiler_params=_params("parallel", "parallel"),
        name="gdn_gates",
    )(x, modv, g.reshape(1, d), w_ab, a_log.reshape(1, 2 * nh), dt_bias.reshape(1, 2 * nh))


def _halo_specs(width, col_of, n_tiles, halo=HALO):
    per = TM // halo

    def prev(bi, t, *rest):
        return (bi, jnp.maximum(t * per - 1, 0), col_of(*rest))

    def nxt(bi, t, *rest):
        return (bi, jnp.minimum((t + 1) * per, n_tiles * per - 1), col_of(*rest))

    return pl.BlockSpec((1, halo, width), prev), pl.BlockSpec((1, halo, width), nxt)


def _halo_valid():
    t = pl.program_id(1)
    nt = pl.num_programs(1)
    return t >= 2, (t >= 1) & (t < nt - 1)


def _conv_kernel(cur_ref, prev_ref, next_ref, w_ref, o_ref, k32_ref, ext_ref):
    j = pl.program_id(2)
    prev_ok, next_ok = _halo_valid()
    halo = HALO_BF16
    ext_ref[0:halo] = jnp.where(prev_ok, prev_ref[0].astype(F32), 0.0)
    ext_ref[halo:halo + TM] = cur_ref[0].astype(F32)
    ext_ref[halo + TM:] = jnp.where(next_ok, next_ref[0].astype(F32), 0.0)
    w = w_ref[...]
    base = halo - GDN_CONV // 2
    acc = w[0:1] * ext_ref[base:base + TM]
    for k in range(1, GDN_CONV):
        acc = acc + w[k:k + 1] * ext_ref[base + k:base + k + TM]
    y = _silu(acc)
    width = y.shape[1]
    is_v = j >= 2 * (GDN_HEADS * GDN_DK // width)
    is_q = j < (GDN_HEADS * GDN_DK // width)
    qscale = jnp.where(is_q, GDN_DK ** -0.5, 1.0)
    vals = []
    for hh in range(width // GDN_DK):
        seg = y[:, hh * GDN_DK:(hh + 1) * GDN_DK]
        nrm = seg * lax.rsqrt(jnp.sum(seg * seg, axis=-1, keepdims=True) + EPS) * qscale
        vals.append(jnp.where(is_v, seg, nrm))
        o_ref[0, :, hh * GDN_DK:(hh + 1) * GDN_DK] = vals[-1].astype(o_ref.dtype)

    @pl.when(j == 1)
    def _():
        for hh, val in enumerate(vals):
            k32_ref[0, :, hh * GDN_DK:(hh + 1) * GDN_DK] = val


def _gdn_conv(p, conv_w):
    b, s, _ = p.shape
    n = conv_w.shape[1]
    width = GDN_HEADS * GDN_DK
    prev_spec, next_spec = _halo_specs(width, lambda j: j, s // TM, HALO_BF16)
    return pl.pallas_call(
        _conv_kernel,
        grid=(b, s // TM, n // width),
        in_specs=[pl.BlockSpec((1, TM, width), lambda bi, t, j: (bi, t, j)),
                  prev_spec, next_spec,
                  pl.BlockSpec((GDN_CONV, width), lambda bi, t, j: (0, j))],
        out_specs=[pl.BlockSpec((1, TM, width), lambda bi, t, j: (bi, t, j)),
                   pl.BlockSpec((1, TM, width), lambda bi, t, j: (bi, t, 0))],
        out_shape=[jax.ShapeDtypeStruct((b, s, n), BF16), jax.ShapeDtypeStruct((b, s, width), F32)],
        scratch_shapes=[pltpu.VMEM((TM + 2 * HALO_BF16, width), F32)],
        compiler_params=_params("parallel", "parallel", "arbitrary"),
        name="gdn_conv",
    )(p, p, p, conv_w)


GDN_PAIR = 2 * GDN_DK
GDN_PAIRS = GDN_HEADS // 2
INV_LANES = 128
SCAN_BATCH = 4


def _pair_cols(cols, hp, width):
    lane = lax.broadcasted_iota(jnp.int32, (cols.shape[0], width), 1)
    return jnp.where(lane < width // 2, cols[:, 2 * hp:2 * hp + 1], cols[:, 2 * hp + 1:2 * hp + 2])


def _pair_blockdiag_rows(x, lane_block):
    n, w = x.shape
    r = lax.broadcasted_iota(jnp.int32, (2 * n, w), 0)
    c = lax.broadcasted_iota(jnp.int32, (2 * n, w), 1)
    same = _block_of(r, n) == jnp.bitwise_and(_block_of(c, lane_block), 1)
    return jnp.where(same, jnp.concatenate([x, x], axis=0), 0.0)


def _pair_decay(gc_cols, gct_row, hp, fwd, inclusive):
    r = lax.broadcasted_iota(jnp.int32, (CHUNK, 2 * CHUNK), 0)
    c = jnp.bitwise_and(lax.broadcasted_iota(jnp.int32, (CHUNK, 2 * CHUNK), 1), CHUNK - 1)
    ahead = (r - c) if fwd else (c - r)
    keep = (ahead >= 0) if inclusive else (ahead > 0)
    rel = _pair_cols(gc_cols, hp, 2 * CHUNK) - gct_row
    return jnp.where(keep, jnp.exp(jnp.where(keep, rel, 0.0)), 0.0)


def _gdn_a_kernel(k_ref, gc_ref, gctp_ref, bt_ref, a_ref):
    for ch in range(TM // CHUNK):
        rows = pl.ds(ch * CHUNK, CHUNK)
        for hp in range(GDN_PAIRS):
            kh, kl = _split(k_ref[0, rows, hp * GDN_PAIR:(hp + 1) * GDN_PAIR])
            kk = _dot3_parts(kh, kl, _pair_blockdiag_rows(kh, GDN_DK), _pair_blockdiag_rows(kl, GDN_DK),
                             (((1,), (1,)), ((), ())))
            for d in range(2):
                decay = _pair_decay(gc_ref[d, 0, rows, :], gctp_ref[d, 0, ch, hp:hp + 1, :], hp, d == 0, False)
                a_ref[d, 0, ch, hp] = _pair_cols(bt_ref[d, 0, rows, :], hp, 2 * CHUNK) * kk * decay


def _gdn_a(k32, gc, gctp, bt):
    b, s, _ = k32.shape
    nh = GDN_HEADS
    cpt = TM // CHUNK
    gate_spec = pl.BlockSpec((2, 1, TM, nh), lambda bi, t: (0, bi, t, 0))
    return pl.pallas_call(
        _gdn_a_kernel,
        grid=(b, s // TM),
        in_specs=[pl.BlockSpec((1, TM, nh * GDN_DK), lambda bi, t: (bi, t, 0)),
                  gate_spec,
                  pl.BlockSpec((2, 1, cpt, GDN_PAIRS, 2 * CHUNK), lambda bi, t: (0, bi, t, 0, 0)),
                  gate_spec],
        out_specs=pl.BlockSpec((2, 1, cpt, GDN_PAIRS, CHUNK, 2 * CHUNK), lambda bi, t: (0, bi, t, 0, 0, 0)),
        out_shape=jax.ShapeDtypeStruct((2, b, s // CHUNK, GDN_PAIRS, CHUNK, 2 * CHUNK), F32),
        compiler_params=_params("parallel", "parallel"),
        name="gdn_a",
    )(k32, gc, gctp, bt)


def _substitute_rows(at_ref, x_ref, hh):
    n = CHUNK
    zero = jnp.zeros((8, INV_LANES), F32)
    for i in range(n):
        nb = (i + 7) // 8
        acc = [-at_ref[hh, pl.ds(i * n + jb * 8, 8), :] for jb in range(nb)]
        for m in range(1, i):
            a_im = jnp.broadcast_to(at_ref[hh, pl.ds(i * n + m, 1), :], (8, INV_LANES))
            for jb in range((m + 7) // 8):
                acc[jb] = acc[jb] - a_im * x_ref[hh, pl.ds(m * n + jb * 8, 8), :]
        for jb in range(n // 8):
            x_ref[hh, pl.ds(i * n + jb * 8, 8), :] = acc[jb] if jb < nb else zero


def _gdn_inv_kernel(a_ref, t_ref, at_ref, x_ref):
    fwd = pl.program_id(0) == 0
    n = CHUNK

    def load(r, transposed):
        slab = a_ref[0, pl.ds(r, INV_LANES, stride=n), :].T
        for hh in range(2):
            dst = pl.ds(r, n, stride=n) if transposed else pl.ds(r * n, n)
            at_ref[hh, dst, :] = slab[hh * n:(hh + 1) * n]

    def store(r, transposed):
        src = pl.ds(r, n, stride=n) if transposed else pl.ds(r * n, n)
        eye = jnp.where(lax.broadcasted_iota(jnp.int32, (n, INV_LANES), 0) == r, 1.0, 0.0)
        slab = jnp.concatenate([x_ref[0, src, :] + eye, x_ref[1, src, :] + eye], axis=0)
        t_ref[0, pl.ds(r, INV_LANES, stride=n), :] = slab.T

    def rows(fn, transposed):
        for r in range(n):
            fn(r, transposed)

    @pl.when(fwd)
    def _():
        rows(load, False)

    @pl.when(jnp.logical_not(fwd))
    def _():
        rows(load, True)

    def halves(hh, carry):
        _substitute_rows(at_ref, x_ref, hh)
        return carry

    lax.fori_loop(0, 2, halves, 0)

    @pl.when(fwd)
    def _():
        rows(store, False)

    @pl.when(jnp.logical_not(fwd))
    def _():
        rows(store, True)


def _gdn_inv(a):
    shape = a.shape
    n_sys = shape[1] * shape[2] * shape[3]
    assert n_sys % INV_LANES == 0
    rows_per_step = INV_LANES * CHUNK
    t = pl.pallas_call(
        _gdn_inv_kernel,
        grid=(2, n_sys // INV_LANES),
        in_specs=[pl.BlockSpec((1, rows_per_step, 2 * CHUNK), lambda d, g: (d, g, 0))],
        out_specs=pl.BlockSpec((1, rows_per_step, 2 * CHUNK), lambda d, g: (d, g, 0)),
        out_shape=jax.ShapeDtypeStruct((2, n_sys * CHUNK, 2 * CHUNK), F32),
        scratch_shapes=[pltpu.VMEM((2, CHUNK * CHUNK, INV_LANES), F32), pltpu.VMEM((2, CHUNK * CHUNK, INV_LANES), F32)],
        compiler_params=_params("parallel", "parallel"),
        name="gdn_inv",
    )(a.reshape(2, n_sys * CHUNK, 2 * CHUNK))
    return t.reshape(shape)


def _gdn_uw_kernel(t_ref, q_ref, k_ref, v_ref, gc_ref, gctp_ref, bt_ref, u_ref, w_ref, qk_ref):
    for ch in range(TM // CHUNK):
        rows = pl.ds(ch * CHUNK, CHUNK)
        for hp in range(GDN_PAIRS):
            cols = slice(hp * GDN_PAIR, (hp + 1) * GDN_PAIR)
            kp = k_ref[0, rows, cols]
            vp = v_ref[0, rows, cols]
            qk = _dot_nt(q_ref[0, rows, cols], _pair_blockdiag_rows(kp, GDN_DK))
            for d in range(2):
                gc = gc_ref[d, 0, rows, :]
                beta = _pair_cols(bt_ref[d, 0, rows, :], hp, GDN_PAIR)
                rhs = jnp.concatenate([vp * beta, kp * (beta * jnp.exp(_pair_cols(gc, hp, GDN_PAIR)))], axis=1)
                uw = _dot(t_ref[d, 0, ch, hp], _pair_blockdiag_rows(rhs, GDN_DK))
                u_ref[d, 0, rows, cols] = uw[:, :GDN_PAIR]
                w_ref[d, 0, rows, cols] = uw[:, GDN_PAIR:].astype(BF16)
                decay = _pair_decay(gc, gctp_ref[d, 0, ch, hp:hp + 1, :], hp, d == 0, True)
                qk_ref[d, 0, ch, hp] = (qk * decay).astype(BF16)


def _gdn_uw(t, qkv, gc, gctp, bt):
    b, s, _ = qkv.shape
    nh = GDN_HEADS
    width = nh * GDN_DK
    cpt = TM // CHUNK
    gate_spec = pl.BlockSpec((2, 1, TM, nh), lambda bi, t_: (0, bi, t_, 0))
    sys_spec = pl.BlockSpec((2, 1, cpt, GDN_PAIRS, CHUNK, 2 * CHUNK), lambda bi, t_: (0, bi, t_, 0, 0, 0))
    tok_spec = pl.BlockSpec((2, 1, TM, width), lambda bi, t_: (0, bi, t_, 0))
    return pl.pallas_call(
        _gdn_uw_kernel,
        grid=(b, s // TM),
        in_specs=[sys_spec,
                  pl.BlockSpec((1, TM, width), lambda bi, t_: (bi, t_, 0)),
                  pl.BlockSpec((1, TM, width), lambda bi, t_: (bi, t_, 1)),
                  pl.BlockSpec((1, TM, width), lambda bi, t_: (bi, t_, 2)),
                  gate_spec,
                  pl.BlockSpec((2, 1, cpt, GDN_PAIRS, 2 * CHUNK), lambda bi, t_: (0, bi, t_, 0, 0)),
                  gate_spec],
        out_specs=[tok_spec, tok_spec, sys_spec],
        out_shape=[jax.ShapeDtypeStruct((2, b, s, width), F32), jax.ShapeDtypeStruct((2, b, s, width), BF16),
                   jax.ShapeDtypeStruct((2, b, s // CHUNK, GDN_PAIRS, CHUNK, 2 * CHUNK), BF16)],
        compiler_params=_params("parallel", "parallel"),
        name="gdn_uw",
    )(t, qkv, qkv, qkv, gc, gctp, bt)


def _gdn_scan_kernel(qf_ref, kf_ref, uf_ref, wf_ref, qkf_ref, gcf_ref, qb_ref, kb_ref, ub_ref, wb_ref, qkb_ref, gcb_ref,
                     of_ref, ob_ref, s_ref):
    @pl.when(pl.program_id(1) == 0)
    def _():
        s_ref[...] = jnp.zeros_like(s_ref)

    dk = GDN_DK
    zeros = jnp.zeros((dk, dk), BF16)
    directions = ((qf_ref, kf_ref, uf_ref, wf_ref, qkf_ref, gcf_ref, of_ref),
                  (qb_ref, kb_ref, ub_ref, wb_ref, qkb_ref, gcb_ref, ob_ref))
    chains = [(bb, d, hp) for bb in range(SCAN_BATCH) for d in range(2) for hp in range(GDN_PAIRS)]
    stage1 = []
    for bb, d, hp in chains:
        q_ref, k_ref, u_ref, w_ref, qk_ref, gc_ref, o_ref = directions[d]
        cols = slice(hp * GDN_PAIR, (hp + 1) * GDN_PAIR)
        gc_all = gc_ref[0, bb]
        glast = gc_all[CHUNK - 1:CHUNK] if d == 0 else gc_all[0:1]
        gcp = _pair_cols(gc_all, hp, GDN_PAIR)
        q_in = q_ref[bb, :, cols] * jnp.exp(gcp)
        k_out = (k_ref[bb, :, cols] * jnp.exp(_pair_cols(glast, hp, GDN_PAIR) - gcp)).astype(BF16)
        sa = s_ref[bb, d, 2 * hp]
        sb = s_ref[bb, d, 2 * hp + 1]
        s_bd = jnp.concatenate([jnp.concatenate([sa.astype(BF16), zeros], axis=1),
                                jnp.concatenate([zeros, sb.astype(BF16)], axis=1)], axis=0)
        both = jnp.dot(jnp.concatenate([w_ref[0, bb, :, cols], q_in.astype(BF16)], axis=0), s_bd,
                       preferred_element_type=F32)
        stage1.append((both, k_out, jnp.exp(glast)))
    stage2 = []
    for (bb, d, hp), (both, k_out, _) in zip(chains, stage1):
        q_ref, k_ref, u_ref, w_ref, qk_ref, gc_ref, o_ref = directions[d]
        cols = slice(hp * GDN_PAIR, (hp + 1) * GDN_PAIR)
        v_new = u_ref[0, bb, :, cols] - both[:CHUNK]
        intra = jnp.dot(qk_ref[0, bb, 0, hp], _pair_blockdiag_rows(v_new, dk).astype(BF16), preferred_element_type=F32)
        upd = _dot_tn(k_out, v_new)
        stage2.append((intra, upd))
    for (bb, d, hp), (both, _, chunk_decay), (intra, upd) in zip(chains, stage1, stage2):
        o_ref = directions[d][6]
        cols = slice(hp * GDN_PAIR, (hp + 1) * GDN_PAIR)
        o_ref[bb, :, cols] = both[CHUNK:] + intra
        s_ref[bb, d, 2 * hp] = s_ref[bb, d, 2 * hp] * chunk_decay[:, 2 * hp:2 * hp + 1] + upd[:dk, :dk]
        s_ref[bb, d, 2 * hp + 1] = s_ref[bb, d, 2 * hp + 1] * chunk_decay[:, 2 * hp + 1:2 * hp + 2] + upd[dk:, dk:]


def _scan_tile(d, step, n_tiles, ctx_tiles):
    back = jnp.where(step < ctx_tiles, ctx_tiles - 1 - step, n_tiles + ctx_tiles - 1 - step)
    return jnp.where(d == 0, step, back)


def _gdn_scan(qkv, u, w, qk, gc):
    b, s, _ = qkv.shape
    nh, dk = GDN_HEADS, GDN_DK
    nc = s // CHUNK
    width = nh * dk
    tile = functools.partial(_scan_tile, n_tiles=nc, ctx_tiles=TM // CHUNK)

    nb = SCAN_BATCH
    assert b % nb == 0

    def specs(d):
        return [pl.BlockSpec((nb, CHUNK, width), lambda bi, i: (bi, tile(d, i), 0)),
                pl.BlockSpec((nb, CHUNK, width), lambda bi, i: (bi, tile(d, i), 1)),
                pl.BlockSpec((1, nb, CHUNK, width), lambda bi, i: (d, bi, tile(d, i), 0)),
                pl.BlockSpec((1, nb, CHUNK, width), lambda bi, i: (d, bi, tile(d, i), 0)),
                pl.BlockSpec((1, nb, 1, GDN_PAIRS, CHUNK, 2 * CHUNK), lambda bi, i: (d, bi, tile(d, i), 0, 0, 0)),
                pl.BlockSpec((1, nb, CHUNK, nh), lambda bi, i: (d, bi, tile(d, i), 0))]

    out = jax.ShapeDtypeStruct((b, s, width), F32)
    return pl.pallas_call(
        _gdn_scan_kernel,
        grid=(b // nb, nc),
        in_specs=specs(0) + specs(1),
        out_specs=[pl.BlockSpec((nb, CHUNK, width), lambda bi, i: (bi, tile(0, i), 0)),
                   pl.BlockSpec((nb, CHUNK, width), lambda bi, i: (bi, tile(1, i), 0))],
        out_shape=[out, out],
        scratch_shapes=[pltpu.VMEM((nb, 2, nh, dk, dk), F32)],
        compiler_params=_params("parallel", "arbitrary"),
        name="gdn_scan",
    )(qkv, qkv, u, w, qk, gc, qkv, qkv, u, w, qk, gc)


def _gdn_out_kernel(of_ref, ob_ref, z_ref, x_ref, mod_ref, ng_ref, w_ref, out_ref):
    m = mod_ref[0, 0]
    o = of_ref[0] + ob_ref[0]
    ng = ng_ref[...]
    parts = []
    for h in range(GDN_HEADS):
        seg = o[:, h * GDN_DK:(h + 1) * GDN_DK]
        parts.append(seg * lax.rsqrt(jnp.mean(seg * seg, axis=-1, keepdims=True) + EPS) * ng)
    y = _dot(jnp.concatenate(parts, axis=-1) * _silu(z_ref[0].astype(F32)), w_ref[...])
    out_ref[0] = x_ref[0] + m[2:3] * y


def _gdn_out(o_f, o_b, p, x, modv, norm_g, w_out_bf16):
    b, s, d = x.shape
    width = GDN_HEADS * GDN_DK
    return pl.pallas_call(
        _gdn_out_kernel,
        grid=(b, s // TM),
        in_specs=[pl.BlockSpec((1, TM, width), lambda bi, t: (bi, t, 0)),
                  pl.BlockSpec((1, TM, width), lambda bi, t: (bi, t, 0)),
                  pl.BlockSpec((1, TM, width), lambda bi, t: (bi, t, 3)),
                  pl.BlockSpec((1, TM, d), lambda bi, t: (bi, t, 0)),
                  pl.BlockSpec((1, 1, 6, d), _mod_spec()),
                  pl.BlockSpec((1, GDN_DK), lambda bi, t: (0, 0)),
                  pl.BlockSpec((width, d), lambda bi, t: (0, 0))],
        out_specs=pl.BlockSpec((1, TM, d), lambda bi, t: (bi, t, 0)),
        out_shape=jax.ShapeDtypeStruct((b, s, d), F32),
        compiler_params=_params("parallel", "parallel"),
        name="gdn_out",
    )(o_f, o_b, p, x, modv, norm_g.reshape(1, GDN_DK), w_out_bf16)


def _mixer_gdn(x, modv, norm_g, w_in, conv_w, a_log, dt_bias, out_norm_g, w_out):
    nh, dk = GDN_HEADS, GDN_DK
    n_main = 4 * nh * dk
    p = _in_proj(x, modv, norm_g, w_in[:, :n_main].astype(BF16), n_main)
    gc, bt = _gdn_gates(x, modv, norm_g, w_in[:, n_main:], a_log, dt_bias)
    b, s, _ = x.shape
    gctp = gc.reshape(2, b, s // CHUNK, CHUNK, GDN_PAIRS, 2).transpose(0, 1, 2, 4, 5, 3)
    gctp = gctp.reshape(2, b, s // CHUNK, GDN_PAIRS, 2 * CHUNK)
    qkv, k32 = _gdn_conv(p, conv_w)
    t = _gdn_inv(_gdn_a(k32, gc, gctp, bt))
    u, w, qk = _gdn_uw(t, qkv, gc, gctp, bt)
    o_f, o_b = _gdn_scan(qkv, u, w, qk, gc)
    return _gdn_out(o_f, o_b, p, x, modv, out_norm_g, w_out.astype(BF16))


def _pool_kernel(x_ref, xp_ref, xn_ref, mod_ref, g_ref, w_ref, b_ref, sc_ref, o_ref, ext_ref):
    t = pl.program_id(1)
    nt = pl.num_programs(1)
    prev_ok, next_ok = _halo_valid()
    m = mod_ref[0, 0]
    g = g_ref[...]
    x = x_ref[0]
    h = _rms_mod(x, g, m[0:1], m[1:2])
    ext_ref[0:HALO] = jnp.where(prev_ok, _rms_mod(xp_ref[0], g, m[0:1], m[1:2]), 0.0)
    ext_ref[HALO:HALO + TM] = h
    ext_ref[HALO + TM:] = jnp.where(next_ok, _rms_mod(xn_ref[0], g, m[0:1], m[1:2]), 0.0)
    row = lax.broadcasted_iota(jnp.int32, (TM, 1), 0)
    pos = row + jnp.where(t == 0, 0, (t - 1) * TM)
    n_seq = jnp.where(t == 0, TM, (nt - 1) * TM)
    pg = POOL_GROUP
    for gi, win in enumerate(POOL_WINDOWS):
        lo_off = HALO - win // 2
        acc = ext_ref[lo_off:lo_off + TM, gi * pg:(gi + 1) * pg]
        for k in range(1, win):
            acc = acc + ext_ref[lo_off + k:lo_off + k + TM, gi * pg:(gi + 1) * pg]
        lo = jnp.clip(pos - win // 2, 0, n_seq)
        hi = jnp.clip(pos + win - win // 2, 0, n_seq)
        pooled = acc / (hi - lo).astype(F32) - h[:, gi * pg:(gi + 1) * pg]
        y = (_dot(pooled, w_ref[gi]) + b_ref[gi]) * sc_ref[:, gi * pg:(gi + 1) * pg]
        o_ref[0, :, gi * pg:(gi + 1) * pg] = x[:, gi * pg:(gi + 1) * pg] + m[2:3, gi * pg:(gi + 1) * pg] * y


def _mixer_pool(x, modv, norm_g, w_group, b_group, scale):
    b, s, d = x.shape
    ng, pg = len(POOL_WINDOWS), POOL_GROUP
    prev_spec, next_spec = _halo_specs(d, lambda: 0, s // TM)
    return pl.pallas_call(
        _pool_kernel,
        grid=(b, s // TM),
        in_specs=[pl.BlockSpec((1, TM, d), lambda bi, t: (bi, t, 0)),
                  prev_spec, next_spec,
                  pl.BlockSpec((1, 1, 6, d), _mod_spec()),
                  pl.BlockSpec((1, d), lambda bi, t: (0, 0)),
                  pl.BlockSpec((ng, pg, pg), lambda bi, t: (0, 0, 0)),
                  pl.BlockSpec((ng, 1, pg), lambda bi, t: (0, 0, 0)),
                  pl.BlockSpec((1, d), lambda bi, t: (0, 0))],
        out_specs=pl.BlockSpec((1, TM, d), lambda bi, t: (bi, t, 0)),
        out_shape=jax.ShapeDtypeStruct((b, s, d), F32),
        scratch_shapes=[pltpu.VMEM((TM + 2 * HALO, d), F32)],
        compiler_params=_params("parallel", "parallel"),
        name="pool_mixer",
    )(x, x, x, modv, norm_g.reshape(1, d), w_group.astype(BF16), b_group.reshape(ng, 1, pg), scale.reshape(1, d))


def _rotate(t, cos, sin_signed):
    half = RET_DK // 2
    swapped = jnp.concatenate([pltpu.roll(t[:, :half], half // 2, 1), pltpu.roll(t[:, half:], half // 2, 1)], axis=-1)
    return t * cos + swapped * sin_signed


def _ret_scan_kernel(lg_ref, q_ref, k_ref, v_ref, cos_ref, sin_ref, o_ref, s_ref):
    d = pl.program_id(1)
    step = pl.program_id(2)

    @pl.when(step == 0)
    def _():
        s_ref[...] = jnp.zeros_like(s_ref)

    fwd = d == 0
    r = lax.broadcasted_iota(jnp.int32, (TM, TM), 0)
    c = lax.broadcasted_iota(jnp.int32, (TM, TM), 1)
    rel = jnp.where(fwd, r - c, c - r).astype(F32)
    row = lax.broadcasted_iota(jnp.int32, (TM, 1), 0)
    q_pow = jnp.where(fwd, row + 1, TM - row).astype(F32)
    k_pow = jnp.where(fwd, TM - 1 - row, row).astype(F32)
    cos = cos_ref[...]
    sin = sin_ref[...]
    dk, dv = RET_DK, RET_DV
    for h in range(RET_HEADS):
        lg = jnp.full((1, 1), lg_ref[d, h], F32)
        q = _rotate(q_ref[0, :, h * dk:(h + 1) * dk].astype(F32), cos, sin)
        k = _rotate(k_ref[0, :, h * dk:(h + 1) * dk].astype(F32) * (dk ** -0.5), cos, sin)
        v = v_ref[0, :, h * dv:(h + 1) * dv]
        dmat = jnp.where(rel >= 0, jnp.exp(jnp.maximum(rel, 0.0) * lg), 0.0)
        inner = _dot_nt(q, k) * dmat
        s = s_ref[h]
        o_ref[0, 0, :, h * dv:(h + 1) * dv] = _dot(inner, v) + _dot(q * jnp.exp(q_pow * lg), s)
        s_ref[h] = s * jnp.exp(TM * lg) + _dot_tn(k * jnp.exp(k_pow * lg), v)


def _ret_scan(p, log_gamma, cos, sin):
    b, s, _ = p.shape
    nt = s // TM
    qw, vw = RET_HEADS * RET_DK, RET_HEADS * RET_DV
    tile = functools.partial(_scan_tile, n_tiles=nt, ctx_tiles=1)
    return pl.pallas_call(
        _ret_scan_kernel,
        grid=(b, 2, nt),
        in_specs=[pl.BlockSpec(memory_space=pltpu.SMEM),
                  pl.BlockSpec((1, TM, qw), lambda bi, d, i: (bi, tile(d, i), 0)),
                  pl.BlockSpec((1, TM, qw), lambda bi, d, i: (bi, tile(d, i), 1)),
                  pl.BlockSpec((1, TM, vw), lambda bi, d, i: (bi, tile(d, i), 1)),
                  pl.BlockSpec((TM, RET_DK), lambda bi, d, i: (tile(d, i), 0)),
                  pl.BlockSpec((TM, RET_DK), lambda bi, d, i: (tile(d, i), 0))],
        out_specs=pl.BlockSpec((1, 1, TM, vw), lambda bi, d, i: (d, bi, tile(d, i), 0)),
        out_shape=jax.ShapeDtypeStruct((2, b, s, vw), F32),
        scratch_shapes=[pltpu.VMEM((RET_HEADS, RET_DK, RET_DV), F32)],
        compiler_params=_params("parallel", "parallel", "arbitrary"),
        name="ret_scan",
    )(log_gamma, p, p, p, cos, sin)


def _ret_out_kernel(o_ref, gate_ref, x_ref, mod_ref, ng_ref, w_ref, out_ref):
    m = mod_ref[0, 0]
    o = o_ref[0, 0] + o_ref[1, 0]
    ng = ng_ref[...]
    parts = []
    for h in range(RET_HEADS):
        seg = o[:, h * RET_DV:(h + 1) * RET_DV]
        mu = jnp.mean(seg, axis=-1, keepdims=True)
        cen = seg - mu
        var = jnp.mean(cen * cen, axis=-1, keepdims=True)
        parts.append(cen * lax.rsqrt(var + EPS) * ng)
    y = _dot(_silu(gate_ref[0].astype(F32)) * jnp.concatenate(parts, axis=-1), w_ref[...])
    out_ref[0] = x_ref[0] + m[2:3] * y


def _ret_out(o, p, x, modv, norm_g, w_out_bf16):
    b, s, d = x.shape
    vw = RET_HEADS * RET_DV
    return pl.pallas_call(
        _ret_out_kernel,
        grid=(b, s // TM),
        in_specs=[pl.BlockSpec((2, 1, TM, vw), lambda bi, t: (0, bi, t, 0)),
                  pl.BlockSpec((1, TM, vw), lambda bi, t: (bi, t, 2)),
                  pl.BlockSpec((1, TM, d), lambda bi, t: (bi, t, 0)),
                  pl.BlockSpec((1, 1, 6, d), _mod_spec()),
                  pl.BlockSpec((1, RET_DV), lambda bi, t: (0, 0)),
                  pl.BlockSpec((vw, d), lambda bi, t: (0, 0))],
        out_specs=pl.BlockSpec((1, TM, d), lambda bi, t: (bi, t, 0)),
        out_shape=jax.ShapeDtypeStruct((b, s, d), F32),
        compiler_params=_params("parallel", "parallel"),
        name="ret_out",
    )(o, p, x, modv, norm_g.reshape(1, RET_DV), w_out_bf16)


def _rotary_tables(s):
    n_lat = s - TM
    pos = jnp.arange(n_lat, dtype=jnp.int32)
    rows = (pos // GRID_W).astype(F32)
    cols = (pos % GRID_W).astype(F32)
    quarter = RET_DK // 4
    inv_freq = ROPE_BASE ** (-jnp.arange(quarter, dtype=F32) / quarter)
    ang_r = rows[:, None] * inv_freq[None, :]
    ang_c = cols[:, None] * inv_freq[None, :]
    cos = jnp.concatenate([jnp.cos(ang_r), jnp.cos(ang_r), jnp.cos(ang_c), jnp.cos(ang_c)], axis=-1)
    sin = jnp.concatenate([-jnp.sin(ang_r), jnp.sin(ang_r), -jnp.sin(ang_c), jnp.sin(ang_c)], axis=-1)
    cos = jnp.concatenate([jnp.ones((TM, RET_DK), F32), cos], axis=0)
    sin = jnp.concatenate([jnp.zeros((TM, RET_DK), F32), sin], axis=0)
    return cos, sin


def _mixer_retention(x, modv, norm_g, w_in, decay_logit, out_norm_g, w_out):
    p = _in_proj(x, modv, norm_g, w_in.astype(BF16), w_in.shape[1] // 2)
    cos, sin = _rotary_tables(x.shape[1])
    o = _ret_scan(p, jax.nn.log_sigmoid(decay_logit.astype(F32)), cos, sin)
    return _ret_out(o, p, x, modv, out_norm_g, w_out.astype(BF16))


ROUTE_LANES = 128
ROUTE_COLS = 8
ISSUE_UNROLL = 8


def _route_kernel(x_ref, mod_ref, g_ref, w_ref, b_ref, rt_ref, cnt_out_ref, cnt_ref):
    first = (pl.program_id(0) == 0) & (pl.program_id(1) == 0)

    @pl.when(first)
    def _():
        cnt_ref[...] = jnp.zeros_like(cnt_ref)

    m = mod_ref[0, 0]
    h = _rms_mod(x_ref[0], g_ref[...], m[3:4], m[4:5])
    logits = _dot3(h, w_ref[...]) + b_ref[...]
    lane = lax.broadcasted_iota(jnp.int32, (TM, ROUTE_LANES), 1)
    big = jnp.int32(ROUTE_LANES)
    neg = -jnp.inf
    glog = jnp.where((lane >= N_EXPERTS) & (lane < N_EXPERTS + N_GROUPS), logits, neg)
    gmax = jnp.max(glog, axis=-1, keepdims=True)
    gsel = jnp.min(jnp.where(glog == gmax, lane, big), axis=-1, keepdims=True) - N_EXPERTS
    p_group = 1.0 / jnp.sum(jnp.exp(glog - gmax), axis=-1, keepdims=True)
    elog = jnp.where((lane >= gsel * EXPERTS_PER_GROUP) & (lane < (gsel + 1) * EXPERTS_PER_GROUP), logits, neg)
    m1 = jnp.max(elog, axis=-1, keepdims=True)
    i1 = jnp.min(jnp.where(elog == m1, lane, big), axis=-1, keepdims=True)
    elog2 = jnp.where(lane == i1, neg, elog)
    m2 = jnp.max(elog2, axis=-1, keepdims=True)
    i2 = jnp.min(jnp.where(elog2 == m2, lane, big), axis=-1, keepdims=True)
    e2 = jnp.exp(m2 - m1)
    w1 = p_group / (1.0 + e2)
    w2 = p_group * e2 / (1.0 + e2)
    onehot = jnp.where((lane == i1) | (lane == i2), 1.0, 0.0)
    r = lax.broadcasted_iota(jnp.int32, (TM, TM), 0)
    c = lax.broadcasted_iota(jnp.int32, (TM, TM), 1)
    before = _dot(jnp.where(c < r, 1.0, 0.0), onehot) + cnt_ref[...]
    r1 = jnp.sum(jnp.where(lane == i1, before, 0.0), axis=-1, keepdims=True)
    r2 = jnp.sum(jnp.where(lane == i2, before, 0.0), axis=-1, keepdims=True)
    cnt_ref[...] = cnt_ref[...] + jnp.sum(onehot, axis=0, keepdims=True)
    cnt_out_ref[...] = cnt_ref[...]
    vals = (i1.astype(F32), i2.astype(F32), w1, w2, r1, r2)
    out = jnp.zeros((TM, ROUTE_LANES), F32)
    for pos_, val in enumerate(vals):
        out = jnp.where(lane == pos_, val, out)
    rt_ref[0] = out[:, :ROUTE_COLS]


def _route(x, modv, norm_g, w_route, b_route):
    b, s, d = x.shape
    return pl.pallas_call(
        _route_kernel,
        grid=(b, s // TM),
        in_specs=[pl.BlockSpec((1, TM, d), lambda bi, t: (bi, t, 0)),
                  pl.BlockSpec((1, 1, 6, d), _mod_spec()),
                  pl.BlockSpec((1, d), lambda bi, t: (0, 0)),
                  pl.BlockSpec((d, ROUTE_LANES), lambda bi, t: (0, 0)),
                  pl.BlockSpec((1, ROUTE_LANES), lambda bi, t: (0, 0))],
        out_specs=[pl.BlockSpec((1, TM, ROUTE_COLS), lambda bi, t: (bi, t, 0)),
                   pl.BlockSpec((1, ROUTE_LANES), lambda bi, t: (0, 0))],
        out_shape=[jax.ShapeDtypeStruct((b, s, ROUTE_COLS), F32), jax.ShapeDtypeStruct((1, ROUTE_LANES), F32)],
        scratch_shapes=[pltpu.VMEM((1, ROUTE_LANES), F32)],
        compiler_params=_params("arbitrary", "arbitrary"),
        name="moe_route",
    )(x, modv, norm_g.reshape(1, d), w_route, b_route)


def _issue_row_copies(make_copy):
    def body(i, carry):
        for k in range(2):
            make_copy(i, k).start()
        return carry

    lax.fori_loop(0, TM, body, 0, unroll=ISSUE_UNROLL)


def _tile_step():
    step = pl.program_id(0) * pl.num_programs(1) + pl.program_id(1)
    return step, pl.num_programs(0) * pl.num_programs(1), jnp.bitwise_and(step, 1)


def _pack_bf16_pairs(h):
    bits = pltpu.bitcast(h.astype(BF16).astype(F32), jnp.uint32)
    half = h.shape[1] // 2
    return jnp.bitwise_or(jnp.right_shift(bits[:, :half], jnp.uint32(16)),
                          jnp.bitwise_and(bits[:, half:], jnp.uint32(0xFFFF0000)))


def _unpack_bf16_pairs(w):
    lo = pltpu.bitcast(jnp.left_shift(w, jnp.uint32(16)), F32)
    hi = pltpu.bitcast(jnp.bitwise_and(w, jnp.uint32(0xFFFF0000)), F32)
    return jnp.concatenate([lo.astype(BF16), hi.astype(BF16)], axis=1)


def _dispatch_kernel(dest_ref, pe_ref, x_ref, mod_ref, g_ref, xb_ref, h_ref, z_ref, sem, zsem):
    step, n_steps, slot = _tile_step()

    @pl.when(step == 0)
    def _():
        z_ref[...] = jnp.zeros_like(z_ref)

        def clear(e):
            start = pl.multiple_of(pe_ref[e] - MOE_BLOCK, MOE_BLOCK)
            return pltpu.make_async_copy(z_ref, xb_ref.at[pl.ds(start, MOE_BLOCK)], zsem.at[e])

        def used(e):
            return pe_ref[e] > (pe_ref[e - 1] if e else 0)

        for e in range(N_EXPERTS):
            @pl.when(used(e))
            def _():
                clear(e).start()
        for e in range(N_EXPERTS):
            @pl.when(used(e))
            def _():
                clear(e).wait()

        def clear_unused(blk, carry):
            cp = pltpu.make_async_copy(z_ref, xb_ref.at[pl.ds(pl.multiple_of(blk * MOE_BLOCK, MOE_BLOCK), MOE_BLOCK)],
                                       zsem.at[0])
            cp.start()
            cp.wait()
            return carry

        lax.fori_loop(_block_of(pe_ref[N_EXPERTS - 1], MOE_BLOCK), xb_ref.shape[0] // MOE_BLOCK, clear_unused, 0)

    m = mod_ref[0, 0]
    h_ref[slot] = _pack_bf16_pairs(_rms_mod(x_ref[0], g_ref[...], m[3:4], m[4:5]))
    _issue_row_copies(lambda i, k: pltpu.make_async_copy(
        h_ref.at[slot, pl.ds(i, 1)], xb_ref.at[pl.ds(dest_ref[0, 0, 2 * i + k], 1)], sem.at[slot]))

    def wait_tile(sl):
        for _ in range(2):
            pltpu.make_async_copy(h_ref.at[sl], xb_ref.at[pl.ds(0, TM)], sem.at[sl]).wait()

    @pl.when(step > 0)
    def _():
        wait_tile(1 - slot)

    @pl.when(step == n_steps - 1)
    def _():
        wait_tile(slot)


def _dispatch(x, modv, norm_g, dest, pad_end, n_rows):
    b, s, d = x.shape
    nt = s // TM
    return pl.pallas_call(
        _dispatch_kernel,
        grid=(b, nt),
        in_specs=[pl.BlockSpec((1, 1, 2 * TM), lambda bi, t: (bi * nt + t, 0, 0), memory_space=pltpu.SMEM),
                  pl.BlockSpec(memory_space=pltpu.SMEM),
                  pl.BlockSpec((1, TM, d), lambda bi, t: (bi, t, 0)),
                  pl.BlockSpec((1, 1, 6, d), _mod_spec()),
                  pl.BlockSpec((1, d), lambda bi, t: (0, 0))],
        out_specs=pl.BlockSpec(memory_space=pl.ANY),
        out_shape=jax.ShapeDtypeStruct((n_rows, d // 2), jnp.uint32),
        scratch_shapes=[pltpu.VMEM((2, TM, d // 2), jnp.uint32), pltpu.VMEM((MOE_BLOCK, d // 2), jnp.uint32),
                        pltpu.SemaphoreType.DMA((2,)), pltpu.SemaphoreType.DMA((N_EXPERTS,))],
        compiler_params=_params("arbitrary", "arbitrary"),
        name="moe_dispatch",
    )(dest.reshape(b * nt, 1, 2 * TM), pad_end, x, modv, norm_g.reshape(1, d))


def _expert_kernel(be_ref, nu_ref, x_ref, wg_ref, wu_ref, wd_ref, o_ref, wg_bf, wu_bf, wd_bf):
    i = pl.program_id(0)

    @pl.when(i < nu_ref[0])
    def _():
        @pl.when((i == 0) | (be_ref[i] != be_ref[jnp.maximum(i - 1, 0)]))
        def _():
            wg_bf[...] = wg_ref[0, 0].astype(BF16)
            wu_bf[...] = wu_ref[0, 0].astype(BF16)
            wd_bf[...] = wd_ref[0, 0].astype(BF16)

        xb = _unpack_bf16_pairs(x_ref[...])
        act = _silu(jnp.dot(xb, wg_bf[...], preferred_element_type=F32)) * jnp.dot(xb, wu_bf[...], preferred_element_type=F32)
        o_ref[...] = jnp.dot(act.astype(BF16), wd_bf[...], preferred_element_type=F32)

    @pl.when(i >= nu_ref[0])
    def _():
        o_ref[...] = jnp.zeros_like(o_ref)


def _experts(xb, block_e, n_used, layer, w_gate, w_up, w_down):
    n_rows = xb.shape[0]
    d, de = w_gate.shape[-2:]
    nb = n_rows // MOE_BLOCK

    def last_used(i, nu):
        return jnp.minimum(i, nu[0] - 1)

    grid_spec = pltpu.PrefetchScalarGridSpec(
        num_scalar_prefetch=2,
        grid=(nb,),
        in_specs=[pl.BlockSpec((MOE_BLOCK, d // 2), lambda i, be, nu: (last_used(i, nu), 0)),
                  pl.BlockSpec((1, 1, d, de), lambda i, be, nu: (layer, be[last_used(i, nu)], 0, 0)),
                  pl.BlockSpec((1, 1, d, de), lambda i, be, nu: (layer, be[last_used(i, nu)], 0, 0)),
                  pl.BlockSpec((1, 1, de, d), lambda i, be, nu: (layer, be[last_used(i, nu)], 0, 0))],
        out_specs=pl.BlockSpec((MOE_BLOCK, d), lambda i, be, nu: (i, 0)),
        scratch_shapes=[pltpu.VMEM((d, de), BF16), pltpu.VMEM((d, de), BF16), pltpu.VMEM((de, d), BF16)],
    )
    return pl.pallas_call(
        _expert_kernel,
        grid_spec=grid_spec,
        out_shape=jax.ShapeDtypeStruct((n_rows, d), F32),
        compiler_params=_params("arbitrary"),
        name="moe_experts",
    )(block_e, n_used, xb, w_gate, w_up, w_down)


def _combine_kernel(dest_ref, next_dest_ref, x_ref, mod_ref, rt_ref, fg_ref, yb_ref, o_ref, y_ref, sem, *, final_norm):
    step, n_steps, slot = _tile_step()

    def gather(d_ref, sl):
        _issue_row_copies(lambda i, k: pltpu.make_async_copy(
            yb_ref.at[pl.ds(d_ref[0, 0, 2 * i + k], 1)], y_ref.at[sl, pl.ds(k * TM + i, 1)], sem.at[sl]))

    @pl.when(step == 0)
    def _():
        gather(dest_ref, 0)

    @pl.when(step + 1 < n_steps)
    def _():
        gather(next_dest_ref, 1 - slot)

    pltpu.make_async_copy(yb_ref.at[pl.ds(0, 2 * TM)], y_ref.at[slot], sem.at[slot]).wait()
    m = mod_ref[0, 0]
    rt = rt_ref[0]
    out = x_ref[0] + m[5:6] * (rt[:, 2:3] * y_ref[slot, 0:TM] + rt[:, 3:4] * y_ref[slot, TM:2 * TM])
    if final_norm:
        out = out * lax.rsqrt(jnp.mean(out * out, axis=-1, keepdims=True) + EPS) * fg_ref[...]
    o_ref[0] = out


def _combine(x, modv, rt, dest, yb, final_g, final_norm):
    b, s, d = x.shape
    nt = s // TM
    skip = 1 if final_norm else 0

    def next_tile(bi, t):
        wrap = t + 1 >= nt - skip
        nb_ = jnp.minimum(jnp.where(wrap, bi + 1, bi), b - 1)
        return (nb_ * nt + jnp.where(wrap, 0, t + 1) + skip, 0, 0)

    return pl.pallas_call(
        functools.partial(_combine_kernel, final_norm=final_norm),
        grid=(b, nt - skip),
        in_specs=[pl.BlockSpec((1, 1, 2 * TM), lambda bi, t: (bi * nt + t + skip, 0, 0), memory_space=pltpu.SMEM),
                  pl.BlockSpec((1, 1, 2 * TM), next_tile, memory_space=pltpu.SMEM),
                  pl.BlockSpec((1, TM, d), lambda bi, t: (bi, t + skip, 0)),
                  pl.BlockSpec((1, 1, 6, d), lambda bi, t: (bi, jnp.minimum(t + skip, 1), 0, 0)),
                  pl.BlockSpec((1, TM, ROUTE_COLS), lambda bi, t: (bi, t + skip, 0)),
                  pl.BlockSpec((1, d), lambda bi, t: (0, 0)),
                  pl.BlockSpec(memory_space=pl.ANY)],
        out_specs=pl.BlockSpec((1, TM, d), lambda bi, t: (bi, t, 0)),
        out_shape=jax.ShapeDtypeStruct((b, s - skip * TM, d), F32),
        scratch_shapes=[pltpu.VMEM((2, 2 * TM, d), F32), pltpu.SemaphoreType.DMA((2,))],
        compiler_params=_params("arbitrary", "arbitrary"),
        name="moe_combine",
    )(dest.reshape(b * nt, 1, 2 * TM), dest.reshape(b * nt, 1, 2 * TM), x, modv, rt, final_g.reshape(1, d), yb)


def _hier_moe(x, modv, norm_g, layer, wg_r, bg_r, we_r, be_r, w_gate, w_up, w_down, final_g, final_norm):
    b, s, d = x.shape
    n_tok = b * s
    pad = ROUTE_LANES - N_EXPERTS - N_GROUPS
    w_route = jnp.concatenate([we_r, wg_r, jnp.zeros((d, pad), F32)], axis=1)
    b_route = jnp.concatenate([be_r, bg_r, jnp.zeros((pad,), F32)]).reshape(1, ROUTE_LANES)
    rt, cnt = _route(x, modv, norm_g, w_route, b_route)
    counts = cnt[0, :N_EXPERTS].astype(jnp.int32)
    padded = (counts + MOE_BLOCK - 1) // MOE_BLOCK * MOE_BLOCK
    pad_end = jnp.cumsum(padded)
    pad_start = pad_end - padded
    n_blocks = -(-(2 * n_tok + N_EXPERTS * (MOE_BLOCK - 1)) // MOE_BLOCK)
    rt2 = rt.reshape(n_tok, ROUTE_COLS)
    expert = rt2[:, 0:2].astype(jnp.int32)
    experts = jnp.arange(N_EXPERTS, dtype=jnp.int32)
    start_of = jnp.sum(jnp.where(expert[..., None] == experts, pad_start, 0), axis=-1)
    dest = (start_of + rt2[:, 4:6].astype(jnp.int32)).reshape(-1)
    block_start = jnp.arange(n_blocks, dtype=jnp.int32) * MOE_BLOCK
    block_e = jnp.minimum(jnp.sum((pad_end[None, :] <= block_start[:, None]).astype(jnp.int32), axis=1), N_EXPERTS - 1)
    n_used = (pad_end[-1:] // MOE_BLOCK).astype(jnp.int32)
    xb = _dispatch(x, modv, norm_g, dest, pad_end.astype(jnp.int32), n_blocks * MOE_BLOCK)
    yb = _experts(xb, block_e, n_used, layer, w_gate, w_up, w_down)
    return _combine(x, modv, rt, dest, yb, final_g, final_norm)


def kernel(x, c, ctx, c_ctx, mod_w, mod_b, norm1_g, norm2_g, final_g, gdn_w_in, gdn_conv_w, gdn_a_log, gdn_dt_bias, gdn_norm_g, gdn_w_out, pool_w, pool_b, pool_scale, ret_w_in, ret_decay_logit, ret_norm_g, ret_w_out, router_group_w, router_group_b, router_expert_w, router_expert_b, exp_w_gate, exp_w_up, exp_w_down):
    b, n_lat, d = x.shape
    depth = mod_w.shape[0]
    assert ctx.shape[1] == TM and n_lat % TM == 0 and b < 16
    xs = jnp.concatenate([ctx, x], axis=1)
    cvec = jnp.concatenate([c, c_ctx[None], jnp.zeros((15 - b, d), F32)], axis=0)
    mods = _modulation(cvec, mod_w, mod_b)
    for i in range(depth):
        j, kind = i // 3, i % 3
        lat = mods[i, :b].reshape(b, 1, 6, d)
        con = jnp.broadcast_to(mods[i, b].reshape(1, 1, 6, d), (b, 1, 6, d))
        modv = jnp.concatenate([con, lat], axis=1)
        if kind == 0:
            xs = _mixer_gdn(xs, modv, norm1_g[i], gdn_w_in[j], gdn_conv_w[j], gdn_a_log[j], gdn_dt_bias[j],
                            gdn_norm_g[j], gdn_w_out[j])
        elif kind == 1:
            xs = _mixer_pool(xs, modv, norm1_g[i], pool_w[j], pool_b[j], pool_scale[j])
        else:
            xs = _mixer_retention(xs, modv, norm1_g[i], ret_w_in[j], ret_decay_logit[j], ret_norm_g[j], ret_w_out[j])
        xs = _hier_moe(xs, modv, norm2_g[i], i, router_group_w[i], router_group_b[i], router_expert_w[i],
                       router_expert_b[i], exp_w_gate, exp_w_up, exp_w_down, final_g, i == depth - 1)
    return xs
```

```python
import functools
import math

import jax
import jax.numpy as jnp
from jax import lax
from jax.experimental import pallas as pl
from jax.experimental.pallas import tpu as pltpu

F32 = jnp.float32
BF16 = jnp.bfloat16
HIGHEST = lax.Precision.HIGHEST

EPS = 1e-6
TM = 256
HALO = 8
HALO_BF16 = 16
CHUNK = 64
GDN_HEADS = 8
GDN_DK = 128
GDN_CONV = 5
RET_HEADS = 4
RET_DK = 256
RET_DV = 512
ROPE_BASE = 10000.0
GRID_W = 64
POOL_WINDOWS = (2, 4, 8, 16)
POOL_GROUP = 256
N_GROUPS = 4
EXPERTS_PER_GROUP = 8
N_EXPERTS = 32
MOE_BLOCK = 512
V7X_VMEM_LIMIT_BYTES = 56 * 1024 * 1024


def _params(*sem):
    return pltpu.CompilerParams(dimension_semantics=sem, vmem_limit_bytes=V7X_VMEM_LIMIT_BYTES)


def _dot(a, b):
    return jnp.dot(a.astype(BF16), b.astype(BF16), preferred_element_type=F32)


def _dot_hi(a, b):
    return jnp.dot(a, b, precision=HIGHEST, preferred_element_type=F32)


def _dot_nt(a, b):
    return lax.dot_general(a.astype(BF16), b.astype(BF16), (((1,), (1,)), ((), ())), preferred_element_type=F32)


def _split(x):
    hi = x.astype(BF16)
    return hi, (x - hi.astype(F32)).astype(BF16)


def _dot3_parts(ah, al, bh, bl, dims):
    def d(p, q):
        return lax.dot_general(p, q, dims, preferred_element_type=F32)
    return d(ah, bh) + (d(ah, bl) + d(al, bh))


def _dot3(a, b):
    return _dot3_parts(*_split(a), *_split(b), (((1,), (0,)), ((), ())))


def _cumsum_dot(ones_mask, x):
    x1 = x.astype(BF16)
    r1 = x - x1.astype(F32)
    x2 = r1.astype(BF16)
    x3 = (r1 - x2.astype(F32)).astype(BF16)
    m = ones_mask.astype(BF16)
    return (jnp.dot(m, x1, preferred_element_type=F32) + jnp.dot(m, x2, preferred_element_type=F32)
            + jnp.dot(m, x3, preferred_element_type=F32))


def _dot_tn(a, b):
    return lax.dot_general(a.astype(BF16), b.astype(BF16), (((0,), (0,)), ((), ())), preferred_element_type=F32)


def _block_of(i, size):
    return jnp.right_shift(i, int(math.log2(size)))


def _silu(x):
    return x * jax.nn.sigmoid(x)


def _softplus(x):
    return jnp.maximum(x, 0.0) + jnp.log(1.0 + jnp.exp(-jnp.abs(x)))


def _rms_mod(x, g, shift, scale):
    y = x * lax.rsqrt(jnp.mean(x * x, axis=-1, keepdims=True) + EPS) * g
    return y * (1.0 + scale) + shift


def _mod_spec(grid_rank_prefix=0):
    def idx(*g):
        b, t = g[grid_rank_prefix], g[grid_rank_prefix + 1]
        return (b, jnp.minimum(t, 1), 0, 0)
    return idx


def _mod_kernel(c_ref, w_ref, b_ref, o_ref):
    o_ref[0] = _dot_hi(_silu(c_ref[...]), w_ref[0]) + b_ref[0]


def _modulation(cvec, mod_w, mod_b):
    n_layers, d, d6 = mod_w.shape
    return pl.pallas_call(
        _mod_kernel,
        grid=(n_layers, d6 // d),
        in_specs=[pl.BlockSpec((16, d), lambda l, j: (0, 0)),
                  pl.BlockSpec((1, d, d), lambda l, j: (l, 0, j)),
                  pl.BlockSpec((1, 1, d), lambda l, j: (l, 0, j))],
        out_specs=pl.BlockSpec((1, 16, d), lambda l, j: (l, 0, j)),
        out_shape=jax.ShapeDtypeStruct((n_layers, 16, d6), F32),
        compiler_params=_params("parallel", "parallel"),
        name="modulation",
    )(cvec, mod_w, mod_b.reshape(n_layers, 1, d6))


def _in_kernel(x_ref, mod_ref, g_ref, w_ref, o_ref):
    m = mod_ref[0, 0]
    h = _rms_mod(x_ref[0], g_ref[...], m[0:1], m[1:2])
    o_ref[0] = _dot(h, w_ref[...]).astype(o_ref.dtype)


def _in_proj(x, modv, g, w_bf16, tn):
    b, s, d = x.shape
    n = w_bf16.shape[1]
    return pl.pallas_call(
        _in_kernel,
        grid=(n // tn, b, s // TM),
        in_specs=[pl.BlockSpec((1, TM, d), lambda j, bi, t: (bi, t, 0)),
                  pl.BlockSpec((1, 1, 6, d), _mod_spec(1)),
                  pl.BlockSpec((1, d), lambda j, bi, t: (0, 0)),
                  pl.BlockSpec((d, tn), lambda j, bi, t: (0, j))],
        out_specs=pl.BlockSpec((1, TM, tn), lambda j, bi, t: (bi, t, j)),
        out_shape=jax.ShapeDtypeStruct((b, s, n), BF16),
        compiler_params=_params("parallel", "parallel", "parallel"),
        name="in_proj",
    )(x, modv, g.reshape(1, d), w_bf16)


def _gates_kernel(x_ref, mod_ref, g_ref, wab_ref, alog_ref, dtb_ref, gc_ref, bt_ref):
    m = mod_ref[0, 0]
    h = _rms_mod(x_ref[0], g_ref[...], m[0:1], m[1:2])
    ab = _dot3(h, wab_ref[...])
    nh = GDN_HEADS
    gate = -jnp.exp(alog_ref[...]) * _softplus(ab[:, :2 * nh] + dtb_ref[...])
    beta = jax.nn.sigmoid(ab[:, 2 * nh:])
    r = lax.broadcasted_iota(jnp.int32, (TM, TM), 0)
    c = lax.broadcasted_iota(jnp.int32, (TM, TM), 1)
    same = _block_of(r, CHUNK) == _block_of(c, CHUNK)
    cum_f = jnp.where(same & (c <= r), 1.0, 0.0)
    cum_b = jnp.where(same & (c >= r), 1.0, 0.0)
    gc_ref[0, 0] = _cumsum_dot(cum_f, gate[:, :nh])
    gc_ref[1, 0] = _cumsum_dot(cum_b, gate[:, nh:])
    bt_ref[0, 0] = beta[:, :nh]
    bt_ref[1, 0] = beta[:, nh:]


def _gdn_gates(x, modv, g, w_ab, a_log, dt_bias):
    b, s, d = x.shape
    nh = GDN_HEADS
    out = jax.ShapeDtypeStruct((2, b, s, nh), F32)
    return pl.pallas_call(
        _gates_kernel,
        grid=(b, s // TM),
        in_specs=[pl.BlockSpec((1, TM, d), lambda bi, t: (bi, t, 0)),
                  pl.BlockSpec((1, 1, 6, d), _mod_spec()),
                  pl.BlockSpec((1, d), lambda bi, t: (0, 0)),
                  pl.BlockSpec((d, 4 * nh), lambda bi, t: (0, 0)),
                  pl.BlockSpec((1, 2 * nh), lambda bi, t: (0, 0)),
                  pl.BlockSpec((1, 2 * nh), lambda bi, t: (0, 0))],
        out_specs=[pl.BlockSpec((2, 1, TM, nh), lambda bi, t: (0, bi, t, 0)),
                   pl.BlockSpec((2, 1, TM, nh), lambda bi, t: (0, bi, t, 0))],
        out_shape=[out, out],
        compiler_params=_params("parallel", "parallel"),
        name="gdn_gates",
    )(x, modv, g.reshape(1, d), w_ab, a_log.reshape(1, 2 * nh), dt_bias.reshape(1, 2 * nh))


def _halo_specs(width, col_of, n_tiles, halo=HALO):
    per = TM // halo

    def prev(bi, t, *rest):
        return (bi, jnp.maximum(t * per - 1, 0), col_of(*rest))

    def nxt(bi, t, *rest):
        return (bi, jnp.minimum((t + 1) * per, n_tiles * per - 1), col_of(*rest))

    return pl.BlockSpec((1, halo, width), prev), pl.BlockSpec((1, halo, width), nxt)


def _halo_valid():
    t = pl.program_id(1)
    nt = pl.num_programs(1)
    return t >= 2, (t >= 1) & (t < nt - 1)


def _conv_kernel(cur_ref, prev_ref, next_ref, w_ref, o_ref, k32_ref, ext_ref):
    j = pl.program_id(2)
    prev_ok, next_ok = _halo_valid()
    halo = HALO_BF16
    ext_ref[0:halo] = jnp.where(prev_ok, prev_ref[0].astype(F32), 0.0)
    ext_ref[halo:halo + TM] = cur_ref[0].astype(F32)
    ext_ref[halo + TM:] = jnp.where(next_ok, next_ref[0].astype(F32), 0.0)
    w = w_ref[...]
    base = halo - GDN_CONV // 2
    acc = w[0:1] * ext_ref[base:base + TM]
    for k in range(1, GDN_CONV):
        acc = acc + w[k:k + 1] * ext_ref[base + k:base + k + TM]
    y = _silu(acc)
    width = y.shape[1]
    is_v = j >= 2 * (GDN_HEADS * GDN_DK // width)
    is_q = j < (GDN_HEADS * GDN_DK // width)
    qscale = jnp.where(is_q, GDN_DK ** -0.5, 1.0)
    vals = []
    for hh in range(width // GDN_DK):
        seg = y[:, hh * GDN_DK:(hh + 1) * GDN_DK]
        nrm = seg * lax.rsqrt(jnp.sum(seg * seg, axis=-1, keepdims=True) + EPS) * qscale
        vals.append(jnp.where(is_v, seg, nrm))
        o_ref[0, :, hh * GDN_DK:(hh + 1) * GDN_DK] = vals[-1].astype(o_ref.dtype)

    @pl.when(j == 1)
    def _():
        for hh, val in enumerate(vals):
            k32_ref[0, :, hh * GDN_DK:(hh + 1) * GDN_DK] = val


def _gdn_conv(p, conv_w):
    b, s, _ = p.shape
    n = conv_w.shape[1]
    width = GDN_HEADS * GDN_DK
    prev_spec, next_spec = _halo_specs(width, lambda j: j, s // TM, HALO_BF16)
    return pl.pallas_call(
        _conv_kernel,
        grid=(b, s // TM, n // width),
        in_specs=[pl.BlockSpec((1, TM, width), lambda bi, t, j: (bi, t, j)),
                  prev_spec, next_spec,
                  pl.BlockSpec((GDN_CONV, width), lambda bi, t, j: (0, j))],
        out_specs=[pl.BlockSpec((1, TM, width), lambda bi, t, j: (bi, t, j)),
                   pl.BlockSpec((1, TM, width), lambda bi, t, j: (bi, t, 0))],
        out_shape=[jax.ShapeDtypeStruct((b, s, n), BF16), jax.ShapeDtypeStruct((b, s, width), F32)],
        scratch_shapes=[pltpu.VMEM((TM + 2 * HALO_BF16, width), F32)],
        compiler_params=_params("parallel", "parallel", "arbitrary"),
        name="gdn_conv",
    )(p, p, p, conv_w)


GDN_PAIR = 2 * GDN_DK
GDN_PAIRS = GDN_HEADS // 2
INV_LANES = 128
SCAN_BATCH = 4


def _pair_cols(cols, hp, width):
    lane = lax.broadcasted_iota(jnp.int32, (cols.shape[0], width), 1)
    return jnp.where(lane < width // 2, cols[:, 2 * hp:2 * hp + 1], cols[:, 2 * hp + 1:2 * hp + 2])


def _pair_blockdiag_rows(x, lane_block):
    n, w = x.shape
    r = lax.broadcasted_iota(jnp.int32, (2 * n, w), 0)
    c = lax.broadcasted_iota(jnp.int32, (2 * n, w), 1)
    same = _block_of(r, n) == jnp.bitwise_and(_block_of(c, lane_block), 1)
    return jnp.where(same, jnp.concatenate([x, x], axis=0), 0.0)


def _pair_decay(gc_cols, gct_row, hp, fwd, inclusive):
    r = lax.broadcasted_iota(jnp.int32, (CHUNK, 2 * CHUNK), 0)
    c = jnp.bitwise_and(lax.broadcasted_iota(jnp.int32, (CHUNK, 2 * CHUNK), 1), CHUNK - 1)
    ahead = (r - c) if fwd else (c - r)
    keep = (ahead >= 0) if inclusive else (ahead > 0)
    rel = _pair_cols(gc_cols, hp, 2 * CHUNK) - gct_row
    return jnp.where(keep, jnp.exp(jnp.where(keep, rel, 0.0)), 0.0)


def _gdn_a_kernel(k_ref, gc_ref, gctp_ref, bt_ref, a_ref):
    for ch in range(TM // CHUNK):
        rows = pl.ds(ch * CHUNK, CHUNK)
        for hp in range(GDN_PAIRS):
            kh, kl = _split(k_ref[0, rows, hp * GDN_PAIR:(hp + 1) * GDN_PAIR])
            kk = _dot3_parts(kh, kl, _pair_blockdiag_rows(kh, GDN_DK), _pair_blockdiag_rows(kl, GDN_DK),
                             (((1,), (1,)), ((), ())))
            for d in range(2):
                decay = _pair_decay(gc_ref[d, 0, rows, :], gctp_ref[d, 0, ch, hp:hp + 1, :], hp, d == 0, False)
                a_ref[d, 0, ch, hp] = _pair_cols(bt_ref[d, 0, rows, :], hp, 2 * CHUNK) * kk * decay


def _gdn_a(k32, gc, gctp, bt):
    b, s, _ = k32.shape
    nh = GDN_HEADS
    cpt = TM // CHUNK
    gate_spec = pl.BlockSpec((2, 1, TM, nh), lambda bi, t: (0, bi, t, 0))
    return pl.pallas_call(
        _gdn_a_kernel,
        grid=(b, s // TM),
        in_specs=[pl.BlockSpec((1, TM, nh * GDN_DK), lambda bi, t: (bi, t, 0)),
                  gate_spec,
                  pl.BlockSpec((2, 1, cpt, GDN_PAIRS, 2 * CHUNK), lambda bi, t: (0, bi, t, 0, 0)),
                  gate_spec],
        out_specs=pl.BlockSpec((2, 1, cpt, GDN_PAIRS, CHUNK, 2 * CHUNK), lambda bi, t: (0, bi, t, 0, 0, 0)),
        out_shape=jax.ShapeDtypeStruct((2, b, s // CHUNK, GDN_PAIRS, CHUNK, 2 * CHUNK), F32),
        compiler_params=_params("parallel", "parallel"),
        name="gdn_a",
    )(k32, gc, gctp, bt)


def _substitute_rows(at_ref, x_ref, hh):
    n = CHUNK
    zero = jnp.zeros((8, INV_LANES), F32)
    for i in range(n):
        nb = (i + 7) // 8
        acc = [-at_ref[hh, pl.ds(i * n + jb * 8, 8), :] for jb in range(nb)]
        for m in range(1, i):
            a_im = jnp.broadcast_to(at_ref[hh, pl.ds(i * n + m, 1), :], (8, INV_LANES))
            for jb in range((m + 7) // 8):
                acc[jb] = acc[jb] - a_im * x_ref[hh, pl.ds(m * n + jb * 8, 8), :]
        for jb in range(n // 8):
            x_ref[hh, pl.ds(i * n + jb * 8, 8), :] = acc[jb] if jb < nb else zero


def _gdn_inv_kernel(a_ref, t_ref, at_ref, x_ref):
    fwd = pl.program_id(0) == 0
    n = CHUNK

    def load(r, transposed):
        slab = a_ref[0, pl.ds(r, INV_LANES, stride=n), :].T
        for hh in range(2):
            dst = pl.ds(r, n, stride=n) if transposed else pl.ds(r * n, n)
            at_ref[hh, dst, :] = slab[hh * n:(hh + 1) * n]

    def store(r, transposed):
        src = pl.ds(r, n, stride=n) if transposed else pl.ds(r * n, n)
        eye = jnp.where(lax.broadcasted_iota(jnp.int32, (n, INV_LANES), 0) == r, 1.0, 0.0)
        slab = jnp.concatenate([x_ref[0, src, :] + eye, x_ref[1, src, :] + eye], axis=0)
        t_ref[0, pl.ds(r, INV_LANES, stride=n), :] = slab.T

    def rows(fn, transposed):
        for r in range(n):
            fn(r, transposed)

    @pl.when(fwd)
    def _():
        rows(load, False)

    @pl.when(jnp.logical_not(fwd))
    def _():
        rows(load, True)

    def halves(hh, carry):
        _substitute_rows(at_ref, x_ref, hh)
        return carry

    lax.fori_loop(0, 2, halves, 0)

    @pl.when(fwd)
    def _():
        rows(store, False)

    @pl.when(jnp.logical_not(fwd))
    def _():
        rows(store, True)


def _gdn_inv(a):
    shape = a.shape
    n_sys = shape[1] * shape[2] * shape[3]
    assert n_sys % INV_LANES == 0
    rows_per_step = INV_LANES * CHUNK
    t = pl.pallas_call(
        _gdn_inv_kernel,
        grid=(2, n_sys // INV_LANES),
        in_specs=[pl.BlockSpec((1, rows_per_step, 2 * CHUNK), lambda d, g: (d, g, 0))],
        out_specs=pl.BlockSpec((1, rows_per_step, 2 * CHUNK), lambda d, g: (d, g, 0)),
        out_shape=jax.ShapeDtypeStruct((2, n_sys * CHUNK, 2 * CHUNK), F32),
        scratch_shapes=[pltpu.VMEM((2, CHUNK * CHUNK, INV_LANES), F32), pltpu.VMEM((2, CHUNK * CHUNK, INV_LANES), F32)],
        compiler_params=_params("parallel", "parallel"),
        name="gdn_inv",
    )(a.reshape(2, n_sys * CHUNK, 2 * CHUNK))
    return t.reshape(shape)


def _gdn_uw_kernel(t_ref, q_ref, k_ref, v_ref, gc_ref, gctp_ref, bt_ref, u_ref, w_ref, qk_ref):
    for ch in range(TM // CHUNK):
        rows = pl.ds(ch * CHUNK, CHUNK)
        for hp in range(GDN_PAIRS):
            cols = slice(hp * GDN_PAIR, (hp + 1) * GDN_PAIR)
            kp = k_ref[0, rows, cols]
            vp = v_ref[0, rows, cols]
            qk = _dot_nt(q_ref[0, rows, cols], _pair_blockdiag_rows(kp, GDN_DK))
            for d in range(2):
                gc = gc_ref[d, 0, rows, :]
                beta = _pair_cols(bt_ref[d, 0, rows, :], hp, GDN_PAIR)
                rhs = jnp.concatenate([vp * beta, kp * (beta * jnp.exp(_pair_cols(gc, hp, GDN_PAIR)))], axis=1)
                uw = _dot(t_ref[d, 0, ch, hp], _pair_blockdiag_rows(rhs, GDN_DK))
                u_ref[d, 0, rows, cols] = uw[:, :GDN_PAIR]
                w_ref[d, 0, rows, cols] = uw[:, GDN_PAIR:].astype(BF16)
                decay = _pair_decay(gc, gctp_ref[d, 0, ch, hp:hp + 1, :], hp, d == 0, True)
                qk_ref[d, 0, ch, hp] = (qk * decay).astype(BF16)


def _gdn_uw(t, qkv, gc, gctp, bt):
    b, s, _ = qkv.shape
    nh = GDN_HEADS
    width = nh * GDN_DK
    cpt = TM // CHUNK
    gate_spec = pl.BlockSpec((2, 1, TM, nh), lambda bi, t_: (0, bi, t_, 0))
    sys_spec = pl.BlockSpec((2, 1, cpt, GDN_PAIRS, CHUNK, 2 * CHUNK), lambda bi, t_: (0, bi, t_, 0, 0, 0))
    tok_spec = pl.BlockSpec((2, 1, TM, width), lambda bi, t_: (0, bi, t_, 0))
    return pl.pallas_call(
        _gdn_uw_kernel,
        grid=(b, s // TM),
        in_specs=[sys_spec,
                  pl.BlockSpec((1, TM, width), lambda bi, t_: (bi, t_, 0)),
                  pl.BlockSpec((1, TM, width), lambda bi, t_: (bi, t_, 1)),
                  pl.BlockSpec((1, TM, width), lambda bi, t_: (bi, t_, 2)),
                  gate_spec,
                  pl.BlockSpec((2, 1, cpt, GDN_PAIRS, 2 * CHUNK), lambda bi, t_: (0, bi, t_, 0, 0)),
                  gate_spec],
        out_specs=[tok_spec, tok_spec, sys_spec],
        out_shape=[jax.ShapeDtypeStruct((2, b, s, width), F32), jax.ShapeDtypeStruct((2, b, s, width), BF16),
                   jax.ShapeDtypeStruct((2, b, s // CHUNK, GDN_PAIRS, CHUNK, 2 * CHUNK), BF16)],
        compiler_params=_params("parallel", "parallel"),
        name="gdn_uw",
    )(t, qkv, qkv, qkv, gc, gctp, bt)


def _gdn_scan_kernel(qf_ref, kf_ref, uf_ref, wf_ref, qkf_ref, gcf_ref, qb_ref, kb_ref, ub_ref, wb_ref, qkb_ref, gcb_ref,
                     of_ref, ob_ref, s_ref):
    @pl.when(pl.program_id(1) == 0)
    def _():
        s_ref[...] = jnp.zeros_like(s_ref)

    dk = GDN_DK
    zeros = jnp.zeros((dk, dk), BF16)
    directions = ((qf_ref, kf_ref, uf_ref, wf_ref, qkf_ref, gcf_ref, of_ref),
                  (qb_ref, kb_ref, ub_ref, wb_ref, qkb_ref, gcb_ref, ob_ref))
    chains = [(bb, d, hp) for bb in range(SCAN_BATCH) for d in range(2) for hp in range(GDN_PAIRS)]
    stage1 = []
    for bb, d, hp in chains:
        q_ref, k_ref, u_ref, w_ref, qk_ref, gc_ref, o_ref = directions[d]
        cols = slice(hp * GDN_PAIR, (hp + 1) * GDN_PAIR)
        gc_all = gc_ref[0, bb]
        glast = gc_all[CHUNK - 1:CHUNK] if d == 0 else gc_all[0:1]
        gcp = _pair_cols(gc_all, hp, GDN_PAIR)
        q_in = q_ref[bb, :, cols] * jnp.exp(gcp)
        k_out = (k_ref[bb, :, cols] * jnp.exp(_pair_cols(glast, hp, GDN_PAIR) - gcp)).astype(BF16)
        sa = s_ref[bb, d, 2 * hp]
        sb = s_ref[bb, d, 2 * hp + 1]
        s_bd = jnp.concatenate([jnp.concatenate([sa.astype(BF16), zeros], axis=1),
                                jnp.concatenate([zeros, sb.astype(BF16)], axis=1)], axis=0)
        both = jnp.dot(jnp.concatenate([w_ref[0, bb, :, cols], q_in.astype(BF16)], axis=0), s_bd,
                       preferred_element_type=F32)
        stage1.append((both, k_out, jnp.exp(glast)))
    stage2 = []
    for (bb, d, hp), (both, k_out, _) in zip(chains, stage1):
        q_ref, k_ref, u_ref, w_ref, qk_ref, gc_ref, o_ref = directions[d]
        cols = slice(hp * GDN_PAIR, (hp + 1) * GDN_PAIR)
        v_new = u_ref[0, bb, :, cols] - both[:CHUNK]
        intra = jnp.dot(qk_ref[0, bb, 0, hp], _pair_blockdiag_rows(v_new, dk).astype(BF16), preferred_element_type=F32)
        upd = _dot_tn(k_out, v_new)
        stage2.append((intra, upd))
    for (bb, d, hp), (both, _, chunk_decay), (intra, upd) in zip(chains, stage1, stage2):
        o_ref = directions[d][6]
        cols = slice(hp * GDN_PAIR, (hp + 1) * GDN_PAIR)
        o_ref[bb, :, cols] = both[CHUNK:] + intra
        s_ref[bb, d, 2 * hp] = s_ref[bb, d, 2 * hp] * chunk_decay[:, 2 * hp:2 * hp + 1] + upd[:dk, :dk]
        s_ref[bb, d, 2 * hp + 1] = s_ref[bb, d, 2 * hp + 1] * chunk_decay[:, 2 * hp + 1:2 * hp + 2] + upd[dk:, dk:]


def _scan_tile(d, step, n_tiles, ctx_tiles):
    back = jnp.where(step < ctx_tiles, ctx_tiles - 1 - step, n_tiles + ctx_tiles - 1 - step)
    return jnp.where(d == 0, step, back)


def _gdn_scan(qkv, u, w, qk, gc):
    b, s, _ = qkv.shape
    nh, dk = GDN_HEADS, GDN_DK
    nc = s // CHUNK
    width = nh * dk
    tile = functools.partial(_scan_tile, n_tiles=nc, ctx_tiles=TM // CHUNK)

    nb = SCAN_BATCH
    assert b % nb == 0

    def specs(d):
        return [pl.BlockSpec((nb, CHUNK, width), lambda bi, i: (bi, tile(d, i), 0)),
                pl.BlockSpec((nb, CHUNK, width), lambda bi, i: (bi, tile(d, i), 1)),
                pl.BlockSpec((1, nb, CHUNK, width), lambda bi, i: (d, bi, tile(d, i), 0)),
                pl.BlockSpec((1, nb, CHUNK, width), lambda bi, i: (d, bi, tile(d, i), 0)),
                pl.BlockSpec((1, nb, 1, GDN_PAIRS, CHUNK, 2 * CHUNK), lambda bi, i: (d, bi, tile(d, i), 0, 0, 0)),
                pl.BlockSpec((1, nb, CHUNK, nh), lambda bi, i: (d, bi, tile(d, i), 0))]

    out = jax.ShapeDtypeStruct((b, s, width), F32)
    return pl.pallas_call(
        _gdn_scan_kernel,
        grid=(b // nb, nc),
        in_specs=specs(0) + specs(1),
        out_specs=[pl.BlockSpec((nb, CHUNK, width), lambda bi, i: (bi, tile(0, i), 0)),
                   pl.BlockSpec((nb, CHUNK, width), lambda bi, i: (bi, tile(1, i), 0))],
        out_shape=[out, out],
        scratch_shapes=[pltpu.VMEM((nb, 2, nh, dk, dk), F32)],
        compiler_params=_params("parallel", "arbitrary"),
        name="gdn_scan",
    )(qkv, qkv, u, w, qk, gc, qkv, qkv, u, w, qk, gc)


def _gdn_out_kernel(of_ref, ob_ref, z_ref, x_ref, mod_ref, ng_ref, w_ref, out_ref):
    m = mod_ref[0, 0]
    o = of_ref[0] + ob_ref[0]
    ng = ng_ref[...]
    parts = []
    for h in range(GDN_HEADS):
        seg = o[:, h * GDN_DK:(h + 1) * GDN_DK]
        parts.append(seg * lax.rsqrt(jnp.mean(seg * seg, axis=-1, keepdims=True) + EPS) * ng)
    y = _dot(jnp.concatenate(parts, axis=-1) * _silu(z_ref[0].astype(F32)), w_ref[...])
    out_ref[0] = x_ref[0] + m[2:3] * y


def _gdn_out(o_f, o_b, p, x, modv, norm_g, w_out_bf16):
    b, s, d = x.shape
    width = GDN_HEADS * GDN_DK
    return pl.pallas_call(
        _gdn_out_kernel,
        grid=(b, s // TM),
        in_specs=[pl.BlockSpec((1, TM, width), lambda bi, t: (bi, t, 0)),
                  pl.BlockSpec((1, TM, width), lambda bi, t: (bi, t, 0)),
                  pl.BlockSpec((1, TM, width), lambda bi, t: (bi, t, 3)),
                  pl.BlockSpec((1, TM, d), lambda bi, t: (bi, t, 0)),
                  pl.BlockSpec((1, 1, 6, d), _mod_spec()),
                  pl.BlockSpec((1, GDN_DK), lambda bi, t: (0, 0)),
                  pl.BlockSpec((width, d), lambda bi, t: (0, 0))],
        out_specs=pl.BlockSpec((1, TM, d), lambda bi, t: (bi, t, 0)),
        out_shape=jax.ShapeDtypeStruct((b, s, d), F32),
        compiler_params=_params("parallel", "parallel"),
        name="gdn_out",
    )(o_f, o_b, p, x, modv, norm_g.reshape(1, GDN_DK), w_out_bf16)


def _mixer_gdn(x, modv, norm_g, w_in, conv_w, a_log, dt_bias, out_norm_g, w_out):
    nh, dk = GDN_HEADS, GDN_DK
    n_main = 4 * nh * dk
    p = _in_proj(x, modv, norm_g, w_in[:, :n_main].astype(BF16), n_main)
    gc, bt = _gdn_gates(x, modv, norm_g, w_in[:, n_main:], a_log, dt_bias)
    b, s, _ = x.shape
    gctp = gc.reshape(2, b, s // CHUNK, CHUNK, GDN_PAIRS, 2).transpose(0, 1, 2, 4, 5, 3)
    gctp = gctp.reshape(2, b, s // CHUNK, GDN_PAIRS, 2 * CHUNK)
    qkv, k32 = _gdn_conv(p, conv_w)
    t = _gdn_inv(_gdn_a(k32, gc, gctp, bt))
    u, w, qk = _gdn_uw(t, qkv, gc, gctp, bt)
    o_f, o_b = _gdn_scan(qkv, u, w, qk, gc)
    return _gdn_out(o_f, o_b, p, x, modv, out_norm_g, w_out.astype(BF16))


def _pool_kernel(x_ref, xp_ref, xn_ref, mod_ref, g_ref, w_ref, b_ref, sc_ref, o_ref, ext_ref):
    t = pl.program_id(1)
    nt = pl.num_programs(1)
    prev_ok, next_ok = _halo_valid()
    m = mod_ref[0, 0]
    g = g_ref[...]
    x = x_ref[0]
    h = _rms_mod(x, g, m[0:1], m[1:2])
    ext_ref[0:HALO] = jnp.where(prev_ok, _rms_mod(xp_ref[0], g, m[0:1], m[1:2]), 0.0)
    ext_ref[HALO:HALO + TM] = h
    ext_ref[HALO + TM:] = jnp.where(next_ok, _rms_mod(xn_ref[0], g, m[0:1], m[1:2]), 0.0)
    row = lax.broadcasted_iota(jnp.int32, (TM, 1), 0)
    pos = row + jnp.where(t == 0, 0, (t - 1) * TM)
    n_seq = jnp.where(t == 0, TM, (nt - 1) * TM)
    pg = POOL_GROUP
    for gi, win in enumerate(POOL_WINDOWS):
        lo_off = HALO - win // 2
        acc = ext_ref[lo_off:lo_off + TM, gi * pg:(gi + 1) * pg]
        for k in range(1, win):
            acc = acc + ext_ref[lo_off + k:lo_off + k + TM, gi * pg:(gi + 1) * pg]
        lo = jnp.clip(pos - win // 2, 0, n_seq)
        hi = jnp.clip(pos + win - win // 2, 0, n_seq)
        pooled = acc / (hi - lo).astype(F32) - h[:, gi * pg:(gi + 1) * pg]
        y = (_dot(pooled, w_ref[gi]) + b_ref[gi]) * sc_ref[:, gi * pg:(gi + 1) * pg]
        o_ref[0, :, gi * pg:(gi + 1) * pg] = x[:, gi * pg:(gi + 1) * pg] + m[2:3, gi * pg:(gi + 1) * pg] * y


def _mixer_pool(x, modv, norm_g, w_group, b_group, scale):
    b, s, d = x.shape
    ng, pg = len(POOL_WINDOWS), POOL_GROUP
    prev_spec, next_spec = _halo_specs(d, lambda: 0, s // TM)
    return pl.pallas_call(
        _pool_kernel,
        grid=(b, s // TM),
        in_specs=[pl.BlockSpec((1, TM, d), lambda bi, t: (bi, t, 0)),
                  prev_spec, next_spec,
                  pl.BlockSpec((1, 1, 6, d), _mod_spec()),
                  pl.BlockSpec((1, d), lambda bi, t: (0, 0)),
                  pl.BlockSpec((ng, pg, pg), lambda bi, t: (0, 0, 0)),
                  pl.BlockSpec((ng, 1, pg), lambda bi, t: (0, 0, 0)),
                  pl.BlockSpec((1, d), lambda bi, t: (0, 0))],
        out_specs=pl.BlockSpec((1, TM, d), lambda bi, t: (bi, t, 0)),
        out_shape=jax.ShapeDtypeStruct((b, s, d), F32),
        scratch_shapes=[pltpu.VMEM((TM + 2 * HALO, d), F32)],
        compiler_params=_params("parallel", "parallel"),
        name="pool_mixer",
    )(x, x, x, modv, norm_g.reshape(1, d), w_group.astype(BF16), b_group.reshape(ng, 1, pg), scale.reshape(1, d))


def _rotate(t, cos, sin_signed):
    half = RET_DK // 2
    swapped = jnp.concatenate([pltpu.roll(t[:, :half], half // 2, 1), pltpu.roll(t[:, half:], half // 2, 1)], axis=-1)
    return t * cos + swapped * sin_signed


def _ret_scan_kernel(lg_ref, q_ref, k_ref, v_ref, cos_ref, sin_ref, o_ref, s_ref):
    d = pl.program_id(1)
    step = pl.program_id(2)

    @pl.when(step == 0)
    def _():
        s_ref[...] = jnp.zeros_like(s_ref)

    fwd = d == 0
    r = lax.broadcasted_iota(jnp.int32, (TM, TM), 0)
    c = lax.broadcasted_iota(jnp.int32, (TM, TM), 1)
    rel = jnp.where(fwd, r - c, c - r).astype(F32)
    row = lax.broadcasted_iota(jnp.int32, (TM, 1), 0)
    q_pow = jnp.where(fwd, row + 1, TM - row).astype(F32)
    k_pow = jnp.where(fwd, TM - 1 - row, row).astype(F32)
    cos = cos_ref[...]
    sin = sin_ref[...]
    dk, dv = RET_DK, RET_DV
    for h in range(RET_HEADS):
        lg = jnp.full((1, 1), lg_ref[d, h], F32)
        q = _rotate(q_ref[0, :, h * dk:(h + 1) * dk].astype(F32), cos, sin)
        k = _rotate(k_ref[0, :, h * dk:(h + 1) * dk].astype(F32) * (dk ** -0.5), cos, sin)
        v = v_ref[0, :, h * dv:(h + 1) * dv]
        dmat = jnp.where(rel >= 0, jnp.exp(jnp.maximum(rel, 0.0) * lg), 0.0)
        inner = _dot_nt(q, k) * dmat
        s = s_ref[h]
        o_ref[0, 0, :, h * dv:(h + 1) * dv] = _dot(inner, v) + _dot(q * jnp.exp(q_pow * lg), s)
        s_ref[h] = s * jnp.exp(TM * lg) + _dot_tn(k * jnp.exp(k_pow * lg), v)


def _ret_scan(p, log_gamma, cos, sin):
    b, s, _ = p.shape
    nt = s // TM
    qw, vw = RET_HEADS * RET_DK, RET_HEADS * RET_DV
    tile = functools.partial(_scan_tile, n_tiles=nt, ctx_tiles=1)
    return pl.pallas_call(
        _ret_scan_kernel,
        grid=(b, 2, nt),
        in_specs=[pl.BlockSpec(memory_space=pltpu.SMEM),
                  pl.BlockSpec((1, TM, qw), lambda bi, d, i: (bi, tile(d, i), 0)),
                  pl.BlockSpec((1, TM, qw), lambda bi, d, i: (bi, tile(d, i), 1)),
                  pl.BlockSpec((1, TM, vw), lambda bi, d, i: (bi, tile(d, i), 1)),
                  pl.BlockSpec((TM, RET_DK), lambda bi, d, i: (tile(d, i), 0)),
                  pl.BlockSpec((TM, RET_DK), lambda bi, d, i: (tile(d, i), 0))],
        out_specs=pl.BlockSpec((1, 1, TM, vw), lambda bi, d, i: (d, bi, tile(d, i), 0)),
        out_shape=jax.ShapeDtypeStruct((2, b, s, vw), F32),
        scratch_shapes=[pltpu.VMEM((RET_HEADS, RET_DK, RET_DV), F32)],
        compiler_params=_params("parallel", "parallel", "arbitrary"),
        name="ret_scan",
    )(log_gamma, p, p, p, cos, sin)


def _ret_out_kernel(o_ref, gate_ref, x_ref, mod_ref, ng_ref, w_ref, out_ref):
    m = mod_ref[0, 0]
    o = o_ref[0, 0] + o_ref[1, 0]
    ng = ng_ref[...]
    parts = []
    for h in range(RET_HEADS):
        seg = o[:, h * RET_DV:(h + 1) * RET_DV]
        mu = jnp.mean(seg, axis=-1, keepdims=True)
        cen = seg - mu
        var = jnp.mean(cen * cen, axis=-1, keepdims=True)
        parts.append(cen * lax.rsqrt(var + EPS) * ng)
    y = _dot(_silu(gate_ref[0].astype(F32)) * jnp.concatenate(parts, axis=-1), w_ref[...])
    out_ref[0] = x_ref[0] + m[2:3] * y


def _ret_out(o, p, x, modv, norm_g, w_out_bf16):
    b, s, d = x.shape
    vw = RET_HEADS * RET_DV
    return pl.pallas_call(
        _ret_out_kernel,
        grid=(b, s // TM),
        in_specs=[pl.BlockSpec((2, 1, TM, vw), lambda bi, t: (0, bi, t, 0)),
                  pl.BlockSpec((1, TM, vw), lambda bi, t: (bi, t, 2)),
                  pl.BlockSpec((1, TM, d), lambda bi, t: (bi, t, 0)),
                  pl.BlockSpec((1, 1, 6, d), _mod_spec()),
                  pl.BlockSpec((1, RET_DV), lambda bi, t: (0, 0)),
                  pl.BlockSpec((vw, d), lambda bi, t: (0, 0))],
        out_specs=pl.BlockSpec((1, TM, d), lambda bi, t: (bi, t, 0)),
        out_shape=jax.ShapeDtypeStruct((b, s, d), F32),
        compiler_params=_params("parallel", "parallel"),
        name="ret_out",
    )(o, p, x, modv, norm_g.reshape(1, RET_DV), w_out_bf16)


def _rotary_tables(s):
    n_lat = s - TM
    pos = jnp.arange(n_lat, dtype=jnp.int32)
    rows = (pos // GRID_W).astype(F32)
    cols = (pos % GRID_W).astype(F32)
    quarter = RET_DK // 4
    inv_freq = ROPE_BASE ** (-jnp.arange(quarter, dtype=F32) / quarter)
    ang_r = rows[:, None] * inv_freq[None, :]
    ang_c = cols[:, None] * inv_freq[None, :]
    cos = jnp.concatenate([jnp.cos(ang_r), jnp.cos(ang_r), jnp.cos(ang_c), jnp.cos(ang_c)], axis=-1)
    sin = jnp.concatenate([-jnp.sin(ang_r), jnp.sin(ang_r), -jnp.sin(ang_c), jnp.sin(ang_c)], axis=-1)
    cos = jnp.concatenate([jnp.ones((TM, RET_DK), F32), cos], axis=0)
    sin = jnp.concatenate([jnp.zeros((TM, RET_DK), F32), sin], axis=0)
    return cos, sin


def _mixer_retention(x, modv, norm_g, w_in, decay_logit, out_norm_g, w_out):
    p = _in_proj(x, modv, norm_g, w_in.astype(BF16), w_in.shape[1] // 2)
    cos, sin = _rotary_tables(x.shape[1])
    o = _ret_scan(p, jax.nn.log_sigmoid(decay_logit.astype(F32)), cos, sin)
    return _ret_out(o, p, x, modv, out_norm_g, w_out.astype(BF16))


ROUTE_LANES = 128
ROUTE_COLS = 8
ISSUE_UNROLL = 8
ROW_TILE = 8


def _route_kernel(x_ref, mod_ref, g_ref, w_ref, b_ref, rt_ref, cnt_out_ref, cnt_ref):
    first = (pl.program_id(0) == 0) & (pl.program_id(1) == 0)

    @pl.when(first)
    def _():
        cnt_ref[...] = jnp.zeros_like(cnt_ref)

    m = mod_ref[0, 0]
    h = _rms_mod(x_ref[0], g_ref[...], m[3:4], m[4:5])
    logits = _dot3(h, w_ref[...]) + b_ref[...]
    lane = lax.broadcasted_iota(jnp.int32, (TM, ROUTE_LANES), 1)
    big = jnp.int32(ROUTE_LANES)
    neg = -jnp.inf
    glog = jnp.where((lane >= N_EXPERTS) & (lane < N_EXPERTS + N_GROUPS), logits, neg)
    gmax = jnp.max(glog, axis=-1, keepdims=True)
    gsel = jnp.min(jnp.where(glog == gmax, lane, big), axis=-1, keepdims=True) - N_EXPERTS
    p_group = 1.0 / jnp.sum(jnp.exp(glog - gmax), axis=-1, keepdims=True)
    elog = jnp.where((lane >= gsel * EXPERTS_PER_GROUP) & (lane < (gsel + 1) * EXPERTS_PER_GROUP), logits, neg)
    m1 = jnp.max(elog, axis=-1, keepdims=True)
    i1 = jnp.min(jnp.where(elog == m1, lane, big), axis=-1, keepdims=True)
    elog2 = jnp.where(lane == i1, neg, elog)
    m2 = jnp.max(elog2, axis=-1, keepdims=True)
    i2 = jnp.min(jnp.where(elog2 == m2, lane, big), axis=-1, keepdims=True)
    e2 = jnp.exp(m2 - m1)
    w1 = p_group / (1.0 + e2)
    w2 = p_group * e2 / (1.0 + e2)
    onehot = jnp.where((lane == i1) | (lane == i2), 1.0, 0.0)
    r = lax.broadcasted_iota(jnp.int32, (TM, TM), 0)
    c = lax.broadcasted_iota(jnp.int32, (TM, TM), 1)
    before = _dot(jnp.where(c < r, 1.0, 0.0), onehot) + cnt_ref[...]
    r1 = jnp.sum(jnp.where(lane == i1, before, 0.0), axis=-1, keepdims=True)
    r2 = jnp.sum(jnp.where(lane == i2, before, 0.0), axis=-1, keepdims=True)
    cnt_ref[...] = cnt_ref[...] + jnp.sum(onehot, axis=0, keepdims=True)
    cnt_out_ref[...] = cnt_ref[...]
    vals = (i1.astype(F32), i2.astype(F32), w1, w2, r1, r2)
    out = jnp.zeros((TM, ROUTE_LANES), F32)
    for pos_, val in enumerate(vals):
        out = jnp.where(lane == pos_, val, out)
    rt_ref[0] = out[:, :ROUTE_COLS]


def _route(x, modv, norm_g, w_route, b_route):
    b, s, d = x.shape
    return pl.pallas_call(
        _route_kernel,
        grid=(b, s // TM),
        in_specs=[pl.BlockSpec((1, TM, d), lambda bi, t: (bi, t, 0)),
                  pl.BlockSpec((1, 1, 6, d), _mod_spec()),
                  pl.BlockSpec((1, d), lambda bi, t: (0, 0)),
                  pl.BlockSpec((d, ROUTE_LANES), lambda bi, t: (0, 0)),
                  pl.BlockSpec((1, ROUTE_LANES), lambda bi, t: (0, 0))],
        out_specs=[pl.BlockSpec((1, TM, ROUTE_COLS), lambda bi, t: (bi, t, 0)),
                   pl.BlockSpec((1, ROUTE_LANES), lambda bi, t: (0, 0))],
        out_shape=[jax.ShapeDtypeStruct((b, s, ROUTE_COLS), F32), jax.ShapeDtypeStruct((1, ROUTE_LANES), F32)],
        scratch_shapes=[pltpu.VMEM((1, ROUTE_LANES), F32)],
        compiler_params=_params("arbitrary", "arbitrary"),
        name="moe_route",
    )(x, modv, norm_g.reshape(1, d), w_route, b_route)


def _issue_row_copies(make_copy):
    def body(i, carry):
        for k in range(2):
            make_copy(i, k).start()
        return carry

    lax.fori_loop(0, TM, body, 0, unroll=ISSUE_UNROLL)


def _tile_step():
    step = pl.program_id(0) * pl.num_programs(1) + pl.program_id(1)
    return step, pl.num_programs(0) * pl.num_programs(1), jnp.bitwise_and(step, 1)


def _pack_bf16_pairs(h):
    bits = pltpu.bitcast(h.astype(BF16).astype(F32), jnp.uint32)
    half = h.shape[1] // 2
    return jnp.bitwise_or(jnp.right_shift(bits[:, :half], jnp.uint32(16)),
                          jnp.bitwise_and(bits[:, half:], jnp.uint32(0xFFFF0000)))


def _unpack_bf16_pairs(w):
    lo = pltpu.bitcast(jnp.left_shift(w, jnp.uint32(16)), F32)
    hi = pltpu.bitcast(jnp.bitwise_and(w, jnp.uint32(0xFFFF0000)), F32)
    return jnp.concatenate([lo.astype(BF16), hi.astype(BF16)], axis=1)


def _dispatch_kernel(dest_ref, pe_ref, x_ref, mod_ref, g_ref, xb_ref, h_ref, z_ref, sem, zsem):
    step, n_steps, slot = _tile_step()

    @pl.when(step == 0)
    def _():
        z_ref[...] = jnp.zeros_like(z_ref)

        def clear(e):
            start = pl.multiple_of(pe_ref[e] - MOE_BLOCK, MOE_BLOCK)
            return pltpu.make_async_copy(z_ref, xb_ref.at[pl.ds(start, MOE_BLOCK)], zsem.at[e])

        def used(e):
            return pe_ref[e] > (pe_ref[e - 1] if e else 0)

        for e in range(N_EXPERTS):
            @pl.when(used(e))
            def _():
                clear(e).start()
        for e in range(N_EXPERTS):
            @pl.when(used(e))
            def _():
                clear(e).wait()

        def clear_unused(blk, carry):
            cp = pltpu.make_async_copy(z_ref, xb_ref.at[pl.ds(pl.multiple_of(blk * MOE_BLOCK, MOE_BLOCK), MOE_BLOCK)],
                                       zsem.at[0])
            cp.start()
            cp.wait()
            return carry

        lax.fori_loop(_block_of(pe_ref[N_EXPERTS - 1], MOE_BLOCK), xb_ref.shape[0] // MOE_BLOCK, clear_unused, 0)

    m = mod_ref[0, 0]
    h_ref[slot] = _pack_bf16_pairs(_rms_mod(x_ref[0], g_ref[...], m[3:4], m[4:5]))
    _issue_row_copies(lambda i, k: pltpu.make_async_copy(
        h_ref.at[slot, pl.ds(i, 1)], xb_ref.at[pl.ds(dest_ref[0, 0, 2 * i + k], 1)], sem.at[slot]))

    def wait_tile(sl):
        for _ in range(2):
            pltpu.make_async_copy(h_ref.at[sl], xb_ref.at[pl.ds(0, TM)], sem.at[sl]).wait()

    @pl.when(step > 0)
    def _():
        wait_tile(1 - slot)

    @pl.when(step == n_steps - 1)
    def _():
        wait_tile(slot)


def _dispatch(x, modv, norm_g, dest, pad_end, n_rows):
    b, s, d = x.shape
    nt = s // TM
    return pl.pallas_call(
        _dispatch_kernel,
        grid=(b, nt),
        in_specs=[pl.BlockSpec((1, 1, 2 * TM), lambda bi, t: (bi * nt + t, 0, 0), memory_space=pltpu.SMEM),
                  pl.BlockSpec(memory_space=pltpu.SMEM),
                  pl.BlockSpec((1, TM, d), lambda bi, t: (bi, t, 0)),
                  pl.BlockSpec((1, 1, 6, d), _mod_spec()),
                  pl.BlockSpec((1, d), lambda bi, t: (0, 0))],
        out_specs=pl.BlockSpec(memory_space=pl.ANY),
        out_shape=jax.ShapeDtypeStruct((n_rows, d // 2), jnp.uint32),
        scratch_shapes=[pltpu.VMEM((2, TM, d // 2), jnp.uint32), pltpu.VMEM((MOE_BLOCK, d // 2), jnp.uint32),
                        pltpu.SemaphoreType.DMA((2,)), pltpu.SemaphoreType.DMA((N_EXPERTS,))],
        compiler_params=_params("arbitrary", "arbitrary"),
        name="moe_dispatch",
    )(dest.reshape(b * nt, 1, 2 * TM), pad_end, x, modv, norm_g.reshape(1, d))


def _expert_kernel(be_ref, nu_ref, x_ref, wg_ref, wu_ref, wd_ref, o_ref, wg_bf, wu_bf, wd_bf):
    i = pl.program_id(0)

    @pl.when(i < nu_ref[0])
    def _():
        @pl.when((i == 0) | (be_ref[i] != be_ref[jnp.maximum(i - 1, 0)]))
        def _():
            wg_bf[...] = wg_ref[0, 0].astype(BF16)
            wu_bf[...] = wu_ref[0, 0].astype(BF16)
            wd_bf[...] = wd_ref[0, 0].astype(BF16)

        xb = _unpack_bf16_pairs(x_ref[...])
        act = _silu(jnp.dot(xb, wg_bf[...], preferred_element_type=F32)) * jnp.dot(xb, wu_bf[...], preferred_element_type=F32)
        y = jnp.dot(act.astype(BF16), wd_bf[...], preferred_element_type=F32)
        for j in range(ROW_TILE):
            o_ref[pl.ds(j, MOE_BLOCK, stride=ROW_TILE), :] = y[:, j * 128:(j + 1) * 128]

    @pl.when(i >= nu_ref[0])
    def _():
        o_ref[...] = jnp.zeros_like(o_ref)


def _experts(xb, block_e, n_used, layer, w_gate, w_up, w_down):
    n_rows = xb.shape[0]
    d, de = w_gate.shape[-2:]
    nb = n_rows // MOE_BLOCK

    def last_used(i, nu):
        return jnp.minimum(i, nu[0] - 1)

    grid_spec = pltpu.PrefetchScalarGridSpec(
        num_scalar_prefetch=2,
        grid=(nb,),
        in_specs=[pl.BlockSpec((MOE_BLOCK, d // 2), lambda i, be, nu: (last_used(i, nu), 0)),
                  pl.BlockSpec((1, 1, d, de), lambda i, be, nu: (layer, be[last_used(i, nu)], 0, 0)),
                  pl.BlockSpec((1, 1, d, de), lambda i, be, nu: (layer, be[last_used(i, nu)], 0, 0)),
                  pl.BlockSpec((1, 1, de, d), lambda i, be, nu: (layer, be[last_used(i, nu)], 0, 0))],
        out_specs=pl.BlockSpec((MOE_BLOCK * ROW_TILE, d // ROW_TILE), lambda i, be, nu: (i, 0)),
        scratch_shapes=[pltpu.VMEM((d, de), BF16), pltpu.VMEM((d, de), BF16), pltpu.VMEM((de, d), BF16)],
    )
    assert d == ROW_TILE * 128
    return pl.pallas_call(
        _expert_kernel,
        grid_spec=grid_spec,
        out_shape=jax.ShapeDtypeStruct((n_rows * ROW_TILE, d // ROW_TILE), F32),
        compiler_params=_params("arbitrary"),
        name="moe_experts",
    )(block_e, n_used, xb, w_gate, w_up, w_down)


def _combine_kernel(dest_ref, next_dest_ref, x_ref, mod_ref, rt_ref, fg_ref, yb_ref, o_ref, y_ref, sem, *, final_norm):
    step, n_steps, slot = _tile_step()

    rt_ = ROW_TILE
    slot_rows = 2 * TM * rt_

    def gather(d_ref, sl):
        _issue_row_copies(lambda i, k: pltpu.make_async_copy(
            yb_ref.at[pl.ds(pl.multiple_of(d_ref[0, 0, 2 * i + k] * rt_, rt_), rt_)],
            y_ref.at[pl.ds(pl.multiple_of(sl * slot_rows + (k * TM + i) * rt_, rt_), rt_)], sem.at[sl]))

    @pl.when(step == 0)
    def _():
        gather(dest_ref, 0)

    @pl.when(step + 1 < n_steps)
    def _():
        gather(next_dest_ref, 1 - slot)

    base = pl.multiple_of(slot * slot_rows, slot_rows)
    pltpu.make_async_copy(yb_ref.at[pl.ds(0, slot_rows)], y_ref.at[pl.ds(base, slot_rows)], sem.at[slot]).wait()
    m = mod_ref[0, 0]
    rt = rt_ref[0]
    x = x_ref[0]
    parts = []
    for j in range(rt_):
        cols = slice(j * 128, (j + 1) * 128)
        y1 = y_ref[pl.ds(base + j, TM, stride=rt_), :]
        y2 = y_ref[pl.ds(base + TM * rt_ + j, TM, stride=rt_), :]
        parts.append(x[:, cols] + m[5:6, cols] * (rt[:, 2:3] * y1 + rt[:, 3:4] * y2))
    if final_norm:
        ms = sum(jnp.sum(p * p, axis=-1, keepdims=True) for p in parts) / x.shape[1]
        scale = lax.rsqrt(ms + EPS)
        parts = [p * scale * fg_ref[:, j * 128:(j + 1) * 128] for j, p in enumerate(parts)]
    for j, p in enumerate(parts):
        o_ref[0, :, j * 128:(j + 1) * 128] = p


def _combine(x, modv, rt, dest, yb, final_g, final_norm):
    b, s, d = x.shape
    nt = s // TM
    skip = 1 if final_norm else 0

    def next_tile(bi, t):
        wrap = t + 1 >= nt - skip
        nb_ = jnp.minimum(jnp.where(wrap, bi + 1, bi), b - 1)
        return (nb_ * nt + jnp.where(wrap, 0, t + 1) + skip, 0, 0)

    return pl.pallas_call(
        functools.partial(_combine_kernel, final_norm=final_norm),
        grid=(b, nt - skip),
        in_specs=[pl.BlockSpec((1, 1, 2 * TM), lambda bi, t: (bi * nt + t + skip, 0, 0), memory_space=pltpu.SMEM),
                  pl.BlockSpec((1, 1, 2 * TM), next_tile, memory_space=pltpu.SMEM),
                  pl.BlockSpec((1, TM, d), lambda bi, t: (bi, t + skip, 0)),
                  pl.BlockSpec((1, 1, 6, d), lambda bi, t: (bi, jnp.minimum(t + skip, 1), 0, 0)),
                  pl.BlockSpec((1, TM, ROUTE_COLS), lambda bi, t: (bi, t + skip, 0)),
                  pl.BlockSpec((1, d), lambda bi, t: (0, 0)),
                  pl.BlockSpec(memory_space=pl.ANY)],
        out_specs=pl.BlockSpec((1, TM, d), lambda bi, t: (bi, t, 0)),
        out_shape=jax.ShapeDtypeStruct((b, s - skip * TM, d), F32),
        scratch_shapes=[pltpu.VMEM((2 * 2 * TM * ROW_TILE, d // ROW_TILE), F32), pltpu.SemaphoreType.DMA((2,))],
        compiler_params=_params("arbitrary", "arbitrary"),
        name="moe_combine",
    )(dest.reshape(b * nt, 1, 2 * TM), dest.reshape(b * nt, 1, 2 * TM), x, modv, rt, final_g.reshape(1, d), yb)


def _hier_moe(x, modv, norm_g, layer, wg_r, bg_r, we_r, be_r, w_gate, w_up, w_down, final_g, final_norm):
    b, s, d = x.shape
    n_tok = b * s
    pad = ROUTE_LANES - N_EXPERTS - N_GROUPS
    w_route = jnp.concatenate([we_r, wg_r, jnp.zeros((d, pad), F32)], axis=1)
    b_route = jnp.concatenate([be_r, bg_r, jnp.zeros((pad,), F32)]).reshape(1, ROUTE_LANES)
    rt, cnt = _route(x, modv, norm_g, w_route, b_route)
    counts = cnt[0, :N_EXPERTS].astype(jnp.int32)
    padded = (counts + MOE_BLOCK - 1) // MOE_BLOCK * MOE_BLOCK
    pad_end = jnp.cumsum(padded)
    pad_start = pad_end - padded
    n_blocks = -(-(2 * n_tok + N_EXPERTS * (MOE_BLOCK - 1)) // MOE_BLOCK)
    rt2 = rt.reshape(n_tok, ROUTE_COLS)
    expert = rt2[:, 0:2].astype(jnp.int32)
    experts = jnp.arange(N_EXPERTS, dtype=jnp.int32)
    start_of = jnp.sum(jnp.where(expert[..., None] == experts, pad_start, 0), axis=-1)
    dest = (start_of + rt2[:, 4:6].astype(jnp.int32)).reshape(-1)
    block_start = jnp.arange(n_blocks, dtype=jnp.int32) * MOE_BLOCK
    block_e = jnp.minimum(jnp.sum((pad_end[None, :] <= block_start[:, None]).astype(jnp.int32), axis=1), N_EXPERTS - 1)
    n_used = (pad_end[-1:] // MOE_BLOCK).astype(jnp.int32)
    xb = _dispatch(x, modv, norm_g, dest, pad_end.astype(jnp.int32), n_blocks * MOE_BLOCK)
    yb = _experts(xb, block_e, n_used, layer, w_gate, w_up, w_down)
    return _combine(x, modv, rt, dest, yb, final_g, final_norm)


def kernel(x, c, ctx, c_ctx, mod_w, mod_b, norm1_g, norm2_g, final_g, gdn_w_in, gdn_conv_w, gdn_a_log, gdn_dt_bias, gdn_norm_g, gdn_w_out, pool_w, pool_b, pool_scale, ret_w_in, ret_decay_logit, ret_norm_g, ret_w_out, router_group_w, router_group_b, router_expert_w, router_expert_b, exp_w_gate, exp_w_up, exp_w_down):
    b, n_lat, d = x.shape
    depth = mod_w.shape[0]
    assert ctx.shape[1] == TM and n_lat % TM == 0 and b < 16
    xs = jnp.concatenate([ctx, x], axis=1)
    cvec = jnp.concatenate([c, c_ctx[None], jnp.zeros((15 - b, d), F32)], axis=0)
    mods = _modulation(cvec, mod_w, mod_b)
    for i in range(depth):
        j, kind = i // 3, i % 3
        lat = mods[i, :b].reshape(b, 1, 6, d)
        con = jnp.broadcast_to(mods[i, b].reshape(1, 1, 6, d), (b, 1, 6, d))
        modv = jnp.concatenate([con, lat], axis=1)
        if kind == 0:
            xs = _mixer_gdn(xs, modv, norm1_g[i], gdn_w_in[j], gdn_conv_w[j], gdn_a_log[j], gdn_dt_bias[j],
                            gdn_norm_g[j], gdn_w_out[j])
        elif kind == 1:
            xs = _mixer_pool(xs, modv, norm1_g[i], pool_w[j], pool_b[j], pool_scale[j])
        else:
            xs = _mixer_retention(xs, modv, norm1_g[i], ret_w_in[j], ret_decay_logit[j], ret_norm_g[j], ret_w_out[j])
        xs = _hier_moe(xs, modv, norm2_g[i], i, router_group_w[i], router_group_b[i], router_expert_w[i],
                       router_expert_b[i], exp_w_gate, exp_w_up, exp_w_down, final_g, i == depth - 1)
    return xs
```

```python
import functools
import math

import jax
import jax.numpy as jnp
from jax import lax
from jax.experimental import pallas as pl
from jax.experimental.pallas import tpu as pltpu

F32 = jnp.float32
BF16 = jnp.bfloat16
HIGHEST = lax.Precision.HIGHEST

EPS = 1e-6
TM = 256
HALO = 8
HALO_BF16 = 16
CHUNK = 64
GDN_HEADS = 8
GDN_DK = 128
GDN_CONV = 5
RET_HEADS = 4
RET_DK = 256
RET_DV = 512
ROPE_BASE = 10000.0
GRID_W = 64
POOL_WINDOWS = (2, 4, 8, 16)
POOL_GROUP = 256
N_GROUPS = 4
EXPERTS_PER_GROUP = 8
N_EXPERTS = 32
MOE_BLOCK = 512
V7X_VMEM_LIMIT_BYTES = 56 * 1024 * 1024


def _params(*sem):
    return pltpu.CompilerParams(dimension_semantics=sem, vmem_limit_bytes=V7X_VMEM_LIMIT_BYTES)


def _dot(a, b):
    return jnp.dot(a.astype(BF16), b.astype(BF16), preferred_element_type=F32)


def _dot_hi(a, b):
    return jnp.dot(a, b, precision=HIGHEST, preferred_element_type=F32)


def _dot_nt(a, b):
    return lax.dot_general(a.astype(BF16), b.astype(BF16), (((1,), (1,)), ((), ())), preferred_element_type=F32)


def _split(x):
    hi = x.astype(BF16)
    return hi, (x - hi.astype(F32)).astype(BF16)


def _dot3_parts(ah, al, bh, bl, dims):
    def d(p, q):
        return lax.dot_general(p, q, dims, preferred_element_type=F32)
    return d(ah, bh) + (d(ah, bl) + d(al, bh))


def _dot3(a, b):
    return _dot3_parts(*_split(a), *_split(b), (((1,), (0,)), ((), ())))


def _cumsum_dot(ones_mask, x):
    x1 = x.astype(BF16)
    r1 = x - x1.astype(F32)
    x2 = r1.astype(BF16)
    x3 = (r1 - x2.astype(F32)).astype(BF16)
    m = ones_mask.astype(BF16)
    return (jnp.dot(m, x1, preferred_element_type=F32) + jnp.dot(m, x2, preferred_element_type=F32)
            + jnp.dot(m, x3, preferred_element_type=F32))


def _dot_tn(a, b):
    return lax.dot_general(a.astype(BF16), b.astype(BF16), (((0,), (0,)), ((), ())), preferred_element_type=F32)


def _block_of(i, size):
    return jnp.right_shift(i, int(math.log2(size)))


def _silu(x):
    return x * jax.nn.sigmoid(x)


def _softplus(x):
    return jnp.maximum(x, 0.0) + jnp.log(1.0 + jnp.exp(-jnp.abs(x)))


def _rms_mod(x, g, shift, scale):
    y = x * lax.rsqrt(jnp.mean(x * x, axis=-1, keepdims=True) + EPS) * g
    return y * (1.0 + scale) + shift


def _mod_spec(grid_rank_prefix=0):
    def idx(*g):
        b, t = g[grid_rank_prefix], g[grid_rank_prefix + 1]
        return (b, jnp.minimum(t, 1), 0, 0)
    return idx


def _mod_kernel(c_ref, w_ref, b_ref, o_ref):
    o_ref[0] = _dot_hi(_silu(c_ref[...]), w_ref[0]) + b_ref[0]


def _modulation(cvec, mod_w, mod_b):
    n_layers, d, d6 = mod_w.shape
    return pl.pallas_call(
        _mod_kernel,
        grid=(n_layers, d6 // d),
        in_specs=[pl.BlockSpec((16, d), lambda l, j: (0, 0)),
                  pl.BlockSpec((1, d, d), lambda l, j: (l, 0, j)),
                  pl.BlockSpec((1, 1, d), lambda l, j: (l, 0, j))],
        out_specs=pl.BlockSpec((1, 16, d), lambda l, j: (l, 0, j)),
        out_shape=jax.ShapeDtypeStruct((n_layers, 16, d6), F32),
        compiler_params=_params("parallel", "parallel"),
        name="modulation",
    )(cvec, mod_w, mod_b.reshape(n_layers, 1, d6))


def _in_kernel(x_ref, mod_ref, g_ref, w_ref, o_ref):
    m = mod_ref[0, 0]
    h = _rms_mod(x_ref[0], g_ref[...], m[0:1], m[1:2])
    o_ref[0] = _dot(h, w_ref[...]).astype(o_ref.dtype)


def _in_proj(x, modv, g, w_bf16, tn):
    b, s, d = x.shape
    n = w_bf16.shape[1]
    return pl.pallas_call(
        _in_kernel,
        grid=(n // tn, b, s // TM),
        in_specs=[pl.BlockSpec((1, TM, d), lambda j, bi, t: (bi, t, 0)),
                  pl.BlockSpec((1, 1, 6, d), _mod_spec(1)),
                  pl.BlockSpec((1, d), lambda j, bi, t: (0, 0)),
                  pl.BlockSpec((d, tn), lambda j, bi, t: (0, j))],
        out_specs=pl.BlockSpec((1, TM, tn), lambda j, bi, t: (bi, t, j)),
        out_shape=jax.ShapeDtypeStruct((b, s, n), BF16),
        compiler_params=_params("parallel", "parallel", "parallel"),
        name="in_proj",
    )(x, modv, g.reshape(1, d), w_bf16)


def _gates_kernel(x_ref, mod_ref, g_ref, wab_ref, alog_ref, dtb_ref, gc_ref, bt_ref):
    m = mod_ref[0, 0]
    h = _rms_mod(x_ref[0], g_ref[...], m[0:1], m[1:2])
    ab = _dot3(h, wab_ref[...])
    nh = GDN_HEADS
    gate = -jnp.exp(alog_ref[...]) * _softplus(ab[:, :2 * nh] + dtb_ref[...])
    beta = jax.nn.sigmoid(ab[:, 2 * nh:])
    r = lax.broadcasted_iota(jnp.int32, (TM, TM), 0)
    c = lax.broadcasted_iota(jnp.int32, (TM, TM), 1)
    same = _block_of(r, CHUNK) == _block_of(c, CHUNK)
    cum_f = jnp.where(same & (c <= r), 1.0, 0.0)
    cum_b = jnp.where(same & (c >= r), 1.0, 0.0)
    gc_ref[0, 0] = _cumsum_dot(cum_f, gate[:, :nh])
    gc_ref[1, 0] = _cumsum_dot(cum_b, gate[:, nh:])
    bt_ref[0, 0] = beta[:, :nh]
    bt_ref[1, 0] = beta[:, nh:]


def _gdn_gates(x, modv, g, w_ab, a_log, dt_bias):
    b, s, d = x.shape
    nh = GDN_HEADS
    out = jax.ShapeDtypeStruct((2, b, s, nh), F32)
    return pl.pallas_call(
        _gates_kernel,
        grid=(b, s // TM),
        in_specs=[pl.BlockSpec((1, TM, d), lambda bi, t: (bi, t, 0)),
                  pl.BlockSpec((1, 1, 6, d), _mod_spec()),
                  pl.BlockSpec((1, d), lambda bi, t: (0, 0)),
                  pl.BlockSpec((d, 4 * nh), lambda bi, t: (0, 0)),
                  pl.BlockSpec((1, 2 * nh), lambda bi, t: (0, 0)),
                  pl.BlockSpec((1, 2 * nh), lambda bi, t: (0, 0))],
        out_specs=[pl.BlockSpec((2, 1, TM, nh), lambda bi, t: (0, bi, t, 0)),
                   pl.BlockSpec((2, 1, TM, nh), lambda bi, t: (0, bi, t, 0))],
        out_shape=[out, out],
        compiler_params=_params("parallel", "parallel"),
        name="gdn_gates",
    )(x, modv, g.reshape(1, d), w_ab, a_log.reshape(1, 2 * nh), dt_bias.reshape(1, 2 * nh))


def _halo_specs(width, col_of, n_tiles, halo=HALO):
    per = TM // halo

    def prev(bi, t, *rest):
        return (bi, jnp.maximum(t * per - 1, 0), col_of(*rest))

    def nxt(bi, t, *rest):
        return (bi, jnp.minimum((t + 1) * per, n_tiles * per - 1), col_of(*rest))

    return pl.BlockSpec((1, halo, width), prev), pl.BlockSpec((1, halo, width), nxt)


def _halo_valid():
    t = pl.program_id(1)
    nt = pl.num_programs(1)
    return t >= 2, (t >= 1) & (t < nt - 1)


def _conv_kernel(cur_ref, prev_ref, next_ref, w_ref, o_ref, k32_ref, ext_ref):
    j = pl.program_id(2)
    prev_ok, next_ok = _halo_valid()
    halo = HALO_BF16
    ext_ref[0:halo] = jnp.where(prev_ok, prev_ref[0].astype(F32), 0.0)
    ext_ref[halo:halo + TM] = cur_ref[0].astype(F32)
    ext_ref[halo + TM:] = jnp.where(next_ok, next_ref[0].astype(F32), 0.0)
    w = w_ref[...]
    base = halo - GDN_CONV // 2
    acc = w[0:1] * ext_ref[base:base + TM]
    for k in range(1, GDN_CONV):
        acc = acc + w[k:k + 1] * ext_ref[base + k:base + k + TM]
    y = _silu(acc)
    width = y.shape[1]
    is_v = j >= 2 * (GDN_HEADS * GDN_DK // width)
    is_q = j < (GDN_HEADS * GDN_DK // width)
    qscale = jnp.where(is_q, GDN_DK ** -0.5, 1.0)
    vals = []
    for hh in range(width // GDN_DK):
        seg = y[:, hh * GDN_DK:(hh + 1) * GDN_DK]
        nrm = seg * lax.rsqrt(jnp.sum(seg * seg, axis=-1, keepdims=True) + EPS) * qscale
        vals.append(jnp.where(is_v, seg, nrm))
        o_ref[0, :, hh * GDN_DK:(hh + 1) * GDN_DK] = vals[-1].astype(o_ref.dtype)

    @pl.when(j == 1)
    def _():
        for hh, val in enumerate(vals):
            k32_ref[0, :, hh * GDN_DK:(hh + 1) * GDN_DK] = val


def _gdn_conv(p, conv_w):
    b, s, _ = p.shape
    n = conv_w.shape[1]
    width = GDN_HEADS * GDN_DK
    prev_spec, next_spec = _halo_specs(width, lambda j: j, s // TM, HALO_BF16)
    return pl.pallas_call(
        _conv_kernel,
        grid=(b, s // TM, n // width),
        in_specs=[pl.BlockSpec((1, TM, width), lambda bi, t, j: (bi, t, j)),
                  prev_spec, next_spec,
                  pl.BlockSpec((GDN_CONV, width), lambda bi, t, j: (0, j))],
        out_specs=[pl.BlockSpec((1, TM, width), lambda bi, t, j: (bi, t, j)),
                   pl.BlockSpec((1, TM, width), lambda bi, t, j: (bi, t, 0))],
        out_shape=[jax.ShapeDtypeStruct((b, s, n), BF16), jax.ShapeDtypeStruct((b, s, width), F32)],
        scratch_shapes=[pltpu.VMEM((TM + 2 * HALO_BF16, width), F32)],
        compiler_params=_params("parallel", "parallel", "arbitrary"),
        name="gdn_conv",
    )(p, p, p, conv_w)


GDN_PAIR = 2 * GDN_DK
GDN_PAIRS = GDN_HEADS // 2
INV_LANES = 128
SCAN_BATCH = 4


def _pair_cols(cols, hp, width):
    lane = lax.broadcasted_iota(jnp.int32, (cols.shape[0], width), 1)
    return jnp.where(lane < width // 2, cols[:, 2 * hp:2 * hp + 1], cols[:, 2 * hp + 1:2 * hp + 2])


def _pair_blockdiag_rows(x, lane_block):
    n, w = x.shape
    r = lax.broadcasted_iota(jnp.int32, (2 * n, w), 0)
    c = lax.broadcasted_iota(jnp.int32, (2 * n, w), 1)
    same = _block_of(r, n) == jnp.bitwise_and(_block_of(c, lane_block), 1)
    return jnp.where(same, jnp.concatenate([x, x], axis=0), 0.0)


def _pair_decay(gc_cols, gct_row, hp, fwd, inclusive):
    r = lax.broadcasted_iota(jnp.int32, (CHUNK, 2 * CHUNK), 0)
    c = jnp.bitwise_and(lax.broadcasted_iota(jnp.int32, (CHUNK, 2 * CHUNK), 1), CHUNK - 1)
    ahead = (r - c) if fwd else (c - r)
    keep = (ahead >= 0) if inclusive else (ahead > 0)
    rel = _pair_cols(gc_cols, hp, 2 * CHUNK) - gct_row
    return jnp.where(keep, jnp.exp(jnp.where(keep, rel, 0.0)), 0.0)


def _gdn_a_kernel(k_ref, gc_ref, gctp_ref, bt_ref, a_ref):
    for ch in range(TM // CHUNK):
        rows = pl.ds(ch * CHUNK, CHUNK)
        for hp in range(GDN_PAIRS):
            kh, kl = _split(k_ref[0, rows, hp * GDN_PAIR:(hp + 1) * GDN_PAIR])
            kk = _dot3_parts(kh, kl, _pair_blockdiag_rows(kh, GDN_DK), _pair_blockdiag_rows(kl, GDN_DK),
                             (((1,), (1,)), ((), ())))
            for d in range(2):
                decay = _pair_decay(gc_ref[d, 0, rows, :], gctp_ref[d, 0, ch, hp:hp + 1, :], hp, d == 0, False)
                a_ref[d, 0, ch, hp] = _pair_cols(bt_ref[d, 0, rows, :], hp, 2 * CHUNK) * kk * decay


def _gdn_a(k32, gc, gctp, bt):
    b, s, _ = k32.shape
    nh = GDN_HEADS
    cpt = TM // CHUNK
    gate_spec = pl.BlockSpec((2, 1, TM, nh), lambda bi, t: (0, bi, t, 0))
    return pl.pallas_call(
        _gdn_a_kernel,
        grid=(b, s // TM),
        in_specs=[pl.BlockSpec((1, TM, nh * GDN_DK), lambda bi, t: (bi, t, 0)),
                  gate_spec,
                  pl.BlockSpec((2, 1, cpt, GDN_PAIRS, 2 * CHUNK), lambda bi, t: (0, bi, t, 0, 0)),
                  gate_spec],
        out_specs=pl.BlockSpec((2, 1, cpt, GDN_PAIRS, CHUNK, 2 * CHUNK), lambda bi, t: (0, bi, t, 0, 0, 0)),
        out_shape=jax.ShapeDtypeStruct((2, b, s // CHUNK, GDN_PAIRS, CHUNK, 2 * CHUNK), F32),
        compiler_params=_params("parallel", "parallel"),
        name="gdn_a",
    )(k32, gc, gctp, bt)


def _substitute_rows(at_ref, x_ref, hh):
    n = CHUNK
    zero = jnp.zeros((8, INV_LANES), F32)
    for i in range(n):
        nb = (i + 7) // 8
        acc = [-at_ref[hh, pl.ds(i * n + jb * 8, 8), :] for jb in range(nb)]
        for m in range(1, i):
            a_im = jnp.broadcast_to(at_ref[hh, pl.ds(i * n + m, 1), :], (8, INV_LANES))
            for jb in range((m + 7) // 8):
                acc[jb] = acc[jb] - a_im * x_ref[hh, pl.ds(m * n + jb * 8, 8), :]
        for jb in range(n // 8):
            x_ref[hh, pl.ds(i * n + jb * 8, 8), :] = acc[jb] if jb < nb else zero


def _gdn_inv_kernel(a_ref, t_ref, at_ref, x_ref):
    fwd = pl.program_id(0) == 0
    n = CHUNK

    def load(r, transposed):
        slab = a_ref[0, pl.ds(r, INV_LANES, stride=n), :].T
        for hh in range(2):
            dst = pl.ds(r, n, stride=n) if transposed else pl.ds(r * n, n)
            at_ref[hh, dst, :] = slab[hh * n:(hh + 1) * n]

    def store(r, transposed):
        src = pl.ds(r, n, stride=n) if transposed else pl.ds(r * n, n)
        eye = jnp.where(lax.broadcasted_iota(jnp.int32, (n, INV_LANES), 0) == r, 1.0, 0.0)
        slab = jnp.concatenate([x_ref[0, src, :] + eye, x_ref[1, src, :] + eye], axis=0)
        t_ref[0, pl.ds(r, INV_LANES, stride=n), :] = slab.T

    def rows(fn, transposed):
        for r in range(n):
            fn(r, transposed)

    @pl.when(fwd)
    def _():
        rows(load, False)

    @pl.when(jnp.logical_not(fwd))
    def _():
        rows(load, True)

    def halves(hh, carry):
        _substitute_rows(at_ref, x_ref, hh)
        return carry

    lax.fori_loop(0, 2, halves, 0)

    @pl.when(fwd)
    def _():
        rows(store, False)

    @pl.when(jnp.logical_not(fwd))
    def _():
        rows(store, True)


def _gdn_inv(a):
    shape = a.shape
    n_sys = shape[1] * shape[2] * shape[3]
    assert n_sys % INV_LANES == 0
    rows_per_step = INV_LANES * CHUNK
    t = pl.pallas_call(
        _gdn_inv_kernel,
        grid=(2, n_sys // INV_LANES),
        in_specs=[pl.BlockSpec((1, rows_per_step, 2 * CHUNK), lambda d, g: (d, g, 0))],
        out_specs=pl.BlockSpec((1, rows_per_step, 2 * CHUNK), lambda d, g: (d, g, 0)),
        out_shape=jax.ShapeDtypeStruct((2, n_sys * CHUNK, 2 * CHUNK), F32),
        scratch_shapes=[pltpu.VMEM((2, CHUNK * CHUNK, INV_LANES), F32), pltpu.VMEM((2, CHUNK * CHUNK, INV_LANES), F32)],
        compiler_params=_params("parallel", "parallel"),
        name="gdn_inv",
    )(a.reshape(2, n_sys * CHUNK, 2 * CHUNK))
    return t.reshape(shape)


def _gdn_uw_kernel(t_ref, q_ref, k_ref, v_ref, gc_ref, gctp_ref, bt_ref, u_ref, w_ref, qk_ref):
    for ch in range(TM // CHUNK):
        rows = pl.ds(ch * CHUNK, CHUNK)
        for hp in range(GDN_PAIRS):
            cols = slice(hp * GDN_PAIR, (hp + 1) * GDN_PAIR)
            kp = k_ref[0, rows, cols]
            vp = v_ref[0, rows, cols]
            qk = _dot_nt(q_ref[0, rows, cols], _pair_blockdiag_rows(kp, GDN_DK))
            for d in range(2):
                gc = gc_ref[d, 0, rows, :]
                beta = _pair_cols(bt_ref[d, 0, rows, :], hp, GDN_PAIR)
                rhs = jnp.concatenate([vp * beta, kp * (beta * jnp.exp(_pair_cols(gc, hp, GDN_PAIR)))], axis=1)
                uw = _dot(t_ref[d, 0, ch, hp], _pair_blockdiag_rows(rhs, GDN_DK))
                u_ref[d, 0, rows, cols] = uw[:, :GDN_PAIR]
                w_ref[d, 0, rows, cols] = uw[:, GDN_PAIR:].astype(BF16)
                decay = _pair_decay(gc, gctp_ref[d, 0, ch, hp:hp + 1, :], hp, d == 0, True)
                qk_ref[d, 0, ch, hp] = (qk * decay).astype(BF16)


def _gdn_uw(t, qkv, gc, gctp, bt):
    b, s, _ = qkv.shape
    nh = GDN_HEADS
    width = nh * GDN_DK
    cpt = TM // CHUNK
    gate_spec = pl.BlockSpec((2, 1, TM, nh), lambda bi, t_: (0, bi, t_, 0))
    sys_spec = pl.BlockSpec((2, 1, cpt, GDN_PAIRS, CHUNK, 2 * CHUNK), lambda bi, t_: (0, bi, t_, 0, 0, 0))
    tok_spec = pl.BlockSpec((2, 1, TM, width), lambda bi, t_: (0, bi, t_, 0))
    return pl.pallas_call(
        _gdn_uw_kernel,
        grid=(b, s // TM),
        in_specs=[sys_spec,
                  pl.BlockSpec((1, TM, width), lambda bi, t_: (bi, t_, 0)),
                  pl.BlockSpec((1, TM, width), lambda bi, t_: (bi, t_, 1)),
                  pl.BlockSpec((1, TM, width), lambda bi, t_: (bi, t_, 2)),
                  gate_spec,
                  pl.BlockSpec((2, 1, cpt, GDN_PAIRS, 2 * CHUNK), lambda bi, t_: (0, bi, t_, 0, 0)),
                  gate_spec],
        out_specs=[tok_spec, tok_spec, sys_spec],
        out_shape=[jax.ShapeDtypeStruct((2, b, s, width), F32), jax.ShapeDtypeStruct((2, b, s, width), BF16),
                   jax.ShapeDtypeStruct((2, b, s // CHUNK, GDN_PAIRS, CHUNK, 2 * CHUNK), BF16)],
        compiler_params=_params("parallel", "parallel"),
        name="gdn_uw",
    )(t, qkv, qkv, qkv, gc, gctp, bt)


def _gdn_scan_kernel(qf_ref, kf_ref, uf_ref, wf_ref, qkf_ref, gcf_ref, qb_ref, kb_ref, ub_ref, wb_ref, qkb_ref, gcb_ref,
                     of_ref, ob_ref, s_ref):
    @pl.when(pl.program_id(1) == 0)
    def _():
        s_ref[...] = jnp.zeros_like(s_ref)

    dk = GDN_DK
    zeros = jnp.zeros((dk, dk), BF16)
    directions = ((qf_ref, kf_ref, uf_ref, wf_ref, qkf_ref, gcf_ref, of_ref),
                  (qb_ref, kb_ref, ub_ref, wb_ref, qkb_ref, gcb_ref, ob_ref))
    chains = [(bb, d, hp) for bb in range(SCAN_BATCH) for d in range(2) for hp in range(GDN_PAIRS)]
    stage1 = []
    for bb, d, hp in chains:
        q_ref, k_ref, u_ref, w_ref, qk_ref, gc_ref, o_ref = directions[d]
        cols = slice(hp * GDN_PAIR, (hp + 1) * GDN_PAIR)
        gc_all = gc_ref[0, bb]
        glast = gc_all[CHUNK - 1:CHUNK] if d == 0 else gc_all[0:1]
        gcp = _pair_cols(gc_all, hp, GDN_PAIR)
        q_in = q_ref[bb, :, cols] * jnp.exp(gcp)
        k_out = (k_ref[bb, :, cols] * jnp.exp(_pair_cols(glast, hp, GDN_PAIR) - gcp)).astype(BF16)
        sa = s_ref[bb, d, 2 * hp]
        sb = s_ref[bb, d, 2 * hp + 1]
        s_bd = jnp.concatenate([jnp.concatenate([sa.astype(BF16), zeros], axis=1),
                                jnp.concatenate([zeros, sb.astype(BF16)], axis=1)], axis=0)
        both = jnp.dot(jnp.concatenate([w_ref[0, bb, :, cols], q_in.astype(BF16)], axis=0), s_bd,
                       preferred_element_type=F32)
        stage1.append((both, k_out, jnp.exp(glast)))
    stage2 = []
    for (bb, d, hp), (both, k_out, _) in zip(chains, stage1):
        q_ref, k_ref, u_ref, w_ref, qk_ref, gc_ref, o_ref = directions[d]
        cols = slice(hp * GDN_PAIR, (hp + 1) * GDN_PAIR)
        v_new = u_ref[0, bb, :, cols] - both[:CHUNK]
        intra = jnp.dot(qk_ref[0, bb, 0, hp], _pair_blockdiag_rows(v_new, dk).astype(BF16), preferred_element_type=F32)
        upd = _dot_tn(k_out, v_new)
        stage2.append((intra, upd))
    for (bb, d, hp), (both, _, chunk_decay), (intra, upd) in zip(chains, stage1, stage2):
        o_ref = directions[d][6]
        cols = slice(hp * GDN_PAIR, (hp + 1) * GDN_PAIR)
        o_ref[bb, :, cols] = both[CHUNK:] + intra
        s_ref[bb, d, 2 * hp] = s_ref[bb, d, 2 * hp] * chunk_decay[:, 2 * hp:2 * hp + 1] + upd[:dk, :dk]
        s_ref[bb, d, 2 * hp + 1] = s_ref[bb, d, 2 * hp + 1] * chunk_decay[:, 2 * hp + 1:2 * hp + 2] + upd[dk:, dk:]


def _scan_tile(d, step, n_tiles, ctx_tiles):
    back = jnp.where(step < ctx_tiles, ctx_tiles - 1 - step, n_tiles + ctx_tiles - 1 - step)
    return jnp.where(d == 0, step, back)


def _gdn_scan(qkv, u, w, qk, gc):
    b, s, _ = qkv.shape
    nh, dk = GDN_HEADS, GDN_DK
    nc = s // CHUNK
    width = nh * dk
    tile = functools.partial(_scan_tile, n_tiles=nc, ctx_tiles=TM // CHUNK)

    nb = SCAN_BATCH
    assert b % nb == 0

    def specs(d):
        return [pl.BlockSpec((nb, CHUNK, width), lambda bi, i: (bi, tile(d, i), 0)),
                pl.BlockSpec((nb, CHUNK, width), lambda bi, i: (bi, tile(d, i), 1)),
                pl.BlockSpec((1, nb, CHUNK, width), lambda bi, i: (d, bi, tile(d, i), 0)),
                pl.BlockSpec((1, nb, CHUNK, width), lambda bi, i: (d, bi, tile(d, i), 0)),
                pl.BlockSpec((1, nb, 1, GDN_PAIRS, CHUNK, 2 * CHUNK), lambda bi, i: (d, bi, tile(d, i), 0, 0, 0)),
                pl.BlockSpec((1, nb, CHUNK, nh), lambda bi, i: (d, bi, tile(d, i), 0))]

    out = jax.ShapeDtypeStruct((b, s, width), F32)
    return pl.pallas_call(
        _gdn_scan_kernel,
        grid=(b // nb, nc),
        in_specs=specs(0) + specs(1),
        out_specs=[pl.BlockSpec((nb, CHUNK, width), lambda bi, i: (bi, tile(0, i), 0)),
                   pl.BlockSpec((nb, CHUNK, width), lambda bi, i: (bi, tile(1, i), 0))],
        out_shape=[out, out],
        scratch_shapes=[pltpu.VMEM((nb, 2, nh, dk, dk), F32)],
        compiler_params=_params("parallel", "arbitrary"),
        name="gdn_scan",
    )(qkv, qkv, u, w, qk, gc, qkv, qkv, u, w, qk, gc)


def _gdn_out_kernel(of_ref, ob_ref, z_ref, x_ref, mod_ref, ng_ref, w_ref, out_ref):
    m = mod_ref[0, 0]
    o = of_ref[0] + ob_ref[0]
    ng = ng_ref[...]
    parts = []
    for h in range(GDN_HEADS):
        seg = o[:, h * GDN_DK:(h + 1) * GDN_DK]
        parts.append(seg * lax.rsqrt(jnp.mean(seg * seg, axis=-1, keepdims=True) + EPS) * ng)
    y = _dot(jnp.concatenate(parts, axis=-1) * _silu(z_ref[0].astype(F32)), w_ref[...])
    out_ref[0] = x_ref[0] + m[2:3] * y


def _gdn_out(o_f, o_b, p, x, modv, norm_g, w_out_bf16):
    b, s, d = x.shape
    width = GDN_HEADS * GDN_DK
    return pl.pallas_call(
        _gdn_out_kernel,
        grid=(b, s // TM),
        in_specs=[pl.BlockSpec((1, TM, width), lambda bi, t: (bi, t, 0)),
                  pl.BlockSpec((1, TM, width), lambda bi, t: (bi, t, 0)),
                  pl.BlockSpec((1, TM, width), lambda bi, t: (bi, t, 3)),
                  pl.BlockSpec((1, TM, d), lambda bi, t: (bi, t, 0)),
                  pl.BlockSpec((1, 1, 6, d), _mod_spec()),
                  pl.BlockSpec((1, GDN_DK), lambda bi, t: (0, 0)),
                  pl.BlockSpec((width, d), lambda bi, t: (0, 0))],
        out_specs=pl.BlockSpec((1, TM, d), lambda bi, t: (bi, t, 0)),
        out_shape=jax.ShapeDtypeStruct((b, s, d), F32),
        compiler_params=_params("parallel", "parallel"),
        name="gdn_out",
    )(o_f, o_b, p, x, modv, norm_g.reshape(1, GDN_DK), w_out_bf16)


def _mixer_gdn(x, modv, norm_g, w_in, conv_w, a_log, dt_bias, out_norm_g, w_out):
    nh, dk = GDN_HEADS, GDN_DK
    n_main = 4 * nh * dk
    p = _in_proj(x, modv, norm_g, w_in[:, :n_main].astype(BF16), n_main)
    gc, bt = _gdn_gates(x, modv, norm_g, w_in[:, n_main:], a_log, dt_bias)
    b, s, _ = x.shape
    gctp = gc.reshape(2, b, s // CHUNK, CHUNK, GDN_PAIRS, 2).transpose(0, 1, 2, 4, 5, 3)
    gctp = gctp.reshape(2, b, s // CHUNK, GDN_PAIRS, 2 * CHUNK)
    qkv, k32 = _gdn_conv(p, conv_w)
    t = _gdn_inv(_gdn_a(k32, gc, gctp, bt))
    u, w, qk = _gdn_uw(t, qkv, gc, gctp, bt)
    o_f, o_b = _gdn_scan(qkv, u, w, qk, gc)
    return _gdn_out(o_f, o_b, p, x, modv, out_norm_g, w_out.astype(BF16))


def _pool_kernel(x_ref, xp_ref, xn_ref, mod_ref, g_ref, w_ref, b_ref, sc_ref, o_ref, ext_ref):
    t = pl.program_id(1)
    nt = pl.num_programs(1)
    prev_ok, next_ok = _halo_valid()
    m = mod_ref[0, 0]
    g = g_ref[...]
    x = x_ref[0]
    h = _rms_mod(x, g, m[0:1], m[1:2])
    ext_ref[0:HALO] = jnp.where(prev_ok, _rms_mod(xp_ref[0], g, m[0:1], m[1:2]), 0.0)
    ext_ref[HALO:HALO + TM] = h
    ext_ref[HALO + TM:] = jnp.where(next_ok, _rms_mod(xn_ref[0], g, m[0:1], m[1:2]), 0.0)
    row = lax.broadcasted_iota(jnp.int32, (TM, 1), 0)
    pos = row + jnp.where(t == 0, 0, (t - 1) * TM)
    n_seq = jnp.where(t == 0, TM, (nt - 1) * TM)
    pg = POOL_GROUP
    for gi, win in enumerate(POOL_WINDOWS):
        lo_off = HALO - win // 2
        acc = ext_ref[lo_off:lo_off + TM, gi * pg:(gi + 1) * pg]
        for k in range(1, win):
            acc = acc + ext_ref[lo_off + k:lo_off + k + TM, gi * pg:(gi + 1) * pg]
        lo = jnp.clip(pos - win // 2, 0, n_seq)
        hi = jnp.clip(pos + win - win // 2, 0, n_seq)
        pooled = acc / (hi - lo).astype(F32) - h[:, gi * pg:(gi + 1) * pg]
        y = (_dot(pooled, w_ref[gi]) + b_ref[gi]) * sc_ref[:, gi * pg:(gi + 1) * pg]
        o_ref[0, :, gi * pg:(gi + 1) * pg] = x[:, gi * pg:(gi + 1) * pg] + m[2:3, gi * pg:(gi + 1) * pg] * y


def _mixer_pool(x, modv, norm_g, w_group, b_group, scale):
    b, s, d = x.shape
    ng, pg = len(POOL_WINDOWS), POOL_GROUP
    prev_spec, next_spec = _halo_specs(d, lambda: 0, s // TM)
    return pl.pallas_call(
        _pool_kernel,
        grid=(b, s // TM),
        in_specs=[pl.BlockSpec((1, TM, d), lambda bi, t: (bi, t, 0)),
                  prev_spec, next_spec,
                  pl.BlockSpec((1, 1, 6, d), _mod_spec()),
                  pl.BlockSpec((1, d), lambda bi, t: (0, 0)),
                  pl.BlockSpec((ng, pg, pg), lambda bi, t: (0, 0, 0)),
                  pl.BlockSpec((ng, 1, pg), lambda bi, t: (0, 0, 0)),
                  pl.BlockSpec((1, d), lambda bi, t: (0, 0))],
        out_specs=pl.BlockSpec((1, TM, d), lambda bi, t: (bi, t, 0)),
        out_shape=jax.ShapeDtypeStruct((b, s, d), F32),
        scratch_shapes=[pltpu.VMEM((TM + 2 * HALO, d), F32)],
        compiler_params=_params("parallel", "parallel"),
        name="pool_mixer",
    )(x, x, x, modv, norm_g.reshape(1, d), w_group.astype(BF16), b_group.reshape(ng, 1, pg), scale.reshape(1, d))


def _rotate(t, cos, sin_signed):
    half = RET_DK // 2
    swapped = jnp.concatenate([pltpu.roll(t[:, :half], half // 2, 1), pltpu.roll(t[:, half:], half // 2, 1)], axis=-1)
    return t * cos + swapped * sin_signed


def _ret_scan_kernel(lg_ref, q_ref, k_ref, v_ref, cos_ref, sin_ref, o_ref, s_ref):
    d = pl.program_id(1)
    step = pl.program_id(2)

    @pl.when(step == 0)
    def _():
        s_ref[...] = jnp.zeros_like(s_ref)

    fwd = d == 0
    r = lax.broadcasted_iota(jnp.int32, (TM, TM), 0)
    c = lax.broadcasted_iota(jnp.int32, (TM, TM), 1)
    rel = jnp.where(fwd, r - c, c - r).astype(F32)
    row = lax.broadcasted_iota(jnp.int32, (TM, 1), 0)
    q_pow = jnp.where(fwd, row + 1, TM - row).astype(F32)
    k_pow = jnp.where(fwd, TM - 1 - row, row).astype(F32)
    cos = cos_ref[...]
    sin = sin_ref[...]
    dk, dv = RET_DK, RET_DV
    for h in range(RET_HEADS):
        lg = jnp.full((1, 1), lg_ref[d, h], F32)
        q = _rotate(q_ref[0, :, h * dk:(h + 1) * dk].astype(F32), cos, sin)
        k = _rotate(k_ref[0, :, h * dk:(h + 1) * dk].astype(F32) * (dk ** -0.5), cos, sin)
        v = v_ref[0, :, h * dv:(h + 1) * dv]
        dmat = jnp.where(rel >= 0, jnp.exp(jnp.maximum(rel, 0.0) * lg), 0.0)
        inner = _dot_nt(q, k) * dmat
        s = s_ref[h]
        o_ref[0, 0, :, h * dv:(h + 1) * dv] = _dot(inner, v) + _dot(q * jnp.exp(q_pow * lg), s)
        s_ref[h] = s * jnp.exp(TM * lg) + _dot_tn(k * jnp.exp(k_pow * lg), v)


def _ret_scan(p, log_gamma, cos, sin):
    b, s, _ = p.shape
    nt = s // TM
    qw, vw = RET_HEADS * RET_DK, RET_HEADS * RET_DV
    tile = functools.partial(_scan_tile, n_tiles=nt, ctx_tiles=1)
    return pl.pallas_call(
        _ret_scan_kernel,
        grid=(b, 2, nt),
        in_specs=[pl.BlockSpec(memory_space=pltpu.SMEM),
                  pl.BlockSpec((1, TM, qw), lambda bi, d, i: (bi, tile(d, i), 0)),
                  pl.BlockSpec((1, TM, qw), lambda bi, d, i: (bi, tile(d, i), 1)),
                  pl.BlockSpec((1, TM, vw), lambda bi, d, i: (bi, tile(d, i), 1)),
                  pl.BlockSpec((TM, RET_DK), lambda bi, d, i: (tile(d, i), 0)),
                  pl.BlockSpec((TM, RET_DK), lambda bi, d, i: (tile(d, i), 0))],
        out_specs=pl.BlockSpec((1, 1, TM, vw), lambda bi, d, i: (d, bi, tile(d, i), 0)),
        out_shape=jax.ShapeDtypeStruct((2, b, s, vw), F32),
        scratch_shapes=[pltpu.VMEM((RET_HEADS, RET_DK, RET_DV), F32)],
        compiler_params=_params("parallel", "parallel", "arbitrary"),
        name="ret_scan",
    )(log_gamma, p, p, p, cos, sin)


def _ret_out_kernel(o_ref, gate_ref, x_ref, mod_ref, ng_ref, w_ref, out_ref):
    m = mod_ref[0, 0]
    o = o_ref[0, 0] + o_ref[1, 0]
    ng = ng_ref[...]
    parts = []
    for h in range(RET_HEADS):
        seg = o[:, h * RET_DV:(h + 1) * RET_DV]
        mu = jnp.mean(seg, axis=-1, keepdims=True)
        cen = seg - mu
        var = jnp.mean(cen * cen, axis=-1, keepdims=True)
        parts.append(cen * lax.rsqrt(var + EPS) * ng)
    y = _dot(_silu(gate_ref[0].astype(F32)) * jnp.concatenate(parts, axis=-1), w_ref[...])
    out_ref[0] = x_ref[0] + m[2:3] * y


def _ret_out(o, p, x, modv, norm_g, w_out_bf16):
    b, s, d = x.shape
    vw = RET_HEADS * RET_DV
    return pl.pallas_call(
        _ret_out_kernel,
        grid=(b, s // TM),
        in_specs=[pl.BlockSpec((2, 1, TM, vw), lambda bi, t: (0, bi, t, 0)),
                  pl.BlockSpec((1, TM, vw), lambda bi, t: (bi, t, 2)),
                  pl.BlockSpec((1, TM, d), lambda bi, t: (bi, t, 0)),
                  pl.BlockSpec((1, 1, 6, d), _mod_spec()),
                  pl.BlockSpec((1, RET_DV), lambda bi, t: (0, 0)),
                  pl.BlockSpec((vw, d), lambda bi, t: (0, 0))],
        out_specs=pl.BlockSpec((1, TM, d), lambda bi, t: (bi, t, 0)),
        out_shape=jax.ShapeDtypeStruct((b, s, d), F32),
        compiler_params=_params("parallel", "parallel"),
        name="ret_out",
    )(o, p, x, modv, norm_g.reshape(1, RET_DV), w_out_bf16)


def _rotary_tables(s):
    n_lat = s - TM
    pos = jnp.arange(n_lat, dtype=jnp.int32)
    rows = (pos // GRID_W).astype(F32)
    cols = (pos % GRID_W).astype(F32)
    quarter = RET_DK // 4
    inv_freq = ROPE_BASE ** (-jnp.arange(quarter, dtype=F32) / quarter)
    ang_r = rows[:, None] * inv_freq[None, :]
    ang_c = cols[:, None] * inv_freq[None, :]
    cos = jnp.concatenate([jnp.cos(ang_r), jnp.cos(ang_r), jnp.cos(ang_c), jnp.cos(ang_c)], axis=-1)
    sin = jnp.concatenate([-jnp.sin(ang_r), jnp.sin(ang_r), -jnp.sin(ang_c), jnp.sin(ang_c)], axis=-1)
    cos = jnp.concatenate([jnp.ones((TM, RET_DK), F32), cos], axis=0)
    sin = jnp.concatenate([jnp.zeros((TM, RET_DK), F32), sin], axis=0)
    return cos, sin


def _mixer_retention(x, modv, norm_g, w_in, decay_logit, out_norm_g, w_out):
    p = _in_proj(x, modv, norm_g, w_in.astype(BF16), w_in.shape[1] // 2)
    cos, sin = _rotary_tables(x.shape[1])
    o = _ret_scan(p, jax.nn.log_sigmoid(decay_logit.astype(F32)), cos, sin)
    return _ret_out(o, p, x, modv, out_norm_g, w_out.astype(BF16))


ROUTE_LANES = 128
ROUTE_COLS = 8
ISSUE_UNROLL = 8
EXPERT_GROUPS = 2
ROW_TILE = 8


def _route_kernel(x_ref, mod_ref, g_ref, w_ref, b_ref, rt_ref, cnt_out_ref, cnt_ref):
    first = (pl.program_id(0) == 0) & (pl.program_id(1) == 0)

    @pl.when(first)
    def _():
        cnt_ref[...] = jnp.zeros_like(cnt_ref)

    m = mod_ref[0, 0]
    h = _rms_mod(x_ref[0], g_ref[...], m[3:4], m[4:5])
    logits = _dot3(h, w_ref[...]) + b_ref[...]
    lane = lax.broadcasted_iota(jnp.int32, (TM, ROUTE_LANES), 1)
    big = jnp.int32(ROUTE_LANES)
    neg = -jnp.inf
    glog = jnp.where((lane >= N_EXPERTS) & (lane < N_EXPERTS + N_GROUPS), logits, neg)
    gmax = jnp.max(glog, axis=-1, keepdims=True)
    gsel = jnp.min(jnp.where(glog == gmax, lane, big), axis=-1, keepdims=True) - N_EXPERTS
    p_group = 1.0 / jnp.sum(jnp.exp(glog - gmax), axis=-1, keepdims=True)
    elog = jnp.where((lane >= gsel * EXPERTS_PER_GROUP) & (lane < (gsel + 1) * EXPERTS_PER_GROUP), logits, neg)
    m1 = jnp.max(elog, axis=-1, keepdims=True)
    i1 = jnp.min(jnp.where(elog == m1, lane, big), axis=-1, keepdims=True)
    elog2 = jnp.where(lane == i1, neg, elog)
    m2 = jnp.max(elog2, axis=-1, keepdims=True)
    i2 = jnp.min(jnp.where(elog2 == m2, lane, big), axis=-1, keepdims=True)
    e2 = jnp.exp(m2 - m1)
    w1 = p_group / (1.0 + e2)
    w2 = p_group * e2 / (1.0 + e2)
    onehot = jnp.where((lane == i1) | (lane == i2), 1.0, 0.0)
    r = lax.broadcasted_iota(jnp.int32, (TM, TM), 0)
    c = lax.broadcasted_iota(jnp.int32, (TM, TM), 1)
    before = _dot(jnp.where(c < r, 1.0, 0.0), onehot) + cnt_ref[...]
    r1 = jnp.sum(jnp.where(lane == i1, before, 0.0), axis=-1, keepdims=True)
    r2 = jnp.sum(jnp.where(lane == i2, before, 0.0), axis=-1, keepdims=True)
    cnt_ref[...] = cnt_ref[...] + jnp.sum(onehot, axis=0, keepdims=True)
    cnt_out_ref[...] = cnt_ref[...]
    vals = (i1.astype(F32), i2.astype(F32), w1, w2, r1, r2)
    out = jnp.zeros((TM, ROUTE_LANES), F32)
    for pos_, val in enumerate(vals):
        out = jnp.where(lane == pos_, val, out)
    rt_ref[0] = out[:, :ROUTE_COLS]


def _route(x, modv, norm_g, w_route, b_route):
    b, s, d = x.shape
    return pl.pallas_call(
        _route_kernel,
        grid=(b, s // TM),
        in_specs=[pl.BlockSpec((1, TM, d), lambda bi, t: (bi, t, 0)),
                  pl.BlockSpec((1, 1, 6, d), _mod_spec()),
                  pl.BlockSpec((1, d), lambda bi, t: (0, 0)),
                  pl.BlockSpec((d, ROUTE_LANES), lambda bi, t: (0, 0)),
                  pl.BlockSpec((1, ROUTE_LANES), lambda bi, t: (0, 0))],
        out_specs=[pl.BlockSpec((1, TM, ROUTE_COLS), lambda bi, t: (bi, t, 0)),
                   pl.BlockSpec((1, ROUTE_LANES), lambda bi, t: (0, 0))],
        out_shape=[jax.ShapeDtypeStruct((b, s, ROUTE_COLS), F32), jax.ShapeDtypeStruct((1, ROUTE_LANES), F32)],
        scratch_shapes=[pltpu.VMEM((1, ROUTE_LANES), F32)],
        compiler_params=_params("arbitrary", "arbitrary"),
        name="moe_route",
    )(x, modv, norm_g.reshape(1, d), w_route, b_route)


def _issue_row_copies(make_copy):
    def body(i, carry):
        for k in range(2):
            make_copy(i, k).start()
        return carry

    lax.fori_loop(0, TM, body, 0, unroll=ISSUE_UNROLL)


def _tile_step():
    step = pl.program_id(0) * pl.num_programs(1) + pl.program_id(1)
    return step, pl.num_programs(0) * pl.num_programs(1), jnp.bitwise_and(step, 1)


def _pack_bf16_pairs(h):
    bits = pltpu.bitcast(h.astype(BF16).astype(F32), jnp.uint32)
    half = h.shape[1] // 2
    return jnp.bitwise_or(jnp.right_shift(bits[:, :half], jnp.uint32(16)),
                          jnp.bitwise_and(bits[:, half:], jnp.uint32(0xFFFF0000)))


def _unpack_bf16_pairs(w):
    lo = pltpu.bitcast(jnp.left_shift(w, jnp.uint32(16)), F32)
    hi = pltpu.bitcast(jnp.bitwise_and(w, jnp.uint32(0xFFFF0000)), F32)
    return jnp.concatenate([lo.astype(BF16), hi.astype(BF16)], axis=1)


def _dispatch_kernel(dest_ref, pe_ref, x_ref, mod_ref, g_ref, xb_ref, h_ref, z_ref, sem, zsem):
    step, n_steps, slot = _tile_step()

    @pl.when(step == 0)
    def _():
        z_ref[...] = jnp.zeros_like(z_ref)

        def clear(e):
            start = pl.multiple_of(pe_ref[e] - MOE_BLOCK, MOE_BLOCK)
            return pltpu.make_async_copy(z_ref, xb_ref.at[pl.ds(start, MOE_BLOCK)], zsem.at[e])

        def used(e):
            return pe_ref[e] > (pe_ref[e - 1] if e else 0)

        for e in range(N_EXPERTS):
            @pl.when(used(e))
            def _():
                clear(e).start()
        for e in range(N_EXPERTS):
            @pl.when(used(e))
            def _():
                clear(e).wait()

        def clear_unused(blk, carry):
            cp = pltpu.make_async_copy(z_ref, xb_ref.at[pl.ds(pl.multiple_of(blk * MOE_BLOCK, MOE_BLOCK), MOE_BLOCK)],
                                       zsem.at[0])
            cp.start()
            cp.wait()
            return carry

        lax.fori_loop(_block_of(pe_ref[N_EXPERTS - 1], MOE_BLOCK), xb_ref.shape[0] // MOE_BLOCK, clear_unused, 0)

    m = mod_ref[0, 0]
    h_ref[slot] = _pack_bf16_pairs(_rms_mod(x_ref[0], g_ref[...], m[3:4], m[4:5]))
    _issue_row_copies(lambda i, k: pltpu.make_async_copy(
        h_ref.at[slot, pl.ds(i, 1)], xb_ref.at[pl.ds(dest_ref[0, 0, 2 * i + k], 1)], sem.at[slot]))

    def wait_tile(sl):
        for _ in range(2):
            pltpu.make_async_copy(h_ref.at[sl], xb_ref.at[pl.ds(0, TM)], sem.at[sl]).wait()

    @pl.when(step > 0)
    def _():
        wait_tile(1 - slot)

    @pl.when(step == n_steps - 1)
    def _():
        wait_tile(slot)


def _dispatch(x, modv, norm_g, dest, pad_end, n_rows):
    b, s, d = x.shape
    nt = s // TM
    return pl.pallas_call(
        _dispatch_kernel,
        grid=(b, nt),
        in_specs=[pl.BlockSpec((1, 1, 2 * TM), lambda bi, t: (bi * nt + t, 0, 0), memory_space=pltpu.SMEM),
                  pl.BlockSpec(memory_space=pltpu.SMEM),
                  pl.BlockSpec((1, TM, d), lambda bi, t: (bi, t, 0)),
                  pl.BlockSpec((1, 1, 6, d), _mod_spec()),
                  pl.BlockSpec((1, d), lambda bi, t: (0, 0))],
        out_specs=pl.BlockSpec(memory_space=pl.ANY),
        out_shape=jax.ShapeDtypeStruct((n_rows, d // 2), jnp.uint32),
        scratch_shapes=[pltpu.VMEM((2, TM, d // 2), jnp.uint32), pltpu.VMEM((MOE_BLOCK, d // 2), jnp.uint32),
                        pltpu.SemaphoreType.DMA((2,)), pltpu.SemaphoreType.DMA((N_EXPERTS,))],
        compiler_params=_params("arbitrary", "arbitrary"),
        name="moe_dispatch",
    )(dest.reshape(b * nt, 1, 2 * TM), pad_end, x, modv, norm_g.reshape(1, d))


def _expert_kernel(be_ref, nu_ref, x_ref, wg_ref, wu_ref, wd_ref, o_ref, wg_bf, wu_bf, wd_bf):
    i = pl.program_id(0)

    @pl.when(i < nu_ref[0])
    def _():
        @pl.when((i == 0) | (be_ref[i] != be_ref[jnp.maximum(i - 1, 0)]))
        def _():
            wg_bf[...] = wg_ref[0, 0].astype(BF16)
            wu_bf[...] = wu_ref[0, 0].astype(BF16)
            wd_bf[...] = wd_ref[0, 0].astype(BF16)

        rows = MOE_BLOCK // EXPERT_GROUPS
        xs = [_unpack_bf16_pairs(x_ref[g * rows:(g + 1) * rows]) for g in range(EXPERT_GROUPS)]
        gate_up = [(jnp.dot(xg, wg_bf[...], preferred_element_type=F32), jnp.dot(xg, wu_bf[...], preferred_element_type=F32))
                   for xg in xs]
        acts = [(_silu(a) * u).astype(BF16) for a, u in gate_up]
        ys = [jnp.dot(act, wd_bf[...], preferred_element_type=F32) for act in acts]
        for g, y in enumerate(ys):
            for j in range(ROW_TILE):
                o_ref[pl.ds(g * rows * ROW_TILE + j, rows, stride=ROW_TILE), :] = y[:, j * 128:(j + 1) * 128]

    @pl.when(i >= nu_ref[0])
    def _():
        o_ref[...] = jnp.zeros_like(o_ref)


def _experts(xb, block_e, n_used, layer, w_gate, w_up, w_down):
    n_rows = xb.shape[0]
    d, de = w_gate.shape[-2:]
    nb = n_rows // MOE_BLOCK

    def last_used(i, nu):
        return jnp.minimum(i, nu[0] - 1)

    grid_spec = pltpu.PrefetchScalarGridSpec(
        num_scalar_prefetch=2,
        grid=(nb,),
        in_specs=[pl.BlockSpec((MOE_BLOCK, d // 2), lambda i, be, nu: (last_used(i, nu), 0)),
                  pl.BlockSpec((1, 1, d, de), lambda i, be, nu: (layer, be[last_used(i, nu)], 0, 0)),
                  pl.BlockSpec((1, 1, d, de), lambda i, be, nu: (layer, be[last_used(i, nu)], 0, 0)),
                  pl.BlockSpec((1, 1, de, d), lambda i, be, nu: (layer, be[last_used(i, nu)], 0, 0))],
        out_specs=pl.BlockSpec((MOE_BLOCK * ROW_TILE, d // ROW_TILE), lambda i, be, nu: (i, 0)),
        scratch_shapes=[pltpu.VMEM((d, de), BF16), pltpu.VMEM((d, de), BF16), pltpu.VMEM((de, d), BF16)],
    )
    assert d == ROW_TILE * 128
    return pl.pallas_call(
        _expert_kernel,
        grid_spec=grid_spec,
        out_shape=jax.ShapeDtypeStruct((n_rows * ROW_TILE, d // ROW_TILE), F32),
        compiler_params=_params("arbitrary"),
        name="moe_experts",
    )(block_e, n_used, xb, w_gate, w_up, w_down)


def _combine_kernel(dest_ref, next_dest_ref, x_ref, mod_ref, rt_ref, fg_ref, yb_ref, o_ref, y_ref, sem, *, final_norm):
    step, n_steps, slot = _tile_step()

    rt_ = ROW_TILE
    slot_rows = 2 * TM * rt_

    def gather(d_ref, sl):
        _issue_row_copies(lambda i, k: pltpu.make_async_copy(
            yb_ref.at[pl.ds(pl.multiple_of(d_ref[0, 0, 2 * i + k] * rt_, rt_), rt_)],
            y_ref.at[pl.ds(pl.multiple_of(sl * slot_rows + (k * TM + i) * rt_, rt_), rt_)], sem.at[sl]))

    @pl.when(step == 0)
    def _():
        gather(dest_ref, 0)

    @pl.when(step + 1 < n_steps)
    def _():
        gather(next_dest_ref, 1 - slot)

    base = pl.multiple_of(slot * slot_rows, slot_rows)
    pltpu.make_async_copy(yb_ref.at[pl.ds(0, slot_rows)], y_ref.at[pl.ds(base, slot_rows)], sem.at[slot]).wait()
    m = mod_ref[0, 0]
    rt = rt_ref[0]
    x = x_ref[0]
    parts = []
    for j in range(rt_):
        cols = slice(j * 128, (j + 1) * 128)
        y1 = y_ref[pl.ds(base + j, TM, stride=rt_), :]
        y2 = y_ref[pl.ds(base + TM * rt_ + j, TM, stride=rt_), :]
        parts.append(x[:, cols] + m[5:6, cols] * (rt[:, 2:3] * y1 + rt[:, 3:4] * y2))
    if final_norm:
        ms = sum(jnp.sum(p * p, axis=-1, keepdims=True) for p in parts) / x.shape[1]
        scale = lax.rsqrt(ms + EPS)
        parts = [p * scale * fg_ref[:, j * 128:(j + 1) * 128] for j, p in enumerate(parts)]
    for j, p in enumerate(parts):
        o_ref[0, :, j * 128:(j + 1) * 128] = p


def _combine(x, modv, rt, dest, yb, final_g, final_norm):
    b, s, d = x.shape
    nt = s // TM
    skip = 1 if final_norm else 0

    def next_tile(bi, t):
        wrap = t + 1 >= nt - skip
        nb_ = jnp.minimum(jnp.where(wrap, bi + 1, bi), b - 1)
        return (nb_ * nt + jnp.where(wrap, 0, t + 1) + skip, 0, 0)

    return pl.pallas_call(
        functools.partial(_combine_kernel, final_norm=final_norm),
        grid=(b, nt - skip),
        in_specs=[pl.BlockSpec((1, 1, 2 * TM), lambda bi, t: (bi * nt + t + skip, 0, 0), memory_space=pltpu.SMEM),
                  pl.BlockSpec((1, 1, 2 * TM), next_tile, memory_space=pltpu.SMEM),
                  pl.BlockSpec((1, TM, d), lambda bi, t: (bi, t + skip, 0)),
                  pl.BlockSpec((1, 1, 6, d), lambda bi, t: (bi, jnp.minimum(t + skip, 1), 0, 0)),
                  pl.BlockSpec((1, TM, ROUTE_COLS), lambda bi, t: (bi, t + skip, 0)),
                  pl.BlockSpec((1, d), lambda bi, t: (0, 0)),
                  pl.BlockSpec(memory_space=pl.ANY)],
        out_specs=pl.BlockSpec((1, TM, d), lambda bi, t: (bi, t, 0)),
        out_shape=jax.ShapeDtypeStruct((b, s - skip * TM, d), F32),
        scratch_shapes=[pltpu.VMEM((2 * 2 * TM * ROW_TILE, d // ROW_TILE), F32), pltpu.SemaphoreType.DMA((2,))],
        compiler_params=_params("arbitrary", "arbitrary"),
        name="moe_combine",
    )(dest.reshape(b * nt, 1, 2 * TM), dest.reshape(b * nt, 1, 2 * TM), x, modv, rt, final_g.reshape(1, d), yb)


def _hier_moe(x, modv, norm_g, layer, wg_r, bg_r, we_r, be_r, w_gate, w_up, w_down, final_g, final_norm):
    b, s, d = x.shape
    n_tok = b * s
    pad = ROUTE_LANES - N_EXPERTS - N_GROUPS
    w_route = jnp.concatenate([we_r, wg_r, jnp.zeros((d, pad), F32)], axis=1)
    b_route = jnp.concatenate([be_r, bg_r, jnp.zeros((pad,), F32)]).reshape(1, ROUTE_LANES)
    rt, cnt = _route(x, modv, norm_g, w_route, b_route)
    counts = cnt[0, :N_EXPERTS].astype(jnp.int32)
    padded = (counts + MOE_BLOCK - 1) // MOE_BLOCK * MOE_BLOCK
    pad_end = jnp.cumsum(padded)
    pad_start = pad_end - padded
    n_blocks = -(-(2 * n_tok + N_EXPERTS * (MOE_BLOCK - 1)) // MOE_BLOCK)
    rt2 = rt.reshape(n_tok, ROUTE_COLS)
    expert = rt2[:, 0:2].astype(jnp.int32)
    experts = jnp.arange(N_EXPERTS, dtype=jnp.int32)
    start_of = jnp.sum(jnp.where(expert[..., None] == experts, pad_start, 0), axis=-1)
    dest = (start_of + rt2[:, 4:6].astype(jnp.int32)).reshape(-1)
    block_start = jnp.arange(n_blocks, dtype=jnp.int32) * MOE_BLOCK
    block_e = jnp.minimum(jnp.sum((pad_end[None, :] <= block_start[:, None]).astype(jnp.int32), axis=1), N_EXPERTS - 1)
    n_used = (pad_end[-1:] // MOE_BLOCK).astype(jnp.int32)
    xb = _dispatch(x, modv, norm_g, dest, pad_end.astype(jnp.int32), n_blocks * MOE_BLOCK)
    yb = _experts(xb, block_e, n_used, layer, w_gate, w_up, w_down)
    return _combine(x, modv, rt, dest, yb, final_g, final_norm)


def kernel(x, c, ctx, c_ctx, mod_w, mod_b, norm1_g, norm2_g, final_g, gdn_w_in, gdn_conv_w, gdn_a_log, gdn_dt_bias, gdn_norm_g, gdn_w_out, pool_w, pool_b, pool_scale, ret_w_in, ret_decay_logit, ret_norm_g, ret_w_out, router_group_w, router_group_b, router_expert_w, router_expert_b, exp_w_gate, exp_w_up, exp_w_down):
    b, n_lat, d = x.shape
    depth = mod_w.shape[0]
    assert ctx.shape[1] == TM and n_lat % TM == 0 and b < 16
    xs = jnp.concatenate([ctx, x], axis=1)
    cvec = jnp.concatenate([c, c_ctx[None], jnp.zeros((15 - b, d), F32)], axis=0)
    mods = _modulation(cvec, mod_w, mod_b)
    for i in range(depth):
        j, kind = i // 3, i % 3
        lat = mods[i, :b].reshape(b, 1, 6, d)
        con = jnp.broadcast_to(mods[i, b].reshape(1, 1, 6, d), (b, 1, 6, d))
        modv = jnp.concatenate([con, lat], axis=1)
        if kind == 0:
            xs = _mixer_gdn(xs, modv, norm1_g[i], gdn_w_in[j], gdn_conv_w[j], gdn_a_log[j], gdn_dt_bias[j],
                            gdn_norm_g[j], gdn_w_out[j])
        elif kind == 1:
            xs = _mixer_pool(xs, modv, norm1_g[i], pool_w[j], pool_b[j], pool_scale[j])
        else:
            xs = _mixer_retention(xs, modv, norm1_g[i], ret_w_in[j], ret_decay_logit[j], ret_norm_g[j], ret_w_out[j])
        xs = _hier_moe(xs, modv, norm2_g[i], i, router_group_w[i], router_group_b[i], router_expert_w[i],
                       router_expert_b[i], exp_w_gate, exp_w_up, exp_w_down, final_g, i == depth - 1)
    return xs
```
